```python
import jax, jax.numpy as jnp
from jax import lax
import numpy as np

D_MODEL = 2048
BATCH = 8
SEQ = 4096
DEPTH = 1

PLE_DIM = 256
ROPE_THETA = 10000.0
LN_EPS = 1e-5
RMS_EPS = 1e-6
NEG_INF = -1e30
Q_BLK = 128

NSA_HEADS = 8
NSA_GROUPS = 2
NSA_HPG = NSA_HEADS // NSA_GROUPS
NSA_DK = 128
NSA_DV = 128
CMP_BLK = 32
CMP_STRIDE = 16
CMP_HID = 256
SLC_BLK = 64
N_SEL = 16
WIN = 512
SEL_Q_BLK = 32
FORCE_SCORE = 1e4

MLA_HEADS = 8
MLA_Q_RANK = 768
MLA_KV_RANK = 512
MLA_NOPE = 128
MLA_ROPE = 64
MLA_DV = 128

N_EXPERTS = 64
TOP_K = 6
N_EXPERT_GROUPS = 8
TOPK_GROUPS = 4
D_EXPERT = 1408
ROUTED_SCALE = 2.5
MOE_ROW_BLK = 512

W_NSA_Q = NSA_HEADS * NSA_DK
W_NSA_KV = 3 * NSA_GROUPS * (NSA_DK + NSA_DV)
W_NSA_GATE = 3 * NSA_HEADS
W_MLA_CQ = MLA_Q_RANK
W_MLA_CKV = MLA_KV_RANK
W_MLA_KR = MLA_ROPE
W_MERGE = 2 * D_MODEL
D_IN = W_NSA_Q + W_NSA_KV + W_NSA_GATE + W_MLA_CQ + W_MLA_CKV + W_MLA_KR + W_MERGE

DN_ALPHA = (2.0 * DEPTH) ** 0.25
DN_BETA = (8.0 * DEPTH) ** -0.25

kernel_name = 'hybrid_nsa_mla_moe_deepnorm'


def _layernorm(z, g, b):
    zf = z.astype(jnp.float32)
    mu = jnp.mean(zf, -1, keepdims=True)
    var = jnp.mean(jnp.square(zf - mu), -1, keepdims=True)
    return ((zf - mu) * lax.rsqrt(var + LN_EPS) * g + b).astype(z.dtype)


def _rmsnorm(z, g):
    zf = z.astype(jnp.float32)
    return (zf * lax.rsqrt(jnp.mean(jnp.square(zf), -1, keepdims=True) + RMS_EPS) * g).astype(z.dtype)


def _rope(z, pos):
    d = z.shape[-1]
    half = d // 2
    inv = jnp.power(ROPE_THETA, -jnp.arange(half, dtype=jnp.float32) * 2.0 / d)
    ang = pos.astype(jnp.float32)[:, None] * inv[None, :]
    shp = (pos.shape[0],) + (1,) * (z.ndim - 3) + (half,)
    cos = jnp.cos(ang).reshape(shp)
    sin = jnp.sin(ang).reshape(shp)
    zf = z.astype(jnp.float32)
    z1, z2 = zf[..., :half], zf[..., half:]
    return jnp.concatenate([z1 * cos - z2 * sin, z2 * cos + z1 * sin], -1).astype(z.dtype)


def _unblock(o):
    nb, b, q = o.shape[:3]
    return jnp.moveaxis(o, 0, 1).reshape((b, nb * q) + o.shape[3:])


def _nsa_compress(z, pe, w1, w2):
    b, s, g, d = z.shape
    n = (s - CMP_BLK) // CMP_STRIDE + 1
    idx = jnp.arange(n)[:, None] * CMP_STRIDE + jnp.arange(CMP_BLK)[None, :]
    blk = z[:, idx] + pe[None, None, :, None, :].astype(z.dtype)
    blk = jnp.swapaxes(blk, 2, 3).reshape(b, n, g, CMP_BLK * d)
    return jax.nn.gelu(blk @ w1) @ w2


def _nsa_selected(qg, k_s, v_s, sel, scale):
    B, S, G, Hg, _ = qg.shape
    n_sel = sel.shape[-1]
    n_slc = S // SLC_BLK
    kb = k_s.reshape(B, n_slc, SLC_BLK, G, k_s.shape[-1]).transpose(0, 3, 1, 2, 4)
    vb = v_s.reshape(B, n_slc, SLC_BLK, G, v_s.shape[-1]).transpose(0, 3, 1, 2, 4)
    bi = jnp.arange(B)[:, None, None, None]
    gi = jnp.arange(G)[None, :, None, None]
    offs = jnp.arange(SLC_BLK)

    def blk(i):
        s0 = i * SEL_Q_BLK
        qb = lax.dynamic_slice_in_dim(qg, s0, SEL_Q_BLK, axis=1)
        ib = lax.dynamic_slice_in_dim(sel, s0, SEL_Q_BLK, axis=2)
        kk = kb[bi, gi, ib]
        vv = vb[bi, gi, ib]
        s = jnp.einsum('bqghd,bgqnld->bghqnl', qb, kk, preferred_element_type=jnp.float32) * scale
        kpos = ib[..., None] * SLC_BLK + offs
        qpos = s0 + jnp.arange(SEL_Q_BLK)
        m = (kpos <= qpos[None, None, :, None, None])[:, :, None]
        s = jnp.where(m, s, NEG_INF).reshape(B, G, Hg, SEL_Q_BLK, n_sel * SLC_BLK)
        pr = jax.nn.softmax(s, axis=-1).astype(vv.dtype).reshape(B, G, Hg, SEL_Q_BLK, n_sel, SLC_BLK)
        return jnp.einsum('bghqnl,bgqnld->bqghd', pr, vv)

    return _unblock(lax.map(blk, jnp.arange(S // SEL_Q_BLK)))


def _nsa_window(qg, k_w, v_w, scale):
    B, S, G, Hg, _ = qg.shape
    pad = ((0, 0), (WIN, 0), (0, 0), (0, 0))
    kp = jnp.pad(k_w, pad)
    vp = jnp.pad(v_w, pad)
    span = Q_BLK + WIN

    def blk(i):
        s0 = i * Q_BLK
        qb = lax.dynamic_slice_in_dim(qg, s0, Q_BLK, axis=1)
        kk = lax.dynamic_slice_in_dim(kp, s0, span, axis=1)
        vv = lax.dynamic_slice_in_dim(vp, s0, span, axis=1)
        s = jnp.einsum('bqghd,bkgd->bghqk', qb, kk, preferred_element_type=jnp.float32) * scale
        qpos = (s0 + jnp.arange(Q_BLK))[:, None]
        kpos = (s0 - WIN + jnp.arange(span))[None, :]
        m = (kpos <= qpos) & (kpos > qpos - WIN) & (kpos >= 0)
        pr = jax.nn.softmax(jnp.where(m, s, NEG_INF), axis=-1).astype(vv.dtype)
        return jnp.einsum('bghqk,bkgd->bqghd', pr, vv)

    return _unblock(lax.map(blk, jnp.arange(S // Q_BLK)))


def _nsa(q, k_c, v_c, k_s, v_s, k_w, v_w, gates, pos, pe_k, pe_v, wk1, wk2, wv1, wv2):
    B, S = q.shape[:2]
    dt = q.dtype
    scale = NSA_DK ** -0.5
    qg = q.reshape(B, S, NSA_GROUPS, NSA_HPG, NSA_DK)

    n_cmp = (S - CMP_BLK) // CMP_STRIDE + 1
    pos_c = jnp.arange(n_cmp) * CMP_STRIDE + (CMP_BLK - 1)
    kc = _rope(_nsa_compress(k_c, pe_k, wk1, wk2), pos_c)
    vc = _nsa_compress(v_c, pe_v, wv1, wv2)
    sc = jnp.einsum('bsghd,bngd->bghsn', qg, kc, preferred_element_type=jnp.float32) * scale
    mask_c = pos_c[None, :] <= pos[:, None]
    p_cmp = jax.nn.softmax(jnp.where(mask_c, sc, NEG_INF), axis=-1) * mask_c
    o_cmp = jnp.einsum('bghsn,bngd->bsghd', p_cmp.astype(dt), vc)

    n_slc = S // SLC_BLK
    ratio = SLC_BLK // CMP_STRIDE
    span = CMP_BLK // CMP_STRIDE
    n_terms = ratio + span - 1
    need = ratio * (n_slc - 1) + n_terms
    imp = jnp.sum(p_cmp, axis=2)
    imp = jnp.pad(imp, ((0, 0), (0, 0), (0, 0), (span - 1, max(0, need - (span - 1) - n_cmp))))
    p_slc = sum(imp[..., r: r + ratio * (n_slc - 1) + 1: ratio] for r in range(n_terms))

    blk_id = jnp.arange(n_slc)[None, :]
    cur = (pos // SLC_BLK)[:, None]
    forced = (blk_id == 0) | (blk_id == cur) | (blk_id == cur - 1)
    valid = blk_id * SLC_BLK <= pos[:, None]
    score = jnp.where(forced, FORCE_SCORE, jnp.where(valid, p_slc, -FORCE_SCORE))
    _, sel = lax.top_k(score, min(N_SEL, n_slc))
    o_slc = _nsa_selected(qg, k_s, v_s, sel, scale)

    o_win = _nsa_window(qg, k_w, v_w, scale)

    g = jax.nn.sigmoid(gates).reshape(B, S, NSA_GROUPS, NSA_HPG, 3)
    o = g[..., 0:1] * o_cmp + g[..., 1:2] * o_slc + g[..., 2:3] * o_win
    return o.reshape(B, S, NSA_HEADS * NSA_DV)


def _mla_attention(q_nope, q_rope, k_nope, k_rope, v):
    B, S, H, _ = q_nope.shape
    scale = (MLA_NOPE + MLA_ROPE) ** -0.5
    kpos = jnp.arange(S)[None, :]

    def blk(i):
        s0 = i * Q_BLK
        qn = lax.dynamic_slice_in_dim(q_nope, s0, Q_BLK, axis=1)
        qr = lax.dynamic_slice_in_dim(q_rope, s0, Q_BLK, axis=1)
        s = (jnp.einsum('bqhd,bkhd->bhqk', qn, k_nope, preferred_element_type=jnp.float32)
             + jnp.einsum('bqhd,bkd->bhqk', qr, k_rope, preferred_element_type=jnp.float32)) * scale
        qpos = (s0 + jnp.arange(Q_BLK))[:, None]
        pr = jax.nn.softmax(jnp.where(kpos <= qpos, s, NEG_INF), axis=-1).astype(v.dtype)
        return jnp.einsum('bhqk,bkhd->bqhd', pr, v)

    o = _unblock(lax.map(blk, jnp.arange(S // Q_BLK)))
    return o.reshape(B, S, H * MLA_DV)


def _moe(x2, router_w, router_b, w_gate, w_up, w_down, sh_gate, sh_up, sh_down):
    T, D = x2.shape
    f32 = jnp.float32
    scores = jax.nn.sigmoid(jnp.matmul(x2, router_w, preferred_element_type=f32))
    biased = (scores + router_b.astype(f32)).reshape(T, N_EXPERT_GROUPS, N_EXPERTS // N_EXPERT_GROUPS)
    grp_score = jnp.sum(lax.top_k(biased, 2)[0], axis=-1)
    gsel = lax.top_k(grp_score, TOPK_GROUPS)[1]
    gmask = jnp.any(gsel[..., None] == jnp.arange(N_EXPERT_GROUPS), axis=1)
    biased = jnp.where(gmask[..., None], biased, NEG_INF).reshape(T, N_EXPERTS)
    eidx = lax.top_k(biased, TOP_K)[1]
    wts = jnp.take_along_axis(scores, eidx, axis=1)
    wts = wts / jnp.sum(wts, axis=-1, keepdims=True) * ROUTED_SCALE

    n = T * TOP_K
    flat_e = eidx.reshape(n)
    flat_t = jnp.arange(n, dtype=jnp.int32) // TOP_K
    flat_w = wts.reshape(n)
    order = jnp.argsort(flat_e)
    se, st, sw = flat_e[order], flat_t[order], flat_w[order]
    counts = jnp.zeros((N_EXPERTS,), jnp.int32).at[flat_e].add(1)
    padded = (counts + MOE_ROW_BLK - 1) // MOE_ROW_BLK * MOE_ROW_BLK
    pad_end = jnp.cumsum(padded)
    pad_start = pad_end - padded
    raw_start = jnp.cumsum(counts) - counts
    dest = pad_start[se] + jnp.arange(n, dtype=jnp.int32) - raw_start[se]
    n_blk = -(-n // MOE_ROW_BLK) + N_EXPERTS
    tok = jnp.full((n_blk * MOE_ROW_BLK,), T, jnp.int32).at[dest].set(st)
    gw = jnp.zeros((n_blk * MOE_ROW_BLK,), f32).at[dest].set(sw)
    blk_e = jnp.minimum(jnp.searchsorted(pad_end, jnp.arange(n_blk) * MOE_ROW_BLK, side='right'), N_EXPERTS - 1)

    def step(acc, xs):
        t, w, e = xs
        xb = x2[jnp.minimum(t, T - 1)]
        h = jax.nn.silu(xb @ w_gate[e]) * (xb @ w_up[e])
        y = (h @ w_down[e]).astype(f32) * w[:, None]
        return acc.at[t].add(y, mode='drop'), None

    routed, _ = lax.scan(step, jnp.zeros((T, D), f32),
                         (tok.reshape(n_blk, MOE_ROW_BLK), gw.reshape(n_blk, MOE_ROW_BLK), blk_e))
    shared = (jax.nn.silu(x2 @ sh_gate) * (x2 @ sh_up)) @ sh_down
    return (routed + shared.astype(f32)).astype(x2.dtype)


def setup_inputs(seed: int = 0) -> dict:
    key = jax.random.key(seed)
    ks = iter(jax.random.split(key, 40))

    def nrm(shape, scale):
        return jax.random.normal(next(ks), shape, jnp.float32) * scale

    L, D, E, F = DEPTH, D_MODEL, N_EXPERTS, D_EXPERT
    return {
        'x': nrm((BATCH, SEQ, D), 1.0),
        'p': nrm((L, BATCH, SEQ, PLE_DIM), 1.0),
        'w_in': nrm((L, D, D_IN), D ** -0.5),
        'nsa_pe_k': nrm((L, CMP_BLK, NSA_DK), 0.1),
        'nsa_pe_v': nrm((L, CMP_BLK, NSA_DV), 0.1),
        'nsa_cmp_k_w1': nrm((L, CMP_BLK * NSA_DK, CMP_HID), (CMP_BLK * NSA_DK) ** -0.5),
        'nsa_cmp_k_w2': nrm((L, CMP_HID, NSA_DK), CMP_HID ** -0.5),
        'nsa_cmp_v_w1': nrm((L, CMP_BLK * NSA_DV, CMP_HID), (CMP_BLK * NSA_DV) ** -0.5),
        'nsa_cmp_v_w2': nrm((L, CMP_HID, NSA_DV), CMP_HID ** -0.5),
        'mla_q_norm': 1.0 + nrm((L, MLA_Q_RANK), 0.02),
        'mla_w_uq': nrm((L, MLA_Q_RANK, MLA_HEADS * (MLA_NOPE + MLA_ROPE)), MLA_Q_RANK ** -0.5),
        'mla_kv_norm': 1.0 + nrm((L, MLA_KV_RANK), 0.02),
        'mla_w_uk': nrm((L, MLA_KV_RANK, MLA_HEADS * MLA_NOPE), MLA_KV_RANK ** -0.5),
        'mla_w_uv': nrm((L, MLA_KV_RANK, MLA_HEADS * MLA_DV), MLA_KV_RANK ** -0.5),
        'w_branch_nsa': nrm((L, NSA_HEADS * NSA_DV, D), (NSA_HEADS * NSA_DV) ** -0.5 * DN_BETA),
        'w_branch_mla': nrm((L, MLA_HEADS * MLA_DV, D), (MLA_HEADS * MLA_DV) ** -0.5 * DN_BETA),
        'w_out': nrm((L, D, D), D ** -0.5 * DN_BETA),
        'ln1_g': 1.0 + nrm((L, D), 0.02),
        'ln1_b': nrm((L, D), 0.02),
        'router_w': nrm((L, D, E), D ** -0.5),
        'router_b': nrm((L, E), 0.01),
        'exp_w_gate': nrm((L, E, D, F), D ** -0.5),
        'exp_w_up': nrm((L, E, D, F), D ** -0.5),
        'exp_w_down': nrm((L, E, F, D), F ** -0.5 * DN_BETA),
        'sh_w_gate': nrm((L, D, F), D ** -0.5),
        'sh_w_up': nrm((L, D, F), D ** -0.5),
        'sh_w_down': nrm((L, F, D), F ** -0.5 * DN_BETA),
        'ln2_g': 1.0 + nrm((L, D), 0.02),
        'ln2_b': nrm((L, D), 0.02),
        'ple_w_proj': nrm((L, PLE_DIM, D), PLE_DIM ** -0.5),
        'ple_w_gate': nrm((L, D, D), D ** -0.5),
    }


def reference(x, p, w_in, nsa_pe_k, nsa_pe_v, nsa_cmp_k_w1, nsa_cmp_k_w2, nsa_cmp_v_w1, nsa_cmp_v_w2,
              mla_q_norm, mla_w_uq, mla_kv_norm, mla_w_uk, mla_w_uv,
              w_branch_nsa, w_branch_mla, w_out, ln1_g, ln1_b,
              router_w, router_b, exp_w_gate, exp_w_up, exp_w_down,
              sh_w_gate, sh_w_up, sh_w_down, ln2_g, ln2_b, ple_w_proj, ple_w_gate):
    B, S, D = x.shape
    pos = jnp.arange(S)
    c0 = W_NSA_Q
    c1 = c0 + W_NSA_KV
    c2 = c1 + W_NSA_GATE
    c3 = c2 + W_MLA_CQ
    c4 = c3 + W_MLA_CKV
    c5 = c4 + W_MLA_KR
    gk = NSA_GROUPS * NSA_DK
    for i in range(DEPTH):
        proj = x @ w_in[i]
        q_a, kv_a, g_a, cq, ckv, kr, g_m = jnp.split(proj, [c0, c1, c2, c3, c4, c5], axis=-1)

        q_a = _rope(q_a.reshape(B, S, NSA_HEADS, NSA_DK), pos)
        kv_a = kv_a.reshape(B, S, 3, NSA_GROUPS * (NSA_DK + NSA_DV))
        k_c = kv_a[:, :, 0, :gk].reshape(B, S, NSA_GROUPS, NSA_DK)
        v_c = kv_a[:, :, 0, gk:].reshape(B, S, NSA_GROUPS, NSA_DV)
        k_s = _rope(kv_a[:, :, 1, :gk].reshape(B, S, NSA_GROUPS, NSA_DK), pos)
        v_s = kv_a[:, :, 1, gk:].reshape(B, S, NSA_GROUPS, NSA_DV)
        k_w = _rope(kv_a[:, :, 2, :gk].reshape(B, S, NSA_GROUPS, NSA_DK), pos)
        v_w = kv_a[:, :, 2, gk:].reshape(B, S, NSA_GROUPS, NSA_DV)
        y_nsa = _nsa(q_a, k_c, v_c, k_s, v_s, k_w, v_w, g_a, pos,
                     nsa_pe_k[i], nsa_pe_v[i], nsa_cmp_k_w1[i], nsa_cmp_k_w2[i],
                     nsa_cmp_v_w1[i], nsa_cmp_v_w2[i])

        qm = (_rmsnorm(cq, mla_q_norm[i]) @ mla_w_uq[i]).reshape(B, S, MLA_HEADS, MLA_NOPE + MLA_ROPE)
        q_nope = qm[..., :MLA_NOPE]
        q_rope = _rope(qm[..., MLA_NOPE:], pos)
        ckv = _rmsnorm(ckv, mla_kv_norm[i])
        k_nope = (ckv @ mla_w_uk[i]).reshape(B, S, MLA_HEADS, MLA_NOPE)
        v_m = (ckv @ mla_w_uv[i]).reshape(B, S, MLA_HEADS, MLA_DV)
        k_rope = _rope(kr, pos)
        y_mla = _mla_attention(q_nope, q_rope, k_nope, k_rope, v_m)

        g_nsa = jax.nn.sigmoid(g_m[..., :D])
        g_mla = jax.nn.sigmoid(g_m[..., D:])
        merged = g_nsa * (y_nsa @ w_branch_nsa[i]) + g_mla * (y_mla @ w_branch_mla[i])
        x = _layernorm(DN_ALPHA * x + merged @ w_out[i], ln1_g[i], ln1_b[i])

        h = _moe(x.reshape(B * S, D), router_w[i], router_b[i], exp_w_gate[i], exp_w_up[i],
                 exp_w_down[i], sh_w_gate[i], sh_w_up[i], sh_w_down[i]).reshape(B, S, D)
        x = _layernorm(DN_ALPHA * x + h, ln2_g[i], ln2_b[i])

        x = x + jax.nn.sigmoid(x @ ple_w_gate[i]) * (p[i] @ ple_w_proj[i])
    return x
```

```python
import functools

import jax
import jax.numpy as jnp
from jax import lax
from jax.experimental import pallas as pl
from jax.experimental.pallas import tpu as pltpu

D_MODEL = 2048
BATCH = 8
SEQ = 4096
DEPTH = 1

PLE_DIM = 256
ROPE_THETA = 10000.0
LN_EPS = 1e-5
RMS_EPS = 1e-6
NEG_INF = -1e30
Q_BLK = 128

NSA_HEADS = 8
NSA_GROUPS = 2
NSA_HPG = NSA_HEADS // NSA_GROUPS
NSA_DK = 128
NSA_DV = 128
CMP_BLK = 32
CMP_STRIDE = 16
CMP_HID = 256
SLC_BLK = 64
N_SEL = 16
WIN = 512
SEL_Q_BLK = 32
FORCE_SCORE = 1e4

MLA_HEADS = 8
MLA_Q_RANK = 768
MLA_KV_RANK = 512
MLA_NOPE = 128
MLA_ROPE = 64
MLA_DV = 128

N_EXPERTS = 64
TOP_K = 6
N_EXPERT_GROUPS = 8
TOPK_GROUPS = 4
D_EXPERT = 1408
ROUTED_SCALE = 2.5
MOE_ROW_BLK = 512

W_NSA_Q = NSA_HEADS * NSA_DK
W_NSA_KV = 3 * NSA_GROUPS * (NSA_DK + NSA_DV)
W_NSA_GATE = 3 * NSA_HEADS
W_MLA_CQ = MLA_Q_RANK
W_MLA_CKV = MLA_KV_RANK
W_MLA_KR = MLA_ROPE
W_MERGE = 2 * D_MODEL
D_IN = W_NSA_Q + W_NSA_KV + W_NSA_GATE + W_MLA_CQ + W_MLA_CKV + W_MLA_KR + W_MERGE

DN_ALPHA = (2.0 * DEPTH) ** 0.25
DN_BETA = (8.0 * DEPTH) ** -0.25

V7X_VMEM_LIMIT_BYTES = 48 * 1024 * 1024


def _mm_kernel(a_ref, b_ref, o_ref):
    a = a_ref[...].astype(jnp.bfloat16)
    o_ref[...] = jnp.dot(a, b_ref[...], preferred_element_type=jnp.float32).astype(o_ref.dtype)


def _mm(a, b, tm=512, tn=512):
    m, k = a.shape
    _, n = b.shape
    n_pad = -(-n // tn) * tn
    b = b.astype(jnp.bfloat16)
    if n_pad != n:
        b = jnp.pad(b, ((0, 0), (0, n_pad - n)))
    out = pl.pallas_call(
        _mm_kernel,
        grid=(m // tm, n_pad // tn),
        in_specs=[pl.BlockSpec((tm, k), lambda i, j: (i, 0)),
                  pl.BlockSpec((k, tn), lambda i, j: (0, j))],
        out_specs=pl.BlockSpec((tm, tn), lambda i, j: (i, j)),
        out_shape=jax.ShapeDtypeStruct((m, n_pad), jnp.float32),
        compiler_params=pltpu.CompilerParams(
            dimension_semantics=("parallel", "arbitrary"),
            vmem_limit_bytes=V7X_VMEM_LIMIT_BYTES),
        name="mm",
    )(a, b)
    return out[:, :n] if n_pad != n else out


def _layernorm(z, g, b):
    zf = z.astype(jnp.float32)
    mu = jnp.mean(zf, -1, keepdims=True)
    var = jnp.mean(jnp.square(zf - mu), -1, keepdims=True)
    return ((zf - mu) * lax.rsqrt(var + LN_EPS) * g + b).astype(z.dtype)


def _rmsnorm(z, g):
    zf = z.astype(jnp.float32)
    return (zf * lax.rsqrt(jnp.mean(jnp.square(zf), -1, keepdims=True) + RMS_EPS) * g).astype(z.dtype)


def _rope(z, pos):
    d = z.shape[-1]
    half = d // 2
    inv = jnp.power(ROPE_THETA, -jnp.arange(half, dtype=jnp.float32) * 2.0 / d)
    ang = pos.astype(jnp.float32)[:, None] * inv[None, :]
    shp = (pos.shape[0],) + (1,) * (z.ndim - 3) + (half,)
    cos = jnp.cos(ang).reshape(shp)
    sin = jnp.sin(ang).reshape(shp)
    zf = z.astype(jnp.float32)
    z1, z2 = zf[..., :half], zf[..., half:]
    return jnp.concatenate([z1 * cos - z2 * sin, z2 * cos + z1 * sin], -1).astype(z.dtype)


def _unblock(o):
    nb, b, q = o.shape[:3]
    return jnp.moveaxis(o, 0, 1).reshape((b, nb * q) + o.shape[3:])


def _nsa_compress(z, pe, w1, w2):
    b, s, g, d = z.shape
    n = (s - CMP_BLK) // CMP_STRIDE + 1
    idx = jnp.arange(n)[:, None] * CMP_STRIDE + jnp.arange(CMP_BLK)[None, :]
    blk = z[:, idx] + pe[None, None, :, None, :].astype(z.dtype)
    blk = jnp.swapaxes(blk, 2, 3).reshape(b, n, g, CMP_BLK * d)
    return jax.nn.gelu(blk @ w1) @ w2


def _nsa_selected(qg, k_s, v_s, sel, scale):
    B, S, G, Hg, _ = qg.shape
    n_sel = sel.shape[-1]
    n_slc = S // SLC_BLK
    kb = k_s.reshape(B, n_slc, SLC_BLK, G, k_s.shape[-1]).transpose(0, 3, 1, 2, 4)
    vb = v_s.reshape(B, n_slc, SLC_BLK, G, v_s.shape[-1]).transpose(0, 3, 1, 2, 4)
    bi = jnp.arange(B)[:, None, None, None]
    gi = jnp.arange(G)[None, :, None, None]
    offs = jnp.arange(SLC_BLK)

    def blk(i):
        s0 = i * SEL_Q_BLK
        qb = lax.dynamic_slice_in_dim(qg, s0, SEL_Q_BLK, axis=1)
        ib = lax.dynamic_slice_in_dim(sel, s0, SEL_Q_BLK, axis=2)
        kk = kb[bi, gi, ib]
        vv = vb[bi, gi, ib]
        s = jnp.einsum('bqghd,bgqnld->bghqnl', qb, kk, preferred_element_type=jnp.float32) * scale
        kpos = ib[..., None] * SLC_BLK + offs
        qpos = s0 + jnp.arange(SEL_Q_BLK)
        m = (kpos <= qpos[None, None, :, None, None])[:, :, None]
        s = jnp.where(m, s, NEG_INF).reshape(B, G, Hg, SEL_Q_BLK, n_sel * SLC_BLK)
        pr = jax.nn.softmax(s, axis=-1).astype(vv.dtype).reshape(B, G, Hg, SEL_Q_BLK, n_sel, SLC_BLK)
        return jnp.einsum('bghqnl,bgqnld->bqghd', pr, vv)

    return _unblock(lax.map(blk, jnp.arange(S // SEL_Q_BLK)))


def _nsa_window(qg, k_w, v_w, scale):
    B, S, G, Hg, _ = qg.shape
    pad = ((0, 0), (WIN, 0), (0, 0), (0, 0))
    kp = jnp.pad(k_w, pad)
    vp = jnp.pad(v_w, pad)
    span = Q_BLK + WIN

    def blk(i):
        s0 = i * Q_BLK
        qb = lax.dynamic_slice_in_dim(qg, s0, Q_BLK, axis=1)
        kk = lax.dynamic_slice_in_dim(kp, s0, span, axis=1)
        vv = lax.dynamic_slice_in_dim(vp, s0, span, axis=1)
        s = jnp.einsum('bqghd,bkgd->bghqk', qb, kk, preferred_element_type=jnp.float32) * scale
        qpos = (s0 + jnp.arange(Q_BLK))[:, None]
        kpos = (s0 - WIN + jnp.arange(span))[None, :]
        m = (kpos <= qpos) & (kpos > qpos - WIN) & (kpos >= 0)
        pr = jax.nn.softmax(jnp.where(m, s, NEG_INF), axis=-1).astype(vv.dtype)
        return jnp.einsum('bghqk,bkgd->bqghd', pr, vv)

    return _unblock(lax.map(blk, jnp.arange(S // Q_BLK)))


def _nsa(q, k_c, v_c, k_s, v_s, k_w, v_w, gates, pos, pe_k, pe_v, wk1, wk2, wv1, wv2):
    B, S = q.shape[:2]
    dt = q.dtype
    scale = NSA_DK ** -0.5
    qg = q.reshape(B, S, NSA_GROUPS, NSA_HPG, NSA_DK)

    n_cmp = (S - CMP_BLK) // CMP_STRIDE + 1
    pos_c = jnp.arange(n_cmp) * CMP_STRIDE + (CMP_BLK - 1)
    kc = _rope(_nsa_compress(k_c, pe_k, wk1, wk2), pos_c)
    vc = _nsa_compress(v_c, pe_v, wv1, wv2)
    sc = jnp.einsum('bsghd,bngd->bghsn', qg, kc, preferred_element_type=jnp.float32) * scale
    mask_c = pos_c[None, :] <= pos[:, None]
    p_cmp = jax.nn.softmax(jnp.where(mask_c, sc, NEG_INF), axis=-1) * mask_c
    o_cmp = jnp.einsum('bghsn,bngd->bsghd', p_cmp.astype(dt), vc)

    n_slc = S // SLC_BLK
    ratio = SLC_BLK // CMP_STRIDE
    span = CMP_BLK // CMP_STRIDE
    n_terms = ratio + span - 1
    need = ratio * (n_slc - 1) + n_terms
    imp = jnp.sum(p_cmp, axis=2)
    imp = jnp.pad(imp, ((0, 0), (0, 0), (0, 0), (span - 1, max(0, need - (span - 1) - n_cmp))))
    p_slc = sum(imp[..., r: r + ratio * (n_slc - 1) + 1: ratio] for r in range(n_terms))

    blk_id = jnp.arange(n_slc)[None, :]
    cur = (pos // SLC_BLK)[:, None]
    forced = (blk_id == 0) | (blk_id == cur) | (blk_id == cur - 1)
    valid = blk_id * SLC_BLK <= pos[:, None]
    score = jnp.where(forced, FORCE_SCORE, jnp.where(valid, p_slc, -FORCE_SCORE))
    _, sel = lax.top_k(score, min(N_SEL, n_slc))
    o_slc = _nsa_selected(qg, k_s, v_s, sel, scale)

    o_win = _nsa_window(qg, k_w, v_w, scale)

    g = jax.nn.sigmoid(gates).reshape(B, S, NSA_GROUPS, NSA_HPG, 3)
    o = g[..., 0:1] * o_cmp + g[..., 1:2] * o_slc + g[..., 2:3] * o_win
    return o.reshape(B, S, NSA_HEADS * NSA_DV)


def _mla_attention(q_nope, q_rope, k_nope, k_rope, v):
    B, S, H, _ = q_nope.shape
    scale = (MLA_NOPE + MLA_ROPE) ** -0.5
    kpos = jnp.arange(S)[None, :]

    def blk(i):
        s0 = i * Q_BLK
        qn = lax.dynamic_slice_in_dim(q_nope, s0, Q_BLK, axis=1)
        qr = lax.dynamic_slice_in_dim(q_rope, s0, Q_BLK, axis=1)
        s = (jnp.einsum('bqhd,bkhd->bhqk', qn, k_nope, preferred_element_type=jnp.float32)
             + jnp.einsum('bqhd,bkd->bhqk', qr, k_rope, preferred_element_type=jnp.float32)) * scale
        qpos = (s0 + jnp.arange(Q_BLK))[:, None]
        pr = jax.nn.softmax(jnp.where(kpos <= qpos, s, NEG_INF), axis=-1).astype(v.dtype)
        return jnp.einsum('bhqk,bkhd->bqhd', pr, v)

    o = _unblock(lax.map(blk, jnp.arange(S // Q_BLK)))
    return o.reshape(B, S, H * MLA_DV)


def _moe(x2, router_w, router_b, w_gate, w_up, w_down, sh_gate, sh_up, sh_down):
    T, D = x2.shape
    f32 = jnp.float32
    scores = jax.nn.sigmoid(jnp.matmul(x2, router_w, preferred_element_type=f32))
    biased = (scores + router_b.astype(f32)).reshape(T, N_EXPERT_GROUPS, N_EXPERTS // N_EXPERT_GROUPS)
    grp_score = jnp.sum(lax.top_k(biased, 2)[0], axis=-1)
    gsel = lax.top_k(grp_score, TOPK_GROUPS)[1]
    gmask = jnp.any(gsel[..., None] == jnp.arange(N_EXPERT_GROUPS), axis=1)
    biased = jnp.where(gmask[..., None], biased, NEG_INF).reshape(T, N_EXPERTS)
    eidx = lax.top_k(biased, TOP_K)[1]
    wts = jnp.take_along_axis(scores, eidx, axis=1)
    wts = wts / jnp.sum(wts, axis=-1, keepdims=True) * ROUTED_SCALE

    n = T * TOP_K
    flat_e = eidx.reshape(n)
    flat_t = jnp.arange(n, dtype=jnp.int32) // TOP_K
    flat_w = wts.reshape(n)
    order = jnp.argsort(flat_e)
    se, st, sw = flat_e[order], flat_t[order], flat_w[order]
    counts = jnp.zeros((N_EXPERTS,), jnp.int32).at[flat_e].add(1)
    padded = (counts + MOE_ROW_BLK - 1) // MOE_ROW_BLK * MOE_ROW_BLK
    pad_end = jnp.cumsum(padded)
    pad_start = pad_end - padded
    raw_start = jnp.cumsum(counts) - counts
    dest = pad_start[se] + jnp.arange(n, dtype=jnp.int32) - raw_start[se]
    n_blk = -(-n // MOE_ROW_BLK) + N_EXPERTS
    tok = jnp.full((n_blk * MOE_ROW_BLK,), T, jnp.int32).at[dest].set(st)
    gw = jnp.zeros((n_blk * MOE_ROW_BLK,), f32).at[dest].set(sw)
    blk_e = jnp.minimum(jnp.searchsorted(pad_end, jnp.arange(n_blk) * MOE_ROW_BLK, side='right'), N_EXPERTS - 1)

    def step(acc, xs):
        t, w, e = xs
        xb = x2[jnp.minimum(t, T - 1)]
        h = jax.nn.silu(xb @ w_gate[e]) * (xb @ w_up[e])
        y = (h @ w_down[e]).astype(f32) * w[:, None]
        return acc.at[t].add(y, mode='drop'), None

    routed, _ = lax.scan(step, jnp.zeros((T, D), f32),
                         (tok.reshape(n_blk, MOE_ROW_BLK), gw.reshape(n_blk, MOE_ROW_BLK), blk_e))
    shared = (jax.nn.silu(x2 @ sh_gate) * (x2 @ sh_up)) @ sh_down
    return (routed + shared.astype(f32)).astype(x2.dtype)


def kernel(x, p, w_in, nsa_pe_k, nsa_pe_v, nsa_cmp_k_w1, nsa_cmp_k_w2, nsa_cmp_v_w1, nsa_cmp_v_w2, mla_q_norm, mla_w_uq, mla_kv_norm, mla_w_uk, mla_w_uv, w_branch_nsa, w_branch_mla, w_out, ln1_g, ln1_b, router_w, router_b, exp_w_gate, exp_w_up, exp_w_down, sh_w_gate, sh_w_up, sh_w_down, ln2_g, ln2_b, ple_w_proj, ple_w_gate):
    B, S, D = x.shape
    T = B * S
    pos = jnp.arange(S)
    c0 = W_NSA_Q
    c1 = c0 + W_NSA_KV
    c2 = c1 + W_NSA_GATE
    c3 = c2 + W_MLA_CQ
    c4 = c3 + W_MLA_CKV
    c5 = c4 + W_MLA_KR
    gk = NSA_GROUPS * NSA_DK
    for i in range(DEPTH):
        proj = _mm(x.reshape(T, D), w_in[i]).reshape(B, S, D_IN)
        q_a, kv_a, g_a, cq, ckv, kr, g_m = jnp.split(proj, [c0, c1, c2, c3, c4, c5], axis=-1)

        q_a = _rope(q_a.reshape(B, S, NSA_HEADS, NSA_DK), pos)
        kv_a = kv_a.reshape(B, S, 3, NSA_GROUPS * (NSA_DK + NSA_DV))
        k_c = kv_a[:, :, 0, :gk].reshape(B, S, NSA_GROUPS, NSA_DK)
        v_c = kv_a[:, :, 0, gk:].reshape(B, S, NSA_GROUPS, NSA_DV)
        k_s = _rope(kv_a[:, :, 1, :gk].reshape(B, S, NSA_GROUPS, NSA_DK), pos)
        v_s = kv_a[:, :, 1, gk:].reshape(B, S, NSA_GROUPS, NSA_DV)
        k_w = _rope(kv_a[:, :, 2, :gk].reshape(B, S, NSA_GROUPS, NSA_DK), pos)
        v_w = kv_a[:, :, 2, gk:].reshape(B, S, NSA_GROUPS, NSA_DV)
        y_nsa = _nsa(q_a, k_c, v_c, k_s, v_s, k_w, v_w, g_a, pos,
                     nsa_pe_k[i], nsa_pe_v[i], nsa_cmp_k_w1[i], nsa_cmp_k_w2[i],
                     nsa_cmp_v_w1[i], nsa_cmp_v_w2[i])

        qm = (_rmsnorm(cq, mla_q_norm[i]) @ mla_w_uq[i]).reshape(B, S, MLA_HEADS, MLA_NOPE + MLA_ROPE)
        q_nope = qm[..., :MLA_NOPE]
        q_rope = _rope(qm[..., MLA_NOPE:], pos)
        ckv = _rmsnorm(ckv, mla_kv_norm[i])
        k_nope = (ckv @ mla_w_uk[i]).reshape(B, S, MLA_HEADS, MLA_NOPE)
        v_m = (ckv @ mla_w_uv[i]).reshape(B, S, MLA_HEADS, MLA_DV)
        k_rope = _rope(kr, pos)
        y_mla = _mla_attention(q_nope, q_rope, k_nope, k_rope, v_m)

        g_nsa = jax.nn.sigmoid(g_m[..., :D])
        g_mla = jax.nn.sigmoid(g_m[..., D:])
        merged = g_nsa * (y_nsa @ w_branch_nsa[i]) + g_mla * (y_mla @ w_branch_mla[i])
        x = _layernorm(DN_ALPHA * x + merged @ w_out[i], ln1_g[i], ln1_b[i])

        h = _moe(x.reshape(B * S, D), router_w[i], router_b[i], exp_w_gate[i], exp_w_up[i],
                 exp_w_down[i], sh_w_gate[i], sh_w_up[i], sh_w_down[i]).reshape(B, S, D)
        x = _layernorm(DN_ALPHA * x + h, ln2_g[i], ln2_b[i])

        x2 = x.reshape(T, D)
        gate = _mm(x2, ple_w_gate[i])
        pp = _mm(p[i].reshape(T, PLE_DIM), ple_w_proj[i])
        x = (x2 + jax.nn.sigmoid(gate) * pp).reshape(B, S, D)
    return x
```

```python
import functools

import jax
import jax.numpy as jnp
from jax import lax
from jax.experimental import pallas as pl
from jax.experimental.pallas import tpu as pltpu

D_MODEL = 2048
BATCH = 8
SEQ = 4096
DEPTH = 1

PLE_DIM = 256
ROPE_THETA = 10000.0
LN_EPS = 1e-5
RMS_EPS = 1e-6
NEG_INF = -1e30
Q_BLK = 128

NSA_HEADS = 8
NSA_GROUPS = 2
NSA_HPG = NSA_HEADS // NSA_GROUPS
NSA_DK = 128
NSA_DV = 128
CMP_BLK = 32
CMP_STRIDE = 16
CMP_HID = 256
SLC_BLK = 64
N_SEL = 16
WIN = 512
SEL_Q_BLK = 32
FORCE_SCORE = 1e4

MLA_HEADS = 8
MLA_Q_RANK = 768
MLA_KV_RANK = 512
MLA_NOPE = 128
MLA_ROPE = 64
MLA_DV = 128

N_EXPERTS = 64
TOP_K = 6
N_EXPERT_GROUPS = 8
TOPK_GROUPS = 4
D_EXPERT = 1408
ROUTED_SCALE = 2.5
MOE_ROW_BLK = 512

W_NSA_Q = NSA_HEADS * NSA_DK
W_NSA_KV = 3 * NSA_GROUPS * (NSA_DK + NSA_DV)
W_NSA_GATE = 3 * NSA_HEADS
W_MLA_CQ = MLA_Q_RANK
W_MLA_CKV = MLA_KV_RANK
W_MLA_KR = MLA_ROPE
W_MERGE = 2 * D_MODEL
D_IN = W_NSA_Q + W_NSA_KV + W_NSA_GATE + W_MLA_CQ + W_MLA_CKV + W_MLA_KR + W_MERGE

DN_ALPHA = (2.0 * DEPTH) ** 0.25
DN_BETA = (8.0 * DEPTH) ** -0.25

V7X_VMEM_LIMIT_BYTES = 48 * 1024 * 1024


def _mm_kernel(a_ref, b_ref, o_ref):
    a = a_ref[...].astype(jnp.bfloat16)
    o_ref[...] = jnp.dot(a, b_ref[...], preferred_element_type=jnp.float32).astype(o_ref.dtype)


def _mm(a, b, tm=512, tn=512):
    m, k = a.shape
    _, n = b.shape
    n_pad = -(-n // tn) * tn
    b = b.astype(jnp.bfloat16)
    if n_pad != n:
        b = jnp.pad(b, ((0, 0), (0, n_pad - n)))
    out = pl.pallas_call(
        _mm_kernel,
        grid=(m // tm, n_pad // tn),
        in_specs=[pl.BlockSpec((tm, k), lambda i, j: (i, 0)),
                  pl.BlockSpec((k, tn), lambda i, j: (0, j))],
        out_specs=pl.BlockSpec((tm, tn), lambda i, j: (i, j)),
        out_shape=jax.ShapeDtypeStruct((m, n_pad), jnp.float32),
        compiler_params=pltpu.CompilerParams(
            dimension_semantics=("parallel", "arbitrary"),
            vmem_limit_bytes=V7X_VMEM_LIMIT_BYTES),
        name="mm",
    )(a, b)
    return out[:, :n] if n_pad != n else out


def _layernorm(z, g, b):
    zf = z.astype(jnp.float32)
    mu = jnp.mean(zf, -1, keepdims=True)
    var = jnp.mean(jnp.square(zf - mu), -1, keepdims=True)
    return ((zf - mu) * lax.rsqrt(var + LN_EPS) * g + b).astype(z.dtype)


def _rmsnorm(z, g):
    zf = z.astype(jnp.float32)
    return (zf * lax.rsqrt(jnp.mean(jnp.square(zf), -1, keepdims=True) + RMS_EPS) * g).astype(z.dtype)


def _rope(z, pos):
    d = z.shape[-1]
    half = d // 2
    inv = jnp.power(ROPE_THETA, -jnp.arange(half, dtype=jnp.float32) * 2.0 / d)
    ang = pos.astype(jnp.float32)[:, None] * inv[None, :]
    shp = (pos.shape[0],) + (1,) * (z.ndim - 3) + (half,)
    cos = jnp.cos(ang).reshape(shp)
    sin = jnp.sin(ang).reshape(shp)
    zf = z.astype(jnp.float32)
    z1, z2 = zf[..., :half], zf[..., half:]
    return jnp.concatenate([z1 * cos - z2 * sin, z2 * cos + z1 * sin], -1).astype(z.dtype)


def _unblock(o):
    nb, b, q = o.shape[:3]
    return jnp.moveaxis(o, 0, 1).reshape((b, nb * q) + o.shape[3:])


def _nsa_compress(z, pe, w1, w2):
    b, s, g, d = z.shape
    n = (s - CMP_BLK) // CMP_STRIDE + 1
    idx = jnp.arange(n)[:, None] * CMP_STRIDE + jnp.arange(CMP_BLK)[None, :]
    blk = z[:, idx] + pe[None, None, :, None, :].astype(z.dtype)
    blk = jnp.swapaxes(blk, 2, 3).reshape(b, n, g, CMP_BLK * d)
    return jax.nn.gelu(blk @ w1) @ w2


MASK_BIG = 16384.0
SLC_TQ = 256
SLC_TK = 512


def _online_softmax_step(s, v, m_s, l_s, acc_s):
    m_prev = m_s[...]
    m_new = jnp.maximum(m_prev, jnp.max(s, axis=-1, keepdims=True))
    alpha = jnp.exp(m_prev - m_new)
    p = jnp.exp(s - m_new)
    l_s[...] = alpha * l_s[...] + jnp.sum(p, axis=-1, keepdims=True)
    acc_s[...] = alpha * acc_s[...] + jnp.dot(p.astype(jnp.bfloat16), v, preferred_element_type=jnp.float32)
    m_s[...] = m_new


def _slc_kernel(q_ref, msk_ref, k_ref, v_ref, o_ref, m_s, l_s, acc_s, *, tq, tk, hpg, dk, n_blk, blk):
    q0 = pl.program_id(2) * tq
    rows = hpg * tq
    bias = ((msk_ref[0, 0] - 1.0) * MASK_BIG).astype(jnp.bfloat16)
    qa = jnp.concatenate(
        [jnp.concatenate([q_ref[0, :, h * dk:(h + 1) * dk], bias], axis=1) for h in range(hpg)], axis=0)
    m_s[...] = jnp.full(m_s.shape, NEG_INF, jnp.float32)
    l_s[...] = jnp.zeros(l_s.shape, jnp.float32)
    acc_s[...] = jnp.zeros(acc_s.shape, jnp.float32)

    def tile(j, causal):
        k0 = pl.multiple_of(j * tk, tk)
        k = k_ref[0, pl.ds(k0, tk), :]
        kblk = (k0 + lax.broadcasted_iota(jnp.int32, (tk, n_blk), 0)) // blk
        onehot = (kblk == lax.broadcasted_iota(jnp.int32, (tk, n_blk), 1)).astype(jnp.bfloat16)
        ka = jnp.concatenate([k, onehot], axis=1)
        s = lax.dot_general(qa, ka, (((1,), (1,)), ((), ())), preferred_element_type=jnp.float32)
        if causal:
            qpos = q0 + lax.rem(lax.broadcasted_iota(jnp.int32, (rows, tk), 0), tq)
            kpos = k0 + lax.broadcasted_iota(jnp.int32, (rows, tk), 1)
            s = jnp.where(kpos <= qpos, s, -MASK_BIG)
        _online_softmax_step(s, v_ref[0, pl.ds(k0, tk), :], m_s, l_s, acc_s)

    j_diag = q0 // tk

    def body(j, carry):
        tile(j, False)
        return carry

    lax.fori_loop(0, j_diag, body, 0)
    tile(j_diag, True)
    out = acc_s[...] / l_s[...]
    o_ref[0] = jnp.concatenate([out[h * tq:(h + 1) * tq] for h in range(hpg)], axis=1)


def _nsa_selected(q, k_s, v_s, mask, scale):
    B, S, hd = q.shape
    G = mask.shape[1]
    n_blk = mask.shape[-1]
    dk = k_s.shape[-1] // G
    hpg = hd // (G * dk)
    tq, tk = min(SLC_TQ, S), min(SLC_TK, S)
    rows = hpg * tq
    qb = (q * scale).astype(jnp.bfloat16)
    kb = k_s.astype(jnp.bfloat16)
    vb = v_s.astype(jnp.bfloat16)
    return pl.pallas_call(
        functools.partial(_slc_kernel, tq=tq, tk=tk, hpg=hpg, dk=dk, n_blk=n_blk, blk=S // n_blk),
        grid=(B, G, S // tq),
        in_specs=[pl.BlockSpec((1, tq, hpg * dk), lambda b, g, i: (b, i, g)),
                  pl.BlockSpec((1, 1, tq, n_blk), lambda b, g, i: (b, g, i, 0)),
                  pl.BlockSpec((1, S, dk), lambda b, g, i: (b, 0, g)),
                  pl.BlockSpec((1, S, dk), lambda b, g, i: (b, 0, g))],
        out_specs=pl.BlockSpec((1, tq, hpg * dk), lambda b, g, i: (b, i, g)),
        out_shape=jax.ShapeDtypeStruct((B, S, hd), jnp.float32),
        scratch_shapes=[pltpu.VMEM((rows, 1), jnp.float32), pltpu.VMEM((rows, 1), jnp.float32),
                        pltpu.VMEM((rows, dk), jnp.float32)],
        compiler_params=pltpu.CompilerParams(
            dimension_semantics=("parallel", "parallel", "arbitrary"),
            vmem_limit_bytes=V7X_VMEM_LIMIT_BYTES),
        name="nsa_selected",
    )(qb, mask, kb, vb)


def _nsa_window(qg, k_w, v_w, scale):
    B, S, G, Hg, _ = qg.shape
    pad = ((0, 0), (WIN, 0), (0, 0), (0, 0))
    kp = jnp.pad(k_w, pad)
    vp = jnp.pad(v_w, pad)
    span = Q_BLK + WIN

    def blk(i):
        s0 = i * Q_BLK
        qb = lax.dynamic_slice_in_dim(qg, s0, Q_BLK, axis=1)
        kk = lax.dynamic_slice_in_dim(kp, s0, span, axis=1)
        vv = lax.dynamic_slice_in_dim(vp, s0, span, axis=1)
        s = jnp.einsum('bqghd,bkgd->bghqk', qb, kk, preferred_element_type=jnp.float32) * scale
        qpos = (s0 + jnp.arange(Q_BLK))[:, None]
        kpos = (s0 - WIN + jnp.arange(span))[None, :]
        m = (kpos <= qpos) & (kpos > qpos - WIN) & (kpos >= 0)
        pr = jax.nn.softmax(jnp.where(m, s, NEG_INF), axis=-1).astype(vv.dtype)
        return jnp.einsum('bghqk,bkgd->bqghd', pr, vv)

    return _unblock(lax.map(blk, jnp.arange(S // Q_BLK)))


def _nsa(q, k_c, v_c, k_s, v_s, k_w, v_w, gates, pos, pe_k, pe_v, wk1, wk2, wv1, wv2):
    B, S = q.shape[:2]
    dt = q.dtype
    scale = NSA_DK ** -0.5
    qg = q.reshape(B, S, NSA_GROUPS, NSA_HPG, NSA_DK)

    n_cmp = (S - CMP_BLK) // CMP_STRIDE + 1
    pos_c = jnp.arange(n_cmp) * CMP_STRIDE + (CMP_BLK - 1)
    kc = _rope(_nsa_compress(k_c, pe_k, wk1, wk2), pos_c)
    vc = _nsa_compress(v_c, pe_v, wv1, wv2)
    sc = jnp.einsum('bsghd,bngd->bghsn', qg, kc, preferred_element_type=jnp.float32) * scale
    mask_c = pos_c[None, :] <= pos[:, None]
    p_cmp = jax.nn.softmax(jnp.where(mask_c, sc, NEG_INF), axis=-1) * mask_c
    o_cmp = jnp.einsum('bghsn,bngd->bsghd', p_cmp.astype(dt), vc)

    n_slc = S // SLC_BLK
    ratio = SLC_BLK // CMP_STRIDE
    span = CMP_BLK // CMP_STRIDE
    n_terms = ratio + span - 1
    need = ratio * (n_slc - 1) + n_terms
    imp = jnp.sum(p_cmp, axis=2)
    imp = jnp.pad(imp, ((0, 0), (0, 0), (0, 0), (span - 1, max(0, need - (span - 1) - n_cmp))))
    p_slc = sum(imp[..., r: r + ratio * (n_slc - 1) + 1: ratio] for r in range(n_terms))

    blk_id = jnp.arange(n_slc)[None, :]
    cur = (pos // SLC_BLK)[:, None]
    forced = (blk_id == 0) | (blk_id == cur) | (blk_id == cur - 1)
    valid = blk_id * SLC_BLK <= pos[:, None]
    score = jnp.where(forced, FORCE_SCORE, jnp.where(valid, p_slc, -FORCE_SCORE))
    _, sel = lax.top_k(score, min(N_SEL, n_slc))
    sel_mask = jnp.any(sel[..., None] == jnp.arange(n_slc), axis=-2).astype(jnp.float32)
    o_slc = _nsa_selected(q.reshape(B, S, -1), k_s.reshape(B, S, -1), v_s.reshape(B, S, -1), sel_mask,
                          scale).reshape(B, S, NSA_GROUPS, NSA_HPG, NSA_DV)

    o_win = _nsa_window(qg, k_w, v_w, scale)

    g = jax.nn.sigmoid(gates).reshape(B, S, NSA_GROUPS, NSA_HPG, 3)
    o = g[..., 0:1] * o_cmp + g[..., 1:2] * o_slc + g[..., 2:3] * o_win
    return o.reshape(B, S, NSA_HEADS * NSA_DV)


def _mla_attention(q_nope, q_rope, k_nope, k_rope, v):
    B, S, H, _ = q_nope.shape
    scale = (MLA_NOPE + MLA_ROPE) ** -0.5
    kpos = jnp.arange(S)[None, :]

    def blk(i):
        s0 = i * Q_BLK
        qn = lax.dynamic_slice_in_dim(q_nope, s0, Q_BLK, axis=1)
        qr = lax.dynamic_slice_in_dim(q_rope, s0, Q_BLK, axis=1)
        s = (jnp.einsum('bqhd,bkhd->bhqk', qn, k_nope, preferred_element_type=jnp.float32)
             + jnp.einsum('bqhd,bkd->bhqk', qr, k_rope, preferred_element_type=jnp.float32)) * scale
        qpos = (s0 + jnp.arange(Q_BLK))[:, None]
        pr = jax.nn.softmax(jnp.where(kpos <= qpos, s, NEG_INF), axis=-1).astype(v.dtype)
        return jnp.einsum('bhqk,bkhd->bqhd', pr, v)

    o = _unblock(lax.map(blk, jnp.arange(S // Q_BLK)))
    return o.reshape(B, S, H * MLA_DV)


MOE_TM = 256
MXU_WIDTH_V7X = 256


def _hidden_chunks(f):
    chunks, c0 = [], 0
    while c0 < f:
        cs = min(MXU_WIDTH_V7X, f - c0)
        chunks.append((c0, cs))
        c0 += cs
    return tuple(chunks)


def _moe_ffn_kernel(blk_e_ref, n_used_ref, x_ref, wg_ref, wu_ref, wd_ref, o_ref, *, chunks):
    del blk_e_ref

    @pl.when(pl.program_id(0) < n_used_ref[0])
    def _():
        x = x_ref[...]
        acc = None
        for c0, cs in chunks:
            g = jnp.dot(x, wg_ref[0, :, c0:c0 + cs], preferred_element_type=jnp.float32)
            u = jnp.dot(x, wu_ref[0, :, c0:c0 + cs], preferred_element_type=jnp.float32)
            h = (g * jax.nn.sigmoid(g) * u).astype(jnp.bfloat16)
            y = jnp.dot(h, wd_ref[0, c0:c0 + cs, :], preferred_element_type=jnp.float32)
            acc = y if acc is None else acc + y
        o_ref[...] = acc


def _moe_routed(x2, eidx, wts, w_gate, w_up, w_down):
    T, D = x2.shape
    K = eidx.shape[1]
    E, _, F = w_gate.shape
    tm = MOE_TM
    n = T * K
    n_blk = -(-n // tm) + E
    i32 = jnp.int32

    onehot = jnp.any(eidx[..., None] == jnp.arange(E, dtype=eidx.dtype), axis=1).astype(i32)
    cum = jnp.cumsum(onehot, axis=0)
    counts = cum[-1]
    rank = jnp.take_along_axis(cum, eidx, axis=1) - 1
    padded = (counts + tm - 1) // tm * tm
    pad_end = jnp.cumsum(padded)
    pad_start = pad_end - padded
    dest = (pad_start[eidx] + rank).astype(i32)
    flat_t = jnp.arange(n, dtype=i32) // K
    tok = jnp.zeros((n_blk * tm,), i32).at[dest.reshape(n)].set(flat_t)
    n_used = (pad_end[-1] // tm).astype(i32).reshape(1)
    blk_e = jnp.minimum(jnp.searchsorted(pad_end, jnp.arange(n_blk, dtype=i32) * tm, side='right'),
                        E - 1).astype(i32)

    xs = jnp.take(x2.astype(jnp.bfloat16), tok, axis=0)

    def row_map(i, be, nu):
        return (jnp.minimum(i, nu[0] - 1), 0)

    y = pl.pallas_call(
        functools.partial(_moe_ffn_kernel, chunks=_hidden_chunks(F)),
        grid_spec=pltpu.PrefetchScalarGridSpec(
            num_scalar_prefetch=2,
            grid=(n_blk,),
            in_specs=[pl.BlockSpec((tm, D), row_map),
                      pl.BlockSpec((1, D, F), lambda i, be, nu: (be[i], 0, 0)),
                      pl.BlockSpec((1, D, F), lambda i, be, nu: (be[i], 0, 0)),
                      pl.BlockSpec((1, F, D), lambda i, be, nu: (be[i], 0, 0))],
            out_specs=pl.BlockSpec((tm, D), row_map)),
        out_shape=jax.ShapeDtypeStruct((n_blk * tm, D), jnp.float32),
        compiler_params=pltpu.CompilerParams(
            dimension_semantics=("arbitrary",),
            vmem_limit_bytes=V7X_VMEM_LIMIT_BYTES),
        name="moe_ffn",
    )(blk_e, n_used, xs, w_gate.astype(jnp.bfloat16), w_up.astype(jnp.bfloat16), w_down.astype(jnp.bfloat16))

    yk = jnp.take(y, dest.reshape(n), axis=0).reshape(T, K, D)
    return jnp.sum(yk * wts[..., None], axis=1)


def _moe(x2, router_w, router_b, w_gate, w_up, w_down, sh_gate, sh_up, sh_down):
    T, D = x2.shape
    f32 = jnp.float32
    scores = jax.nn.sigmoid(jnp.matmul(x2, router_w, preferred_element_type=f32))
    biased = (scores + router_b.astype(f32)).reshape(T, N_EXPERT_GROUPS, N_EXPERTS // N_EXPERT_GROUPS)
    grp_score = jnp.sum(lax.top_k(biased, 2)[0], axis=-1)
    gsel = lax.top_k(grp_score, TOPK_GROUPS)[1]
    gmask = jnp.any(gsel[..., None] == jnp.arange(N_EXPERT_GROUPS), axis=1)
    biased = jnp.where(gmask[..., None], biased, NEG_INF).reshape(T, N_EXPERTS)
    eidx = lax.top_k(biased, TOP_K)[1]
    wts = jnp.take_along_axis(scores, eidx, axis=1)
    wts = wts / jnp.sum(wts, axis=-1, keepdims=True) * ROUTED_SCALE

    routed = _moe_routed(x2, eidx, wts, w_gate, w_up, w_down)
    shared = (jax.nn.silu(x2 @ sh_gate) * (x2 @ sh_up)) @ sh_down
    return (routed + shared.astype(f32)).astype(x2.dtype)


def kernel(x, p, w_in, nsa_pe_k, nsa_pe_v, nsa_cmp_k_w1, nsa_cmp_k_w2, nsa_cmp_v_w1, nsa_cmp_v_w2, mla_q_norm, mla_w_uq, mla_kv_norm, mla_w_uk, mla_w_uv, w_branch_nsa, w_branch_mla, w_out, ln1_g, ln1_b, router_w, router_b, exp_w_gate, exp_w_up, exp_w_down, sh_w_gate, sh_w_up, sh_w_down, ln2_g, ln2_b, ple_w_proj, ple_w_gate):
    B, S, D = x.shape
    T = B * S
    pos = jnp.arange(S)
    c0 = W_NSA_Q
    c1 = c0 + W_NSA_KV
    c2 = c1 + W_NSA_GATE
    c3 = c2 + W_MLA_CQ
    c4 = c3 + W_MLA_CKV
    c5 = c4 + W_MLA_KR
    gk = NSA_GROUPS * NSA_DK
    for i in range(DEPTH):
        proj = _mm(x.reshape(T, D), w_in[i]).reshape(B, S, D_IN)
        q_a, kv_a, g_a, cq, ckv, kr, g_m = jnp.split(proj, [c0, c1, c2, c3, c4, c5], axis=-1)

        q_a = _rope(q_a.reshape(B, S, NSA_HEADS, NSA_DK), pos)
        kv_a = kv_a.reshape(B, S, 3, NSA_GROUPS * (NSA_DK + NSA_DV))
        k_c = kv_a[:, :, 0, :gk].reshape(B, S, NSA_GROUPS, NSA_DK)
        v_c = kv_a[:, :, 0, gk:].reshape(B, S, NSA_GROUPS, NSA_DV)
        k_s = _rope(kv_a[:, :, 1, :gk].reshape(B, S, NSA_GROUPS, NSA_DK), pos)
        v_s = kv_a[:, :, 1, gk:].reshape(B, S, NSA_GROUPS, NSA_DV)
        k_w = _rope(kv_a[:, :, 2, :gk].reshape(B, S, NSA_GROUPS, NSA_DK), pos)
        v_w = kv_a[:, :, 2, gk:].reshape(B, S, NSA_GROUPS, NSA_DV)
        y_nsa = _nsa(q_a, k_c, v_c, k_s, v_s, k_w, v_w, g_a, pos,
                     nsa_pe_k[i], nsa_pe_v[i], nsa_cmp_k_w1[i], nsa_cmp_k_w2[i],
                     nsa_cmp_v_w1[i], nsa_cmp_v_w2[i])

        qm = (_rmsnorm(cq, mla_q_norm[i]) @ mla_w_uq[i]).reshape(B, S, MLA_HEADS, MLA_NOPE + MLA_ROPE)
        q_nope = qm[..., :MLA_NOPE]
        q_rope = _rope(qm[..., MLA_NOPE:], pos)
        ckv = _rmsnorm(ckv, mla_kv_norm[i])
        k_nope = (ckv @ mla_w_uk[i]).reshape(B, S, MLA_HEADS, MLA_NOPE)
        v_m = (ckv @ mla_w_uv[i]).reshape(B, S, MLA_HEADS, MLA_DV)
        k_rope = _rope(kr, pos)
        y_mla = _mla_attention(q_nope, q_rope, k_nope, k_rope, v_m)

        g_nsa = jax.nn.sigmoid(g_m[..., :D])
        g_mla = jax.nn.sigmoid(g_m[..., D:])
        merged = g_nsa * (y_nsa @ w_branch_nsa[i]) + g_mla * (y_mla @ w_branch_mla[i])
        x = _layernorm(DN_ALPHA * x + merged @ w_out[i], ln1_g[i], ln1_b[i])

        h = _moe(x.reshape(B * S, D), router_w[i], router_b[i], exp_w_gate[i], exp_w_up[i],
                 exp_w_down[i], sh_w_gate[i], sh_w_up[i], sh_w_down[i]).reshape(B, S, D)
        x = _layernorm(DN_ALPHA * x + h, ln2_g[i], ln2_b[i])

        x2 = x.reshape(T, D)
        gate = _mm(x2, ple_w_gate[i])
        pp = _mm(p[i].reshape(T, PLE_DIM), ple_w_proj[i])
        x = (x2 + jax.nn.sigmoid(gate) * pp).reshape(B, S, D)
    return x
```

```python
import functools
import math

import jax
import jax.numpy as jnp
from jax import lax
from jax.experimental import pallas as pl
from jax.experimental.pallas import tpu as pltpu

D_MODEL = 2048
BATCH = 8
SEQ = 4096
DEPTH = 1

PLE_DIM = 256
ROPE_THETA = 10000.0
LN_EPS = 1e-5
RMS_EPS = 1e-6
NEG_INF = -1e30

NSA_HEADS = 8
NSA_GROUPS = 2
NSA_HPG = NSA_HEADS // NSA_GROUPS
NSA_DK = 128
NSA_DV = 128
CMP_BLK = 32
CMP_STRIDE = 16
CMP_HID = 256
SLC_BLK = 64
N_SEL = 16
WIN = 512
FORCE_SCORE = 1e4

MLA_HEADS = 8
MLA_Q_RANK = 768
MLA_KV_RANK = 512
MLA_NOPE = 128
MLA_ROPE = 64
MLA_DV = 128

N_EXPERTS = 64
TOP_K = 6
N_EXPERT_GROUPS = 8
TOPK_GROUPS = 4
D_EXPERT = 1408
ROUTED_SCALE = 2.5

W_NSA_Q = NSA_HEADS * NSA_DK
W_NSA_KV = 3 * NSA_GROUPS * (NSA_DK + NSA_DV)
W_NSA_GATE = 3 * NSA_HEADS
W_MLA_CQ = MLA_Q_RANK
W_MLA_CKV = MLA_KV_RANK
W_MLA_KR = MLA_ROPE
W_MERGE = 2 * D_MODEL
D_IN = W_NSA_Q + W_NSA_KV + W_NSA_GATE + W_MLA_CQ + W_MLA_CKV + W_MLA_KR + W_MERGE

DN_ALPHA = (2.0 * DEPTH) ** 0.25
DN_BETA = (8.0 * DEPTH) ** -0.25

V7X_VMEM_LIMIT_BYTES = 48 * 1024 * 1024
LANES = 128
MXU_WIDTH_V7X = 256
PROJ_TM = 512

BF16 = jnp.bfloat16
F32 = jnp.float32


def _params(sem):
    return pltpu.CompilerParams(dimension_semantics=sem, vmem_limit_bytes=V7X_VMEM_LIMIT_BYTES)


def _resident(shape):
    return pl.BlockSpec(shape, lambda *_: (0,) * len(shape), pipeline_mode=pl.Buffered(1))


def _mm_kernel(a_ref, b_ref, o_ref):
    a = a_ref[...].astype(BF16)
    o_ref[...] = jnp.dot(a, b_ref[...], preferred_element_type=F32).astype(o_ref.dtype)


def _mm(a, b, tm=512, tn=512):
    m, k = a.shape
    _, n = b.shape
    n_pad = -(-n // tn) * tn
    b = b.astype(BF16)
    if n_pad != n:
        b = jnp.pad(b, ((0, 0), (0, n_pad - n)))
    out = pl.pallas_call(
        _mm_kernel,
        grid=(m // tm, n_pad // tn),
        in_specs=[pl.BlockSpec((tm, k), lambda i, j: (i, 0)),
                  pl.BlockSpec((k, tn), lambda i, j: (0, j))],
        out_specs=pl.BlockSpec((tm, tn), lambda i, j: (i, j)),
        out_shape=jax.ShapeDtypeStruct((m, n_pad), F32),
        compiler_params=_params(("parallel", "arbitrary")),
        name="mm",
    )(a, b)
    return out[:, :n] if n_pad != n else out


def _rope_tables_128(pos):
    half = LANES // 2
    inv = jnp.power(ROPE_THETA, -jnp.arange(half, dtype=F32) * 2.0 / LANES)
    ang = pos.astype(F32)[:, None] * inv[None, :]
    cos, sin = jnp.cos(ang), jnp.sin(ang)
    return jnp.concatenate([cos, cos], -1), jnp.concatenate([-sin, sin], -1)


def _rope_tables_64(pos):
    half = MLA_ROPE // 2
    inv = jnp.power(ROPE_THETA, -jnp.arange(half, dtype=F32) * 2.0 / MLA_ROPE)
    ang = pos.astype(F32)[:, None] * inv[None, :]
    cos, sin = jnp.cos(ang), jnp.sin(ang)
    zero = jnp.zeros_like(sin)
    return (jnp.concatenate([cos, cos, cos, cos], -1),
            jnp.concatenate([-sin, zero, -sin, zero], -1),
            jnp.concatenate([zero, sin, zero, sin], -1))


def _rope128(z, cos, sin):
    return z * cos + pltpu.roll(z, LANES // 2, 1) * sin


def _rope64(z, cos, sin_lo, sin_hi):
    return z * cos + pltpu.roll(z, LANES - MLA_ROPE // 2, 1) * sin_lo + pltpu.roll(z, MLA_ROPE // 2, 1) * sin_hi


def _nsa_proj_kernel(x_ref, w_ref, cos_ref, sin_ref, o_ref, *, rope_heads, n_q_heads, scale):
    x = x_ref[...]
    cos, sin = cos_ref[...], sin_ref[...]
    n_heads = w_ref.shape[1] // LANES
    for t in range(n_heads // 2):
        c0 = t * MXU_WIDTH_V7X
        z = jnp.dot(x, w_ref[:, c0:c0 + MXU_WIDTH_V7X], preferred_element_type=F32)
        for hh in range(2):
            head = 2 * t + hh
            zh = z[:, hh * LANES:(hh + 1) * LANES]
            if head in rope_heads:
                zh = _rope128(zh, cos, sin)
            if head < n_q_heads:
                zh = zh * scale
            o_ref[:, head * LANES:(head + 1) * LANES] = zh.astype(o_ref.dtype)


def _nsa_proj(xb, w, cos, sin, S):
    T, D = xb.shape
    n = w.shape[1]
    tm = min(PROJ_TM, S)
    n_pos = S // tm
    kv0 = NSA_HEADS
    per = NSA_GROUPS * 2
    rope_heads = tuple(range(NSA_HEADS)) + tuple(kv0 + per * br + g for br in (1, 2) for g in range(NSA_GROUPS))
    return pl.pallas_call(
        functools.partial(_nsa_proj_kernel, rope_heads=rope_heads, n_q_heads=NSA_HEADS, scale=NSA_DK ** -0.5),
        grid=(T // tm,),
        in_specs=[pl.BlockSpec((tm, D), lambda i: (i, 0)),
                  _resident((D, n)),
                  pl.BlockSpec((tm, LANES), lambda i: (i % n_pos, 0)),
                  pl.BlockSpec((tm, LANES), lambda i: (i % n_pos, 0))],
        out_specs=pl.BlockSpec((tm, n), lambda i: (i, 0)),
        out_shape=jax.ShapeDtypeStruct((T, n), BF16),
        compiler_params=_params(("parallel",)),
        name="nsa_proj",
    )(xb, w, cos, sin)


def _rmsnorm_rows(z, g):
    return z * lax.rsqrt(jnp.mean(jnp.square(z), -1, keepdims=True) + RMS_EPS) * g


def _mla_in_kernel(x_ref, w_ref, gq_ref, gkv_ref, wuq_ref, wukv_ref, cos_ref, slo_ref, shi_ref,
                   q_ref, kv_ref, kr_ref, gate_ref, *, scale):
    x = x_ref[...]
    cos, slo, shi = cos_ref[...], slo_ref[...], shi_ref[...]
    c1 = MLA_Q_RANK
    c2 = c1 + MLA_KV_RANK
    cq = jnp.dot(x, w_ref[:, :c1], preferred_element_type=F32)
    cqn = _rmsnorm_rows(cq, gq_ref[...]).astype(BF16)
    ckv = jnp.dot(x, w_ref[:, c1:c2], preferred_element_type=F32)
    ckvn = _rmsnorm_rows(ckv, gkv_ref[...]).astype(BF16)
    tail = jnp.dot(x, w_ref[:, c2:c2 + 2 * LANES], preferred_element_type=F32)
    kr_ref[...] = _rope64(tail[:, :LANES], cos, slo, shi).astype(kr_ref.dtype)
    gate_ref[...] = jax.nn.sigmoid(tail[:, LANES:])
    for h in range(MLA_HEADS):
        c0 = h * MXU_WIDTH_V7X
        z = jnp.dot(cqn, wuq_ref[:, c0:c0 + MXU_WIDTH_V7X], preferred_element_type=F32)
        q_ref[:, c0:c0 + LANES] = (z[:, :LANES] * scale).astype(q_ref.dtype)
        q_ref[:, c0 + LANES:c0 + 2 * LANES] = (_rope64(z[:, LANES:], cos, slo, shi) * scale).astype(q_ref.dtype)
    n_kv = wukv_ref.shape[1]
    for c0 in range(0, n_kv, MXU_WIDTH_V7X):
        kv_ref[:, c0:c0 + MXU_WIDTH_V7X] = jnp.dot(
            ckvn, wukv_ref[:, c0:c0 + MXU_WIDTH_V7X], preferred_element_type=F32).astype(kv_ref.dtype)


def _mla_in(xb, w, gq, gkv, wuq, wukv, cos, slo, shi, S):
    T, D = xb.shape
    tm = min(PROJ_TM, S)
    n_pos = S // tm
    tbl = pl.BlockSpec((tm, LANES), lambda i: (i % n_pos, 0))
    nq, nkv = wuq.shape[1], wukv.shape[1]
    return pl.pallas_call(
        functools.partial(_mla_in_kernel, scale=(MLA_NOPE + MLA_ROPE) ** -0.5),
        grid=(T // tm,),
        in_specs=[pl.BlockSpec((tm, D), lambda i: (i, 0)),
                  _resident(w.shape), _resident(gq.shape), _resident(gkv.shape),
                  _resident(wuq.shape), _resident(wukv.shape), tbl, tbl, tbl],
        out_specs=[pl.BlockSpec((tm, nq), lambda i: (i, 0)),
                   pl.BlockSpec((tm, nkv), lambda i: (i, 0)),
                   pl.BlockSpec((tm, LANES), lambda i: (i, 0)),
                   pl.BlockSpec((tm, LANES), lambda i: (i, 0))],
        out_shape=[jax.ShapeDtypeStruct((T, nq), BF16), jax.ShapeDtypeStruct((T, nkv), BF16),
                   jax.ShapeDtypeStruct((T, LANES), BF16), jax.ShapeDtypeStruct((T, LANES), F32)],
        compiler_params=_params(("parallel",)),
        name="mla_in",
    )(xb, w, gq, gkv, wuq, wukv, cos, slo, shi)


MASK_BIG = 16384.0


def _softmax_reset(m_s, l_s, acc_s):
    m_s[...] = jnp.full(m_s.shape, NEG_INF, F32)
    l_s[...] = jnp.zeros(l_s.shape, F32)
    acc_s[...] = jnp.zeros(acc_s.shape, F32)


def _online_softmax_step(s, v, m_s, l_s, acc_s):
    m_prev = m_s[...]
    m_new = jnp.maximum(m_prev, jnp.max(s, axis=-1, keepdims=True))
    alpha = jnp.exp(m_prev - m_new)
    p = jnp.exp(s - m_new)
    l_s[...] = alpha * l_s[...] + jnp.sum(p, axis=-1, keepdims=True)
    acc_s[...] = alpha * acc_s[...] + jnp.dot(p.astype(BF16), v, preferred_element_type=F32)
    m_s[...] = m_new


def _dot_nt(a, b):
    return lax.dot_general(a, b, (((1,), (1,)), ((), ())), preferred_element_type=F32)


MLA_TQ = 512
MLA_TK = 512


def _mla_attn_kernel(q_ref, kn_ref, kr_ref, v_ref, o_ref, m_s, l_s, acc_s, *, tq, tk):
    q0 = pl.program_id(2) * tq
    q = q_ref[0]
    _softmax_reset(m_s, l_s, acc_s)

    def tile(j, causal):
        k0 = pl.multiple_of(j * tk, tk)
        ka = jnp.concatenate([kn_ref[0, pl.ds(k0, tk), :], kr_ref[0, pl.ds(k0, tk), :]], axis=1)
        s = _dot_nt(q, ka)
        if causal:
            qpos = q0 + lax.broadcasted_iota(jnp.int32, (tq, tk), 0)
            kpos = k0 + lax.broadcasted_iota(jnp.int32, (tq, tk), 1)
            s = jnp.where(kpos <= qpos, s, -MASK_BIG)
        _online_softmax_step(s, v_ref[0, pl.ds(k0, tk), :], m_s, l_s, acc_s)

    j_diag = q0 // tk

    def body(j, carry):
        tile(j, False)
        return carry

    lax.fori_loop(0, j_diag, body, 0)
    tile(j_diag, True)
    o_ref[0] = (acc_s[...] / l_s[...]).astype(o_ref.dtype)


def _mla_attention(q, kv, kr):
    B, S, _ = q.shape
    H = MLA_HEADS
    tq = tk = min(MLA_TQ, S)
    return pl.pallas_call(
        functools.partial(_mla_attn_kernel, tq=tq, tk=tk),
        grid=(B, H, S // tq),
        in_specs=[pl.BlockSpec((1, tq, 2 * LANES), lambda b, h, i: (b, i, h)),
                  pl.BlockSpec((1, S, LANES), lambda b, h, i: (b, 0, h)),
                  pl.BlockSpec((1, S, LANES), lambda b, h, i: (b, 0, 0)),
                  pl.BlockSpec((1, S, LANES), lambda b, h, i: (b, 0, H + h))],
        out_specs=pl.BlockSpec((1, tq, LANES), lambda b, h, i: (b, i, h)),
        out_shape=jax.ShapeDtypeStruct((B, S, H * MLA_DV), BF16),
        scratch_shapes=[pltpu.VMEM((tq, 1), F32), pltpu.VMEM((tq, 1), F32), pltpu.VMEM((tq, MLA_DV), F32)],
        compiler_params=_params(("parallel", "parallel", "arbitrary")),
        name="mla_attn",
    )(q, kv, kr, kv)


def _gelu_tanh(x):
    return 0.5 * x * (1.0 + jnp.tanh(math.sqrt(2.0 / math.pi) * (x + 0.044715 * (x * x * x))))


def _compress_kernel(z_ref, pe_ref, w1_ref, w2_ref, cos_ref, sin_ref, o_ref, *, hid):
    which = pl.program_id(1)
    z = z_ref[0, 0, 0]
    n_chunks = z.shape[0]
    cd = z.shape[1]
    ab = jnp.dot(z, w1_ref[0], preferred_element_type=F32)
    a_part = ab[:, :hid]
    b_next = pltpu.roll(ab[:, hid:], n_chunks - 1, 0)
    pe_a = jnp.dot(pe_ref[0, :, :cd], w1_ref[0, :, :hid], preferred_element_type=F32)
    pe_b = jnp.dot(pe_ref[0, :, cd:], w1_ref[0, :, hid:], preferred_element_type=F32)
    h = _gelu_tanh(a_part + b_next + pe_a[0:1] + pe_b[0:1])
    out = jnp.dot(h.astype(BF16), w2_ref[0], preferred_element_type=F32)
    roped = _rope128(out, cos_ref[...], sin_ref[...])
    o_ref[0, 0, 0] = jnp.where(which == 0, roped, out).astype(o_ref.dtype)


def _nsa_compress(zc, pe, w1, w2, cos_c, sin_c):
    B, _, G, n_chunks, cd = zc.shape
    hid = w2.shape[1]
    d = w2.shape[2]
    return pl.pallas_call(
        functools.partial(_compress_kernel, hid=hid),
        grid=(B, 2, G),
        in_specs=[pl.BlockSpec((1, 1, 1, n_chunks, cd), lambda b, t, g: (b, t, g, 0, 0)),
                  pl.BlockSpec((1, 8, 2 * cd), lambda b, t, g: (t, 0, 0)),
                  pl.BlockSpec((1, cd, 2 * hid), lambda b, t, g: (t, 0, 0)),
                  pl.BlockSpec((1, hid, d), lambda b, t, g: (t, 0, 0)),
                  pl.BlockSpec((n_chunks, LANES), lambda b, t, g: (0, 0)),
                  pl.BlockSpec((n_chunks, LANES), lambda b, t, g: (0, 0))],
        out_specs=pl.BlockSpec((1, 1, 1, n_chunks, d), lambda b, t, g: (b, t, g, 0, 0)),
        out_shape=jax.ShapeDtypeStruct((B, 2, G, n_chunks, d), BF16),
        compiler_params=_params(("parallel", "parallel", "parallel")),
        name="nsa_compress",
    )(zc, pe, w1, w2, cos_c, sin_c)


NSA_TQ = 256
NSA_TK = 512


def _nsa_kernel(q_ref, g_ref, kc_ref, vc_ref, ks_ref, vs_ref, kw_ref, vw_ref, o_ref,
                m_s, l_s, acc_s, out_s, *, tq, tk, hpg, dk, n_cmp, n_slc):
    q0 = pl.program_id(2) * tq
    rows = hpg * tq
    i32 = jnp.int32
    q = jnp.concatenate([q_ref[0, :, h * dk:(h + 1) * dk] for h in range(hpg)], axis=0)
    gates = g_ref[0, 0]

    def gate_col(branch):
        return jnp.concatenate([gates[:, 3 * h + branch:3 * h + branch + 1] for h in range(hpg)], axis=0)

    n_pad = kc_ref.shape[3]
    sc = _dot_nt(q, kc_ref[0, 0, 0])
    qpos_r = q0 + lax.rem(lax.broadcasted_iota(i32, (rows, n_pad), 0), tq)
    n_io = lax.broadcasted_iota(i32, (rows, n_pad), 1)
    vis = (n_io * CMP_STRIDE + (CMP_BLK - 1) <= qpos_r) & (n_io < n_cmp)
    sc = jnp.where(vis, sc, NEG_INF)
    e = jnp.where(vis, jnp.exp(sc - jnp.max(sc, axis=-1, keepdims=True)), 0.0)
    p = e / jnp.maximum(jnp.sum(e, axis=-1, keepdims=True), 1e-30)
    o_cmp = jnp.dot(p.astype(BF16), vc_ref[0, 0, 0], preferred_element_type=F32)
    out_s[...] = gate_col(0) * o_cmp

    imp = p[0:tq]
    for h in range(1, hpg):
        imp = imp + p[h * tq:(h + 1) * tq]
    ratio = SLC_BLK // CMP_STRIDE
    span = CMP_BLK // CMP_STRIDE
    j_p = lax.broadcasted_iota(i32, (n_slc, n_pad), 0)
    c_p = lax.broadcasted_iota(i32, (n_slc, n_pad), 1)
    lo_c = ratio * j_p - (span - 1)
    pool = jnp.where((c_p >= lo_c) & (c_p < lo_c + ratio + span - 1), 1.0, 0.0).astype(BF16)
    hi = imp.astype(BF16)
    r1 = imp - hi.astype(F32)
    mid = r1.astype(BF16)
    lo = (r1 - mid.astype(F32)).astype(BF16)
    p_slc_t = _dot_nt(pool, hi) + _dot_nt(pool, mid) + _dot_nt(pool, lo)

    j_io = lax.broadcasted_iota(i32, (n_slc, tq), 0)
    qpos_l = q0 + lax.broadcasted_iota(i32, (n_slc, tq), 1)
    cur = qpos_l // SLC_BLK
    forced = (j_io == 0) | (j_io == cur) | (j_io == cur - 1)
    valid = j_io * SLC_BLK <= qpos_l
    score = jnp.where(forced, FORCE_SCORE, jnp.where(valid, p_slc_t, -FORCE_SCORE))
    cnt = jnp.zeros((n_slc, tq), i32)
    for i in range(n_slc):
        row = score[i:i + 1, :]
        beats = (row > score) | ((row == score) & (j_io > i))
        cnt = cnt + beats.astype(i32)
    bias_t = jnp.where(cnt < N_SEL, 0.0, -MASK_BIG)
    if n_slc < LANES:
        bias_t = jnp.concatenate([bias_t, jnp.full((LANES - n_slc, tq), -MASK_BIG, F32)], axis=0)
    bias = jnp.transpose(bias_t).astype(BF16)
    qa = jnp.concatenate([q, jnp.concatenate([bias] * hpg, axis=0)], axis=1)

    qpos_k = q0 + lax.rem(lax.broadcasted_iota(i32, (rows, tk), 0), tq)
    k_io = lax.broadcasted_iota(i32, (rows, tk), 1)

    _softmax_reset(m_s, l_s, acc_s)

    def slc_tile(j, causal):
        k0 = pl.multiple_of(j * tk, tk)
        kblk = (k0 + lax.broadcasted_iota(i32, (tk, LANES), 0)) // SLC_BLK
        onehot = jnp.where(kblk == lax.broadcasted_iota(i32, (tk, LANES), 1), 1.0, 0.0).astype(BF16)
        ka = jnp.concatenate([ks_ref[0, pl.ds(k0, tk), :], onehot], axis=1)
        s = _dot_nt(qa, ka)
        if causal:
            s = jnp.where(k0 + k_io <= qpos_k, s, -MASK_BIG)
        _online_softmax_step(s, vs_ref[0, pl.ds(k0, tk), :], m_s, l_s, acc_s)

    j_diag = q0 // tk

    def body(j, carry):
        slc_tile(j, False)
        return carry

    lax.fori_loop(0, j_diag, body, 0)
    slc_tile(j_diag, True)
    out_s[...] += gate_col(1) * (acc_s[...] / l_s[...])

    _softmax_reset(m_s, l_s, acc_s)

    def win_tile(j):
        k0 = pl.multiple_of(j * tk, tk)
        s = _dot_nt(q, kw_ref[0, pl.ds(k0, tk), :])
        kpos = k0 + k_io
        s = jnp.where((kpos <= qpos_k) & (kpos > qpos_k - WIN), s, -MASK_BIG)
        _online_softmax_step(s, vw_ref[0, pl.ds(k0, tk), :], m_s, l_s, acc_s)

    @pl.when(j_diag >= 1)
    def _():
        win_tile(j_diag - 1)

    win_tile(j_diag)
    out = out_s[...] + gate_col(2) * (acc_s[...] / l_s[...])
    o_ref[0] = jnp.concatenate([out[h * tq:(h + 1) * tq] for h in range(hpg)], axis=1).astype(o_ref.dtype)


def _nsa_attention(qkv, gates, cmp_kv):
    B, S, _ = qkv.shape
    G, hpg, dk = NSA_GROUPS, NSA_HPG, NSA_DK
    n_pad = cmp_kv.shape[3]
    n_cmp = (S - CMP_BLK) // CMP_STRIDE + 1
    tq, tk = min(NSA_TQ, S), min(NSA_TK, S)
    assert tk == WIN and tq <= tk
    rows = hpg * tq
    kv0 = NSA_HEADS

    def kv_spec(branch, is_v):
        base = kv0 + 2 * G * branch + G * is_v
        return pl.BlockSpec((1, S, dk), lambda b, g, i: (b, 0, base + g))

    return pl.pallas_call(
        functools.partial(_nsa_kernel, tq=tq, tk=tk, hpg=hpg, dk=dk, n_cmp=n_cmp, n_slc=S // SLC_BLK),
        grid=(B, G, S // tq),
        in_specs=[pl.BlockSpec((1, tq, hpg * dk), lambda b, g, i: (b, i, g)),
                  pl.BlockSpec((1, 1, tq, LANES), lambda b, g, i: (b, g, i, 0)),
                  pl.BlockSpec((1, 1, 1, n_pad, dk), lambda b, g, i: (b, 0, g, 0, 0)),
                  pl.BlockSpec((1, 1, 1, n_pad, dk), lambda b, g, i: (b, 1, g, 0, 0)),
                  kv_spec(1, 0), kv_spec(1, 1), kv_spec(2, 0), kv_spec(2, 1)],
        out_specs=pl.BlockSpec((1, tq, hpg * dk), lambda b, g, i: (b, i, g)),
        out_shape=jax.ShapeDtypeStruct((B, S, NSA_HEADS * NSA_DV), BF16),
        scratch_shapes=[pltpu.VMEM((rows, 1), F32), pltpu.VMEM((rows, 1), F32),
                        pltpu.VMEM((rows, dk), F32), pltpu.VMEM((rows, dk), F32)],
        compiler_params=_params(("parallel", "parallel", "arbitrary")),
        name="nsa_attn",
    )(qkv, gates, cmp_kv, cmp_kv, qkv, qkv, qkv, qkv)


MOE_TM = 256


def _hidden_chunks(f):
    chunks, c0 = [], 0
    while c0 < f:
        cs = min(MXU_WIDTH_V7X, f - c0)
        chunks.append((c0, cs))
        c0 += cs
    return tuple(chunks)


def _moe_ffn_kernel(blk_e_ref, n_used_ref, x_ref, wg_ref, wu_ref, wd_ref, o_ref, *, chunks):
    del blk_e_ref

    @pl.when(pl.program_id(0) < n_used_ref[0])
    def _():
        x = x_ref[...]
        acc = None
        for c0, cs in chunks:
            g = jnp.dot(x, wg_ref[0, :, c0:c0 + cs], preferred_element_type=F32)
            u = jnp.dot(x, wu_ref[0, :, c0:c0 + cs], preferred_element_type=F32)
            h = (g * jax.nn.sigmoid(g) * u).astype(BF16)
            y = jnp.dot(h, wd_ref[0, c0:c0 + cs, :], preferred_element_type=F32)
            acc = y if acc is None else acc + y
        o_ref[...] = acc


def _moe_routed(x2, eidx, wts, w_gate, w_up, w_down):
    T, D = x2.shape
    K = eidx.shape[1]
    E, _, F = w_gate.shape
    tm = MOE_TM
    n = T * K
    n_blk = -(-n // tm) + E
    i32 = jnp.int32

    onehot = jnp.any(eidx[..., None] == jnp.arange(E, dtype=eidx.dtype), axis=1).astype(i32)
    cum = jnp.cumsum(onehot, axis=0)
    counts = cum[-1]
    rank = jnp.take_along_axis(cum, eidx, axis=1) - 1
    padded = (counts + tm - 1) // tm * tm
    pad_end = jnp.cumsum(padded)
    pad_start = pad_end - padded
    dest = (pad_start[eidx] + rank).astype(i32)
    flat_t = jnp.arange(n, dtype=i32) // K
    tok = jnp.zeros((n_blk * tm,), i32).at[dest.reshape(n)].set(flat_t)
    n_used = (pad_end[-1] // tm).astype(i32).reshape(1)
    blk_e = jnp.minimum(jnp.searchsorted(pad_end, jnp.arange(n_blk, dtype=i32) * tm, side='right'),
                        E - 1).astype(i32)

    xs = jnp.take(x2.astype(BF16), tok, axis=0)

    def row_map(i, be, nu):
        return (jnp.minimum(i, nu[0] - 1), 0)

    y = pl.pallas_call(
        functools.partial(_moe_ffn_kernel, chunks=_hidden_chunks(F)),
        grid_spec=pltpu.PrefetchScalarGridSpec(
            num_scalar_prefetch=2,
            grid=(n_blk,),
            in_specs=[pl.BlockSpec((tm, D), row_map),
                      pl.BlockSpec((1, D, F), lambda i, be, nu: (be[i], 0, 0)),
                      pl.BlockSpec((1, D, F), lambda i, be, nu: (be[i], 0, 0)),
                      pl.BlockSpec((1, F, D), lambda i, be, nu: (be[i], 0, 0))],
            out_specs=pl.BlockSpec((tm, D), row_map)),
        out_shape=jax.ShapeDtypeStruct((n_blk * tm, D), F32),
        compiler_params=_params(("arbitrary",)),
        name="moe_ffn",
    )(blk_e, n_used, xs, w_gate.astype(BF16), w_up.astype(BF16), w_down.astype(BF16))

    yk = jnp.take(y, dest.reshape(n), axis=0).reshape(T, K, D)
    return jnp.sum(yk * wts[..., None], axis=1)


def _layernorm(z, g, b):
    zf = z.astype(F32)
    mu = jnp.mean(zf, -1, keepdims=True)
    var = jnp.mean(jnp.square(zf - mu), -1, keepdims=True)
    return ((zf - mu) * lax.rsqrt(var + LN_EPS) * g + b).astype(z.dtype)


def _moe(x2, router_w, router_b, w_gate, w_up, w_down, sh_gate, sh_up, sh_down):
    T, D = x2.shape
    scores = jax.nn.sigmoid(jnp.matmul(x2, router_w, preferred_element_type=F32))
    biased = (scores + router_b.astype(F32)).reshape(T, N_EXPERT_GROUPS, N_EXPERTS // N_EXPERT_GROUPS)
    grp_score = jnp.sum(lax.top_k(biased, 2)[0], axis=-1)
    gsel = lax.top_k(grp_score, TOPK_GROUPS)[1]
    gmask = jnp.any(gsel[..., None] == jnp.arange(N_EXPERT_GROUPS), axis=1)
    biased = jnp.where(gmask[..., None], biased, NEG_INF).reshape(T, N_EXPERTS)
    eidx = lax.top_k(biased, TOP_K)[1]
    wts = jnp.take_along_axis(scores, eidx, axis=1)
    wts = wts / jnp.sum(wts, axis=-1, keepdims=True) * ROUTED_SCALE

    routed = _moe_routed(x2, eidx, wts, w_gate, w_up, w_down)
    shared = (jax.nn.silu(x2 @ sh_gate) * (x2 @ sh_up)) @ sh_down
    return (routed + shared.astype(F32)).astype(x2.dtype)


def _attention_branches(xb, S, w_in, pe_k, pe_v, wk1, wk2, wv1, wv2, q_norm, w_uq, kv_norm, w_uk, w_uv):
    T, D = xb.shape
    B = T // S
    c0 = W_NSA_Q
    c1 = c0 + W_NSA_KV
    c2 = c1 + W_NSA_GATE
    c3 = c2 + W_MLA_CQ
    c4 = c3 + W_MLA_CKV
    c5 = c4 + W_MLA_KR
    G, dk = NSA_GROUPS, NSA_DK
    pos = jnp.arange(S)
    cos128, sin128 = _rope_tables_128(pos)
    cos64, slo64, shi64 = _rope_tables_64(pos)

    qkv = _nsa_proj(xb, w_in[:, :c1].astype(BF16), cos128, sin128, S)

    pad_kr = jnp.zeros((D, LANES - W_MLA_KR), w_in.dtype)
    pad_g = jnp.zeros((D, LANES - W_NSA_GATE), w_in.dtype)
    w_m = jnp.concatenate([w_in[:, c2:c4], w_in[:, c4:c5], pad_kr, w_in[:, c1:c2], pad_g], axis=1).astype(BF16)
    H = MLA_HEADS
    wq = w_uq.reshape(MLA_Q_RANK, H, MLA_NOPE + MLA_ROPE)
    wq = jnp.concatenate([wq, jnp.zeros((MLA_Q_RANK, H, 2 * LANES - MLA_NOPE - MLA_ROPE), wq.dtype)], axis=-1)
    wq = wq.reshape(MLA_Q_RANK, H * 2 * LANES).astype(BF16)
    wukv = jnp.concatenate([w_uk, w_uv], axis=1).astype(BF16)
    q_m, kv_m, kr_m, gates = _mla_in(xb, w_m, q_norm.reshape(1, -1), kv_norm.reshape(1, -1), wq, wukv,
                                     cos64, slo64, shi64, S)

    n_chunks = S // CMP_STRIDE
    kvc = qkv[:, NSA_HEADS * dk:(NSA_HEADS + 2 * G) * dk]
    zc = kvc.reshape(B, n_chunks, CMP_STRIDE, 2, G, dk).transpose(0, 3, 4, 1, 2, 5)
    zc = zc.reshape(B, 2, G, n_chunks, CMP_STRIDE * dk)
    pe = jnp.stack([pe_k.reshape(-1), pe_v.reshape(-1)])
    pe = jnp.broadcast_to(pe[:, None, :], (2, 8, pe.shape[-1])).astype(BF16)
    half = CMP_STRIDE * dk
    w1 = jnp.stack([jnp.concatenate([wk1[:half], wk1[half:]], axis=1),
                    jnp.concatenate([wv1[:half], wv1[half:]], axis=1)]).astype(BF16)
    w2 = jnp.stack([wk2, wv2]).astype(BF16)
    pos_c = jnp.arange(n_chunks) * CMP_STRIDE + (CMP_BLK - 1)
    cos_c, sin_c = _rope_tables_128(pos_c)
    cmp_kv = _nsa_compress(zc, pe, w1, w2, cos_c, sin_c)

    g3 = gates[:, :W_NSA_GATE].reshape(B, S, G, NSA_HPG * 3).transpose(0, 2, 1, 3)
    g3 = jnp.pad(g3, ((0, 0), (0, 0), (0, 0), (0, LANES - NSA_HPG * 3)))

    y_nsa = _nsa_attention(qkv.reshape(B, S, -1), g3, cmp_kv)
    y_mla = _mla_attention(q_m.reshape(B, S, -1), kv_m.reshape(B, S, -1), kr_m.reshape(B, S, -1))
    return y_nsa, y_mla


def kernel(x, p, w_in, nsa_pe_k, nsa_pe_v, nsa_cmp_k_w1, nsa_cmp_k_w2, nsa_cmp_v_w1, nsa_cmp_v_w2, mla_q_norm, mla_w_uq, mla_kv_norm, mla_w_uk, mla_w_uv, w_branch_nsa, w_branch_mla, w_out, ln1_g, ln1_b, router_w, router_b, exp_w_gate, exp_w_up, exp_w_down, sh_w_gate, sh_w_up, sh_w_down, ln2_g, ln2_b, ple_w_proj, ple_w_gate):
    B, S, D = x.shape
    T = B * S
    c5 = D_IN - W_MERGE
    for i in range(DEPTH):
        x2 = x.reshape(T, D)
        xb = x2.astype(BF16)
        y_nsa, y_mla = _attention_branches(
            xb, S, w_in[i], nsa_pe_k[i], nsa_pe_v[i], nsa_cmp_k_w1[i], nsa_cmp_k_w2[i],
            nsa_cmp_v_w1[i], nsa_cmp_v_w2[i], mla_q_norm[i], mla_w_uq[i], mla_kv_norm[i],
            mla_w_uk[i], mla_w_uv[i])

        g_m = _mm(x2, w_in[i][:, c5:])
        g_nsa = jax.nn.sigmoid(g_m[:, :D])
        g_mla = jax.nn.sigmoid(g_m[:, D:])
        merged = (g_nsa * _mm(y_nsa.reshape(T, -1), w_branch_nsa[i])
                  + g_mla * _mm(y_mla.reshape(T, -1), w_branch_mla[i]))
        x2 = _layernorm(DN_ALPHA * x2 + _mm(merged, w_out[i]), ln1_g[i], ln1_b[i])

        h = _moe(x2, router_w[i], router_b[i], exp_w_gate[i], exp_w_up[i],
                 exp_w_down[i], sh_w_gate[i], sh_w_up[i], sh_w_down[i])
        x2 = _layernorm(DN_ALPHA * x2 + h, ln2_g[i], ln2_b[i])

        gate = _mm(x2, ple_w_gate[i])
        pp = _mm(p[i].reshape(T, PLE_DIM), ple_w_proj[i])
        x = (x2 + jax.nn.sigmoid(gate) * pp).reshape(B, S, D)
    return x
```

```python
import functools
import math

import jax
import jax.numpy as jnp
from jax import lax
from jax.experimental import pallas as pl
from jax.experimental.pallas import tpu as pltpu

D_MODEL = 2048
BATCH = 8
SEQ = 4096
DEPTH = 1

PLE_DIM = 256
ROPE_THETA = 10000.0
LN_EPS = 1e-5
RMS_EPS = 1e-6
NEG_INF = -1e30

NSA_HEADS = 8
NSA_GROUPS = 2
NSA_HPG = NSA_HEADS // NSA_GROUPS
NSA_DK = 128
NSA_DV = 128
CMP_BLK = 32
CMP_STRIDE = 16
CMP_HID = 256
SLC_BLK = 64
N_SEL = 16
WIN = 512
FORCE_SCORE = 1e4

MLA_HEADS = 8
MLA_Q_RANK = 768
MLA_KV_RANK = 512
MLA_NOPE = 128
MLA_ROPE = 64
MLA_DV = 128

N_EXPERTS = 64
TOP_K = 6
N_EXPERT_GROUPS = 8
TOPK_GROUPS = 4
D_EXPERT = 1408
ROUTED_SCALE = 2.5

W_NSA_Q = NSA_HEADS * NSA_DK
W_NSA_KV = 3 * NSA_GROUPS * (NSA_DK + NSA_DV)
W_NSA_GATE = 3 * NSA_HEADS
W_MLA_CQ = MLA_Q_RANK
W_MLA_CKV = MLA_KV_RANK
W_MLA_KR = MLA_ROPE
W_MERGE = 2 * D_MODEL
D_IN = W_NSA_Q + W_NSA_KV + W_NSA_GATE + W_MLA_CQ + W_MLA_CKV + W_MLA_KR + W_MERGE

DN_ALPHA = (2.0 * DEPTH) ** 0.25
DN_BETA = (8.0 * DEPTH) ** -0.25

V7X_VMEM_LIMIT_BYTES = 48 * 1024 * 1024
LANES = 128
MXU_WIDTH_V7X = 256
PROJ_TM = 512

BF16 = jnp.bfloat16
F32 = jnp.float32


def _params(sem):
    return pltpu.CompilerParams(dimension_semantics=sem, vmem_limit_bytes=V7X_VMEM_LIMIT_BYTES)


def _resident(shape):
    return pl.BlockSpec(shape, lambda *_: (0,) * len(shape), pipeline_mode=pl.Buffered(1))


def _rope_tables_128(pos):
    half = LANES // 2
    inv = jnp.power(ROPE_THETA, -jnp.arange(half, dtype=F32) * 2.0 / LANES)
    ang = pos.astype(F32)[:, None] * inv[None, :]
    cos, sin = jnp.cos(ang), jnp.sin(ang)
    return jnp.concatenate([cos, cos], -1), jnp.concatenate([-sin, sin], -1)


def _rope_tables_64(pos):
    half = MLA_ROPE // 2
    inv = jnp.power(ROPE_THETA, -jnp.arange(half, dtype=F32) * 2.0 / MLA_ROPE)
    ang = pos.astype(F32)[:, None] * inv[None, :]
    cos, sin = jnp.cos(ang), jnp.sin(ang)
    zero = jnp.zeros_like(sin)
    return (jnp.concatenate([cos, cos, cos, cos], -1),
            jnp.concatenate([-sin, zero, -sin, zero], -1),
            jnp.concatenate([zero, sin, zero, sin], -1))


def _rope128(z, cos, sin):
    return z * cos + pltpu.roll(z, LANES // 2, 1) * sin


def _rope64(z, cos, sin_lo, sin_hi):
    return z * cos + pltpu.roll(z, LANES - MLA_ROPE // 2, 1) * sin_lo + pltpu.roll(z, MLA_ROPE // 2, 1) * sin_hi


def _nsa_proj_kernel(x_ref, w_ref, cos_ref, sin_ref, o_ref, *, rope_heads, n_q_heads, scale):
    x = x_ref[...]
    cos, sin = cos_ref[...], sin_ref[...]
    n_heads = w_ref.shape[1] // LANES
    for t in range(n_heads // 2):
        c0 = t * MXU_WIDTH_V7X
        z = jnp.dot(x, w_ref[:, c0:c0 + MXU_WIDTH_V7X], preferred_element_type=F32)
        for hh in range(2):
            head = 2 * t + hh
            zh = z[:, hh * LANES:(hh + 1) * LANES]
            if head in rope_heads:
                zh = _rope128(zh, cos, sin)
            if head < n_q_heads:
                zh = zh * scale
            o_ref[:, head * LANES:(head + 1) * LANES] = zh.astype(o_ref.dtype)


def _nsa_proj(xb, w, cos, sin, S):
    T, D = xb.shape
    n = w.shape[1]
    tm = min(PROJ_TM, S)
    n_pos = S // tm
    kv0 = NSA_HEADS
    per = NSA_GROUPS * 2
    rope_heads = tuple(range(NSA_HEADS)) + tuple(kv0 + per * br + g for br in (1, 2) for g in range(NSA_GROUPS))
    return pl.pallas_call(
        functools.partial(_nsa_proj_kernel, rope_heads=rope_heads, n_q_heads=NSA_HEADS, scale=NSA_DK ** -0.5),
        grid=(T // tm,),
        in_specs=[pl.BlockSpec((tm, D), lambda i: (i, 0)),
                  _resident((D, n)),
                  pl.BlockSpec((tm, LANES), lambda i: (i % n_pos, 0)),
                  pl.BlockSpec((tm, LANES), lambda i: (i % n_pos, 0))],
        out_specs=pl.BlockSpec((tm, n), lambda i: (i, 0)),
        out_shape=jax.ShapeDtypeStruct((T, n), BF16),
        compiler_params=_params(("parallel",)),
        name="nsa_proj",
    )(xb, w, cos, sin)


def _rmsnorm_rows(z, g):
    return z * lax.rsqrt(jnp.mean(jnp.square(z), -1, keepdims=True) + RMS_EPS) * g


def _mla_in_kernel(x_ref, w_ref, gq_ref, gkv_ref, wuq_ref, wukv_ref, cos_ref, slo_ref, shi_ref,
                   q_ref, kv_ref, kr_ref, gate_ref, *, scale):
    x = x_ref[...]
    cos, slo, shi = cos_ref[...], slo_ref[...], shi_ref[...]
    c1 = MLA_Q_RANK
    c2 = c1 + MLA_KV_RANK
    cq = jnp.dot(x, w_ref[:, :c1], preferred_element_type=F32)
    cqn = _rmsnorm_rows(cq, gq_ref[...]).astype(BF16)
    ckv = jnp.dot(x, w_ref[:, c1:c2], preferred_element_type=F32)
    ckvn = _rmsnorm_rows(ckv, gkv_ref[...]).astype(BF16)
    tail = jnp.dot(x, w_ref[:, c2:c2 + 2 * LANES], preferred_element_type=F32)
    kr_ref[...] = _rope64(tail[:, :LANES], cos, slo, shi).astype(kr_ref.dtype)
    gate_ref[...] = jax.nn.sigmoid(tail[:, LANES:])
    for h in range(MLA_HEADS):
        c0 = h * MXU_WIDTH_V7X
        z = jnp.dot(cqn, wuq_ref[:, c0:c0 + MXU_WIDTH_V7X], preferred_element_type=F32)
        q_ref[:, c0:c0 + LANES] = (z[:, :LANES] * scale).astype(q_ref.dtype)
        q_ref[:, c0 + LANES:c0 + 2 * LANES] = (_rope64(z[:, LANES:], cos, slo, shi) * scale).astype(q_ref.dtype)
    n_kv = wukv_ref.shape[1]
    for c0 in range(0, n_kv, MXU_WIDTH_V7X):
        kv_ref[:, c0:c0 + MXU_WIDTH_V7X] = jnp.dot(
            ckvn, wukv_ref[:, c0:c0 + MXU_WIDTH_V7X], preferred_element_type=F32).astype(kv_ref.dtype)


def _mla_in(xb, w, gq, gkv, wuq, wukv, cos, slo, shi, S):
    T, D = xb.shape
    tm = min(PROJ_TM, S)
    n_pos = S // tm
    tbl = pl.BlockSpec((tm, LANES), lambda i: (i % n_pos, 0))
    nq, nkv = wuq.shape[1], wukv.shape[1]
    return pl.pallas_call(
        functools.partial(_mla_in_kernel, scale=(MLA_NOPE + MLA_ROPE) ** -0.5),
        grid=(T // tm,),
        in_specs=[pl.BlockSpec((tm, D), lambda i: (i, 0)),
                  _resident(w.shape), _resident(gq.shape), _resident(gkv.shape),
                  _resident(wuq.shape), _resident(wukv.shape), tbl, tbl, tbl],
        out_specs=[pl.BlockSpec((tm, nq), lambda i: (i, 0)),
                   pl.BlockSpec((tm, nkv), lambda i: (i, 0)),
                   pl.BlockSpec((tm, LANES), lambda i: (i, 0)),
                   pl.BlockSpec((tm, LANES), lambda i: (i, 0))],
        out_shape=[jax.ShapeDtypeStruct((T, nq), BF16), jax.ShapeDtypeStruct((T, nkv), BF16),
                   jax.ShapeDtypeStruct((T, LANES), BF16), jax.ShapeDtypeStruct((T, LANES), F32)],
        compiler_params=_params(("parallel",)),
        name="mla_in",
    )(xb, w, gq, gkv, wuq, wukv, cos, slo, shi)


MASK_BIG = 16384.0


def _softmax_reset(m_s, l_s, acc_s):
    m_s[...] = jnp.full(m_s.shape, NEG_INF, F32)
    l_s[...] = jnp.zeros(l_s.shape, F32)
    acc_s[...] = jnp.zeros(acc_s.shape, F32)


def _online_softmax_step(s, v, m_s, l_s, acc_s):
    m_prev = m_s[...]
    m_new = jnp.maximum(m_prev, jnp.max(s, axis=-1, keepdims=True))
    alpha = jnp.exp(m_prev - m_new)
    p = jnp.exp(s - m_new)
    l_s[...] = alpha * l_s[...] + jnp.sum(p, axis=-1, keepdims=True)
    acc_s[...] = alpha * acc_s[...] + jnp.dot(p.astype(BF16), v, preferred_element_type=F32)
    m_s[...] = m_new


def _dot_nt(a, b):
    return lax.dot_general(a, b, (((1,), (1,)), ((), ())), preferred_element_type=F32)


MLA_TQ = 512
MLA_TK = 512


def _mla_attn_kernel(q_ref, kn_ref, kr_ref, v_ref, o_ref, m_s, l_s, acc_s, *, tq, tk):
    q0 = pl.program_id(2) * tq
    q = q_ref[0]
    _softmax_reset(m_s, l_s, acc_s)

    def tile(j, causal):
        k0 = pl.multiple_of(j * tk, tk)
        ka = jnp.concatenate([kn_ref[0, pl.ds(k0, tk), :], kr_ref[0, pl.ds(k0, tk), :]], axis=1)
        s = _dot_nt(q, ka)
        if causal:
            qpos = q0 + lax.broadcasted_iota(jnp.int32, (tq, tk), 0)
            kpos = k0 + lax.broadcasted_iota(jnp.int32, (tq, tk), 1)
            s = jnp.where(kpos <= qpos, s, -MASK_BIG)
        _online_softmax_step(s, v_ref[0, pl.ds(k0, tk), :], m_s, l_s, acc_s)

    j_diag = q0 // tk

    def body(j, carry):
        tile(j, False)
        return carry

    lax.fori_loop(0, j_diag, body, 0)
    tile(j_diag, True)
    o_ref[0] = (acc_s[...] / l_s[...]).astype(o_ref.dtype)


def _mla_attention(q, kv, kr):
    B, S, _ = q.shape
    H = MLA_HEADS
    tq = tk = min(MLA_TQ, S)
    return pl.pallas_call(
        functools.partial(_mla_attn_kernel, tq=tq, tk=tk),
        grid=(B, H, S // tq),
        in_specs=[pl.BlockSpec((1, tq, 2 * LANES), lambda b, h, i: (b, i, h)),
                  pl.BlockSpec((1, S, LANES), lambda b, h, i: (b, 0, h)),
                  pl.BlockSpec((1, S, LANES), lambda b, h, i: (b, 0, 0)),
                  pl.BlockSpec((1, S, LANES), lambda b, h, i: (b, 0, H + h))],
        out_specs=pl.BlockSpec((1, tq, LANES), lambda b, h, i: (b, i, h)),
        out_shape=jax.ShapeDtypeStruct((B, S, H * MLA_DV), BF16),
        scratch_shapes=[pltpu.VMEM((tq, 1), F32), pltpu.VMEM((tq, 1), F32), pltpu.VMEM((tq, MLA_DV), F32)],
        compiler_params=_params(("parallel", "parallel", "arbitrary")),
        name="mla_attn",
    )(q, kv, kr, kv)


def _gelu_tanh(x):
    return 0.5 * x * (1.0 + jnp.tanh(math.sqrt(2.0 / math.pi) * (x + 0.044715 * (x * x * x))))


def _compress_kernel(z_ref, pe_ref, w1_ref, w2_ref, cos_ref, sin_ref, o_ref, *, hid):
    which = pl.program_id(1)
    z = z_ref[0, 0, 0]
    n_chunks = z.shape[0]
    cd = z.shape[1]
    ab = jnp.dot(z, w1_ref[0], preferred_element_type=F32)
    a_part = ab[:, :hid]
    b_next = pltpu.roll(ab[:, hid:], n_chunks - 1, 0)
    pe_a = jnp.dot(pe_ref[0, :, :cd], w1_ref[0, :, :hid], preferred_element_type=F32)
    pe_b = jnp.dot(pe_ref[0, :, cd:], w1_ref[0, :, hid:], preferred_element_type=F32)
    h = _gelu_tanh(a_part + b_next + pe_a[0:1] + pe_b[0:1])
    out = jnp.dot(h.astype(BF16), w2_ref[0], preferred_element_type=F32)
    roped = _rope128(out, cos_ref[...], sin_ref[...])
    o_ref[0, 0, 0] = jnp.where(which == 0, roped, out).astype(o_ref.dtype)


def _nsa_compress(zc, pe, w1, w2, cos_c, sin_c):
    B, _, G, n_chunks, cd = zc.shape
    hid = w2.shape[1]
    d = w2.shape[2]
    return pl.pallas_call(
        functools.partial(_compress_kernel, hid=hid),
        grid=(B, 2, G),
        in_specs=[pl.BlockSpec((1, 1, 1, n_chunks, cd), lambda b, t, g: (b, t, g, 0, 0)),
                  pl.BlockSpec((1, 8, 2 * cd), lambda b, t, g: (t, 0, 0)),
                  pl.BlockSpec((1, cd, 2 * hid), lambda b, t, g: (t, 0, 0)),
                  pl.BlockSpec((1, hid, d), lambda b, t, g: (t, 0, 0)),
                  pl.BlockSpec((n_chunks, LANES), lambda b, t, g: (0, 0)),
                  pl.BlockSpec((n_chunks, LANES), lambda b, t, g: (0, 0))],
        out_specs=pl.BlockSpec((1, 1, 1, n_chunks, d), lambda b, t, g: (b, t, g, 0, 0)),
        out_shape=jax.ShapeDtypeStruct((B, 2, G, n_chunks, d), BF16),
        compiler_params=_params(("parallel", "parallel", "parallel")),
        name="nsa_compress",
    )(zc, pe, w1, w2, cos_c, sin_c)


NSA_TQ = 256
NSA_TK = 512


def _nsa_kernel(q_ref, g_ref, kc_ref, vc_ref, ks_ref, vs_ref, kw_ref, vw_ref, o_ref,
                m_s, l_s, acc_s, out_s, *, tq, tk, hpg, dk, n_cmp, n_slc):
    q0 = pl.program_id(2) * tq
    rows = hpg * tq
    i32 = jnp.int32
    q = jnp.concatenate([q_ref[0, :, h * dk:(h + 1) * dk] for h in range(hpg)], axis=0)
    gates = g_ref[0, 0]

    def gate_col(branch):
        return jnp.concatenate([gates[:, 3 * h + branch:3 * h + branch + 1] for h in range(hpg)], axis=0)

    n_pad = kc_ref.shape[3]
    sc = _dot_nt(q, kc_ref[0, 0, 0])
    qpos_r = q0 + lax.rem(lax.broadcasted_iota(i32, (rows, n_pad), 0), tq)
    n_io = lax.broadcasted_iota(i32, (rows, n_pad), 1)
    vis = (n_io * CMP_STRIDE + (CMP_BLK - 1) <= qpos_r) & (n_io < n_cmp)
    sc = jnp.where(vis, sc, NEG_INF)
    e = jnp.where(vis, jnp.exp(sc - jnp.max(sc, axis=-1, keepdims=True)), 0.0)
    p = e / jnp.maximum(jnp.sum(e, axis=-1, keepdims=True), 1e-30)
    o_cmp = jnp.dot(p.astype(BF16), vc_ref[0, 0, 0], preferred_element_type=F32)
    out_s[...] = gate_col(0) * o_cmp

    imp = p[0:tq]
    for h in range(1, hpg):
        imp = imp + p[h * tq:(h + 1) * tq]
    ratio = SLC_BLK // CMP_STRIDE
    span = CMP_BLK // CMP_STRIDE
    j_p = lax.broadcasted_iota(i32, (n_slc, n_pad), 0)
    c_p = lax.broadcasted_iota(i32, (n_slc, n_pad), 1)
    lo_c = ratio * j_p - (span - 1)
    pool = jnp.where((c_p >= lo_c) & (c_p < lo_c + ratio + span - 1), 1.0, 0.0).astype(BF16)
    hi = imp.astype(BF16)
    r1 = imp - hi.astype(F32)
    mid = r1.astype(BF16)
    lo = (r1 - mid.astype(F32)).astype(BF16)
    p_slc_t = _dot_nt(pool, hi) + _dot_nt(pool, mid) + _dot_nt(pool, lo)

    j_io = lax.broadcasted_iota(i32, (n_slc, tq), 0)
    qpos_l = q0 + lax.broadcasted_iota(i32, (n_slc, tq), 1)
    cur = qpos_l // SLC_BLK
    forced = (j_io == 0) | (j_io == cur) | (j_io == cur - 1)
    valid = j_io * SLC_BLK <= qpos_l
    score = jnp.where(forced, FORCE_SCORE, jnp.where(valid, p_slc_t, -FORCE_SCORE))
    cnt = jnp.zeros((n_slc, tq), i32)
    for i in range(n_slc):
        row = score[i:i + 1, :]
        beats = (row > score) | ((row == score) & (j_io > i))
        cnt = cnt + beats.astype(i32)
    bias_t = jnp.where(cnt < N_SEL, 0.0, -MASK_BIG)
    if n_slc < LANES:
        bias_t = jnp.concatenate([bias_t, jnp.full((LANES - n_slc, tq), -MASK_BIG, F32)], axis=0)
    bias = jnp.transpose(bias_t).astype(BF16)
    qa = jnp.concatenate([q, jnp.concatenate([bias] * hpg, axis=0)], axis=1)

    qpos_k = q0 + lax.rem(lax.broadcasted_iota(i32, (rows, tk), 0), tq)
    k_io = lax.broadcasted_iota(i32, (rows, tk), 1)

    _softmax_reset(m_s, l_s, acc_s)

    def slc_tile(j, causal):
        k0 = pl.multiple_of(j * tk, tk)
        kblk = (k0 + lax.broadcasted_iota(i32, (tk, LANES), 0)) // SLC_BLK
        onehot = jnp.where(kblk == lax.broadcasted_iota(i32, (tk, LANES), 1), 1.0, 0.0).astype(BF16)
        ka = jnp.concatenate([ks_ref[0, pl.ds(k0, tk), :], onehot], axis=1)
        s = _dot_nt(qa, ka)
        if causal:
            s = jnp.where(k0 + k_io <= qpos_k, s, -MASK_BIG)
        _online_softmax_step(s, vs_ref[0, pl.ds(k0, tk), :], m_s, l_s, acc_s)

    j_diag = q0 // tk

    def body(j, carry):
        slc_tile(j, False)
        return carry

    lax.fori_loop(0, j_diag, body, 0)
    slc_tile(j_diag, True)
    out_s[...] += gate_col(1) * (acc_s[...] / l_s[...])

    _softmax_reset(m_s, l_s, acc_s)

    def win_tile(j):
        k0 = pl.multiple_of(j * tk, tk)
        s = _dot_nt(q, kw_ref[0, pl.ds(k0, tk), :])
        kpos = k0 + k_io
        s = jnp.where((kpos <= qpos_k) & (kpos > qpos_k - WIN), s, -MASK_BIG)
        _online_softmax_step(s, vw_ref[0, pl.ds(k0, tk), :], m_s, l_s, acc_s)

    @pl.when(j_diag >= 1)
    def _():
        win_tile(j_diag - 1)

    win_tile(j_diag)
    out = out_s[...] + gate_col(2) * (acc_s[...] / l_s[...])
    o_ref[0] = jnp.concatenate([out[h * tq:(h + 1) * tq] for h in range(hpg)], axis=1).astype(o_ref.dtype)


def _nsa_attention(qkv, gates, cmp_kv):
    B, S, _ = qkv.shape
    G, hpg, dk = NSA_GROUPS, NSA_HPG, NSA_DK
    n_pad = cmp_kv.shape[3]
    n_cmp = (S - CMP_BLK) // CMP_STRIDE + 1
    tq, tk = min(NSA_TQ, S), min(NSA_TK, S)
    assert tk == WIN and tq <= tk
    rows = hpg * tq
    kv0 = NSA_HEADS

    def kv_spec(branch, is_v):
        base = kv0 + 2 * G * branch + G * is_v
        return pl.BlockSpec((1, S, dk), lambda b, g, i: (b, 0, base + g))

    return pl.pallas_call(
        functools.partial(_nsa_kernel, tq=tq, tk=tk, hpg=hpg, dk=dk, n_cmp=n_cmp, n_slc=S // SLC_BLK),
        grid=(B, G, S // tq),
        in_specs=[pl.BlockSpec((1, tq, hpg * dk), lambda b, g, i: (b, i, g)),
                  pl.BlockSpec((1, 1, tq, LANES), lambda b, g, i: (b, g, i, 0)),
                  pl.BlockSpec((1, 1, 1, n_pad, dk), lambda b, g, i: (b, 0, g, 0, 0)),
                  pl.BlockSpec((1, 1, 1, n_pad, dk), lambda b, g, i: (b, 1, g, 0, 0)),
                  kv_spec(1, 0), kv_spec(1, 1), kv_spec(2, 0), kv_spec(2, 1)],
        out_specs=pl.BlockSpec((1, tq, hpg * dk), lambda b, g, i: (b, i, g)),
        out_shape=jax.ShapeDtypeStruct((B, S, NSA_HEADS * NSA_DV), BF16),
        scratch_shapes=[pltpu.VMEM((rows, 1), F32), pltpu.VMEM((rows, 1), F32),
                        pltpu.VMEM((rows, dk), F32), pltpu.VMEM((rows, dk), F32)],
        compiler_params=_params(("parallel", "parallel", "arbitrary")),
        name="nsa_attn",
    )(qkv, gates, cmp_kv, cmp_kv, qkv, qkv, qkv, qkv)


MOE_TM = 256


def _hidden_chunks(f):
    chunks, c0 = [], 0
    while c0 < f:
        cs = min(MXU_WIDTH_V7X, f - c0)
        chunks.append((c0, cs))
        c0 += cs
    return tuple(chunks)


def _swiglu(x, wg_ref, wu_ref, wd_ref, chunks):
    acc = None
    for c0, cs in chunks:
        g = jnp.dot(x, wg_ref[0, :, c0:c0 + cs], preferred_element_type=F32)
        u = jnp.dot(x, wu_ref[0, :, c0:c0 + cs], preferred_element_type=F32)
        h = (g * jax.nn.sigmoid(g) * u).astype(BF16)
        y = jnp.dot(h, wd_ref[0, c0:c0 + cs, :], preferred_element_type=F32)
        acc = y if acc is None else acc + y
    return acc


def _moe_ffn_kernel(blk_e_ref, n_used_ref, x_ref, wg_ref, wu_ref, wd_ref, o_ref, *, chunks):
    del blk_e_ref

    @pl.when(pl.program_id(0) < n_used_ref[0])
    def _():
        o_ref[...] = _swiglu(x_ref[...], wg_ref, wu_ref, wd_ref, chunks).astype(o_ref.dtype)


def _cast_kernel(w_ref, o_ref):
    o_ref[...] = w_ref[...].astype(o_ref.dtype)


def _cast_bf16(w):
    E, R, C = w.shape
    return pl.pallas_call(
        _cast_kernel,
        grid=(E,),
        in_specs=[pl.BlockSpec((1, R, C), lambda e: (e, 0, 0))],
        out_specs=pl.BlockSpec((1, R, C), lambda e: (e, 0, 0)),
        out_shape=jax.ShapeDtypeStruct(w.shape, BF16),
        compiler_params=_params(("parallel",)),
        name="cast_bf16",
    )(w)


def _moe_routed(xb, eidx, wts, w_gate, w_up, w_down):
    T, D = xb.shape
    K = eidx.shape[1]
    E, _, F = w_gate.shape
    tm = MOE_TM
    n = T * K
    n_blk = -(-n // tm) + E
    i32 = jnp.int32

    onehot = jnp.any(eidx[..., None] == jnp.arange(E, dtype=eidx.dtype), axis=1).astype(i32)
    cum = jnp.cumsum(onehot, axis=0)
    counts = cum[-1]
    rank = jnp.take_along_axis(cum, eidx, axis=1) - 1
    padded = (counts + tm - 1) // tm * tm
    pad_end = jnp.cumsum(padded)
    pad_start = pad_end - padded
    dest = (pad_start[eidx] + rank).astype(i32)
    flat_t = jnp.arange(n, dtype=i32) // K
    tok = jnp.zeros((n_blk * tm,), i32).at[dest.reshape(n)].set(flat_t)
    n_used = (pad_end[-1] // tm).astype(i32).reshape(1)
    blk_e = jnp.minimum(jnp.searchsorted(pad_end, jnp.arange(n_blk, dtype=i32) * tm, side='right'),
                        E - 1).astype(i32)

    xs = jnp.take(xb, tok, axis=0)

    def row_map(i, be, nu):
        return (jnp.minimum(i, nu[0] - 1), 0)

    y = pl.pallas_call(
        functools.partial(_moe_ffn_kernel, chunks=_hidden_chunks(F)),
        grid_spec=pltpu.PrefetchScalarGridSpec(
            num_scalar_prefetch=2,
            grid=(n_blk,),
            in_specs=[pl.BlockSpec((tm, D), row_map),
                      pl.BlockSpec((1, D, F), lambda i, be, nu: (be[i], 0, 0)),
                      pl.BlockSpec((1, D, F), lambda i, be, nu: (be[i], 0, 0)),
                      pl.BlockSpec((1, F, D), lambda i, be, nu: (be[i], 0, 0))],
            out_specs=pl.BlockSpec((tm, D), row_map)),
        out_shape=jax.ShapeDtypeStruct((n_blk * tm, D), BF16),
        compiler_params=_params(("arbitrary",)),
        name="moe_ffn",
    )(blk_e, n_used, xs, w_gate, w_up, w_down)

    yk = jnp.take(y, dest.reshape(n), axis=0).reshape(T, K, D)
    return jnp.sum(yk.astype(F32) * wts[..., None], axis=1)


def _layernorm_rows(z, g, b):
    mu = jnp.mean(z, -1, keepdims=True)
    zc = z - mu
    var = jnp.mean(jnp.square(zc), -1, keepdims=True)
    return zc * lax.rsqrt(var + LN_EPS) * g + b


def _merge_kernel(x_ref, yn_ref, ym_ref, wg_ref, wbn_ref, wbm_ref, o_ref):
    x, yn, ym = x_ref[...], yn_ref[...], ym_ref[...]
    d = o_ref.shape[1]
    for c0 in range(0, d, MXU_WIDTH_V7X):
        cs = slice(c0, c0 + MXU_WIDTH_V7X)
        gn = jax.nn.sigmoid(jnp.dot(x, wg_ref[:, cs], preferred_element_type=F32))
        gm = jax.nn.sigmoid(jnp.dot(x, wg_ref[:, d + c0:d + c0 + MXU_WIDTH_V7X], preferred_element_type=F32))
        bn = jnp.dot(yn, wbn_ref[:, cs], preferred_element_type=F32)
        bm = jnp.dot(ym, wbm_ref[:, cs], preferred_element_type=F32)
        o_ref[:, cs] = (gn * bn + gm * bm).astype(o_ref.dtype)


def _merge(xb, y_nsa, y_mla, w_g, wb_nsa, wb_mla):
    T, D = xb.shape
    tm = PROJ_TM
    row = lambda a: pl.BlockSpec((tm, a.shape[1]), lambda i: (i, 0))
    return pl.pallas_call(
        _merge_kernel,
        grid=(T // tm,),
        in_specs=[row(xb), row(y_nsa), row(y_mla), _resident(w_g.shape), _resident(wb_nsa.shape),
                  _resident(wb_mla.shape)],
        out_specs=pl.BlockSpec((tm, D), lambda i: (i, 0)),
        out_shape=jax.ShapeDtypeStruct((T, D), BF16),
        compiler_params=_params(("parallel",)),
        name="merge",
    )(xb, y_nsa, y_mla, w_g, wb_nsa, wb_mla)


def _out_ln1_kernel(m_ref, x_ref, wo_ref, g_ref, b_ref, wr_ref, x1_ref, x1b_ref, sc_ref):
    m = m_ref[...]
    d = x1_ref.shape[1]
    for c0 in range(0, d, MXU_WIDTH_V7X):
        cs = slice(c0, c0 + MXU_WIDTH_V7X)
        x1_ref[:, cs] = DN_ALPHA * x_ref[:, cs] + jnp.dot(m, wo_ref[:, cs], preferred_element_type=F32)
    x1 = _layernorm_rows(x1_ref[...], g_ref[...], b_ref[...])
    x1_ref[...] = x1
    x1b = x1.astype(BF16)
    x1b_ref[...] = x1b
    sc_ref[...] = jax.nn.sigmoid(jnp.dot(x1b, wr_ref[...], preferred_element_type=F32))


def _out_ln1(merged, x2, w_out, g, b, w_router):
    T, D = x2.shape
    tm = MOE_TM
    n_r = w_router.shape[1]
    row = lambda w: pl.BlockSpec((tm, w), lambda i: (i, 0))
    return pl.pallas_call(
        _out_ln1_kernel,
        grid=(T // tm,),
        in_specs=[row(D), row(D), _resident(w_out.shape), _resident(g.shape), _resident(b.shape),
                  _resident(w_router.shape)],
        out_specs=[row(D), row(D), row(n_r)],
        out_shape=[jax.ShapeDtypeStruct((T, D), F32), jax.ShapeDtypeStruct((T, D), BF16),
                   jax.ShapeDtypeStruct((T, n_r), F32)],
        compiler_params=_params(("parallel",)),
        name="out_ln1",
    )(merged, x2, w_out, g, b, w_router)


def _final_kernel(x1_ref, r_ref, p_ref, sg_ref, su_ref, sd_ref, g_ref, b_ref, wpg_ref, wpp_ref, o_ref, *, chunks):
    x1 = x1_ref[...]
    shared = _swiglu(x1.astype(BF16), sg_ref, su_ref, sd_ref, chunks)
    x2 = _layernorm_rows(DN_ALPHA * x1 + (r_ref[...] + shared), g_ref[...], b_ref[...])
    o_ref[...] = x2
    x2b = x2.astype(BF16)
    pb = p_ref[...].astype(BF16)
    d = o_ref.shape[1]
    for c0 in range(0, d, MXU_WIDTH_V7X):
        cs = slice(c0, c0 + MXU_WIDTH_V7X)
        gate = jax.nn.sigmoid(jnp.dot(x2b, wpg_ref[:, cs], preferred_element_type=F32))
        pp = jnp.dot(pb, wpp_ref[:, cs], preferred_element_type=F32)
        o_ref[:, cs] += gate * pp


FINAL_VMEM_LIMIT_BYTES = 56 * 1024 * 1024


def _final(x1, routed, p2, sh_gate, sh_up, sh_down, g, b, w_pg, w_pp):
    T, D = x1.shape
    tm = MOE_TM
    F = sh_gate.shape[-1]
    row = lambda w: pl.BlockSpec((tm, w), lambda i: (i, 0))
    return pl.pallas_call(
        functools.partial(_final_kernel, chunks=_hidden_chunks(F)),
        grid=(T // tm,),
        in_specs=[row(D), row(D), row(p2.shape[1]), _resident(sh_gate.shape), _resident(sh_up.shape),
                  _resident(sh_down.shape), _resident(g.shape), _resident(b.shape), _resident(w_pg.shape),
                  _resident(w_pp.shape)],
        out_specs=row(D),
        out_shape=jax.ShapeDtypeStruct((T, D), F32),
        compiler_params=pltpu.CompilerParams(dimension_semantics=("parallel",),
                                             vmem_limit_bytes=FINAL_VMEM_LIMIT_BYTES),
        name="final",
    )(x1, routed, p2, sh_gate, sh_up, sh_down, g, b, w_pg, w_pp)


def _route(scores, router_b):
    T = scores.shape[0]
    biased = (scores + router_b.astype(F32)).reshape(T, N_EXPERT_GROUPS, N_EXPERTS // N_EXPERT_GROUPS)
    grp_score = jnp.sum(lax.top_k(biased, 2)[0], axis=-1)
    gsel = lax.top_k(grp_score, TOPK_GROUPS)[1]
    gmask = jnp.any(gsel[..., None] == jnp.arange(N_EXPERT_GROUPS), axis=1)
    biased = jnp.where(gmask[..., None], biased, NEG_INF).reshape(T, N_EXPERTS)
    eidx = lax.top_k(biased, TOP_K)[1]
    wts = jnp.take_along_axis(scores, eidx, axis=1)
    wts = wts / jnp.sum(wts, axis=-1, keepdims=True) * ROUTED_SCALE
    return eidx, wts


def _attention_branches(xb, S, w_in, pe_k, pe_v, wk1, wk2, wv1, wv2, q_norm, w_uq, kv_norm, w_uk, w_uv):
    T, D = xb.shape
    B = T // S
    c0 = W_NSA_Q
    c1 = c0 + W_NSA_KV
    c2 = c1 + W_NSA_GATE
    c3 = c2 + W_MLA_CQ
    c4 = c3 + W_MLA_CKV
    c5 = c4 + W_MLA_KR
    G, dk = NSA_GROUPS, NSA_DK
    pos = jnp.arange(S)
    cos128, sin128 = _rope_tables_128(pos)
    cos64, slo64, shi64 = _rope_tables_64(pos)

    qkv = _nsa_proj(xb, w_in[:, :c1].astype(BF16), cos128, sin128, S)

    pad_kr = jnp.zeros((D, LANES - W_MLA_KR), w_in.dtype)
    pad_g = jnp.zeros((D, LANES - W_NSA_GATE), w_in.dtype)
    w_m = jnp.concatenate([w_in[:, c2:c4], w_in[:, c4:c5], pad_kr, w_in[:, c1:c2], pad_g], axis=1).astype(BF16)
    H = MLA_HEADS
    wq = w_uq.reshape(MLA_Q_RANK, H, MLA_NOPE + MLA_ROPE)
    wq = jnp.concatenate([wq, jnp.zeros((MLA_Q_RANK, H, 2 * LANES - MLA_NOPE - MLA_ROPE), wq.dtype)], axis=-1)
    wq = wq.reshape(MLA_Q_RANK, H * 2 * LANES).astype(BF16)
    wukv = jnp.concatenate([w_uk, w_uv], axis=1).astype(BF16)
    q_m, kv_m, kr_m, gates = _mla_in(xb, w_m, q_norm.reshape(1, -1), kv_norm.reshape(1, -1), wq, wukv,
                                     cos64, slo64, shi64, S)

    n_chunks = S // CMP_STRIDE
    kvc = qkv[:, NSA_HEADS * dk:(NSA_HEADS + 2 * G) * dk]
    zc = kvc.reshape(B, n_chunks, CMP_STRIDE, 2, G, dk).transpose(0, 3, 4, 1, 2, 5)
    zc = zc.reshape(B, 2, G, n_chunks, CMP_STRIDE * dk)
    pe = jnp.stack([pe_k.reshape(-1), pe_v.reshape(-1)])
    pe = jnp.broadcast_to(pe[:, None, :], (2, 8, pe.shape[-1])).astype(BF16)
    half = CMP_STRIDE * dk
    w1 = jnp.stack([jnp.concatenate([wk1[:half], wk1[half:]], axis=1),
                    jnp.concatenate([wv1[:half], wv1[half:]], axis=1)]).astype(BF16)
    w2 = jnp.stack([wk2, wv2]).astype(BF16)
    pos_c = jnp.arange(n_chunks) * CMP_STRIDE + (CMP_BLK - 1)
    cos_c, sin_c = _rope_tables_128(pos_c)
    cmp_kv = _nsa_compress(zc, pe, w1, w2, cos_c, sin_c)

    g3 = gates[:, :W_NSA_GATE].reshape(B, S, G, NSA_HPG * 3).transpose(0, 2, 1, 3)
    g3 = jnp.pad(g3, ((0, 0), (0, 0), (0, 0), (0, LANES - NSA_HPG * 3)))

    y_nsa = _nsa_attention(qkv.reshape(B, S, -1), g3, cmp_kv)
    y_mla = _mla_attention(q_m.reshape(B, S, -1), kv_m.reshape(B, S, -1), kr_m.reshape(B, S, -1))
    return y_nsa, y_mla


def kernel(x, p, w_in, nsa_pe_k, nsa_pe_v, nsa_cmp_k_w1, nsa_cmp_k_w2, nsa_cmp_v_w1, nsa_cmp_v_w2, mla_q_norm, mla_w_uq, mla_kv_norm, mla_w_uk, mla_w_uv, w_branch_nsa, w_branch_mla, w_out, ln1_g, ln1_b, router_w, router_b, exp_w_gate, exp_w_up, exp_w_down, sh_w_gate, sh_w_up, sh_w_down, ln2_g, ln2_b, ple_w_proj, ple_w_gate):
    B, S, D = x.shape
    T = B * S
    c5 = D_IN - W_MERGE
    for i in range(DEPTH):
        x2 = x.reshape(T, D)
        xb = x2.astype(BF16)
        y_nsa, y_mla = _attention_branches(
            xb, S, w_in[i], nsa_pe_k[i], nsa_pe_v[i], nsa_cmp_k_w1[i], nsa_cmp_k_w2[i],
            nsa_cmp_v_w1[i], nsa_cmp_v_w2[i], mla_q_norm[i], mla_w_uq[i], mla_kv_norm[i],
            mla_w_uk[i], mla_w_uv[i])

        merged = _merge(xb, y_nsa.reshape(T, -1), y_mla.reshape(T, -1), w_in[i][:, c5:].astype(BF16),
                        w_branch_nsa[i].astype(BF16), w_branch_mla[i].astype(BF16))
        w_router = jnp.pad(router_w[i], ((0, 0), (0, LANES - N_EXPERTS))).astype(BF16)
        x1, x1b, scores = _out_ln1(merged, x2, w_out[i].astype(BF16), ln1_g[i].reshape(1, D),
                                   ln1_b[i].reshape(1, D), w_router)

        eidx, wts = _route(scores[:, :N_EXPERTS], router_b[i])
        routed = _moe_routed(x1b, eidx, wts, _cast_bf16(exp_w_gate[i]), _cast_bf16(exp_w_up[i]),
                             _cast_bf16(exp_w_down[i]))
        out = _final(x1, routed, p[i].reshape(T, PLE_DIM), sh_w_gate[i][None].astype(BF16),
                     sh_w_up[i][None].astype(BF16), sh_w_down[i][None].astype(BF16), ln2_g[i].reshape(1, D),
                     ln2_b[i].reshape(1, D), ple_w_gate[i].astype(BF16), ple_w_proj[i].astype(BF16))
        x = out.reshape(B, S, D)
    return x
```

```python
import functools
import math

import jax
import jax.numpy as jnp
from jax import lax
from jax.experimental import pallas as pl
from jax.experimental.pallas import tpu as pltpu

D_MODEL = 2048
BATCH = 8
SEQ = 4096
DEPTH = 1

PLE_DIM = 256
ROPE_THETA = 10000.0
LN_EPS = 1e-5
RMS_EPS = 1e-6
NEG_INF = -1e30

NSA_HEADS = 8
NSA_GROUPS = 2
NSA_HPG = NSA_HEADS // NSA_GROUPS
NSA_DK = 128
NSA_DV = 128
CMP_BLK = 32
CMP_STRIDE = 16
CMP_HID = 256
SLC_BLK = 64
N_SEL = 16
WIN = 512
FORCE_SCORE = 1e4

MLA_HEADS = 8
MLA_Q_RANK = 768
MLA_KV_RANK = 512
MLA_NOPE = 128
MLA_ROPE = 64
MLA_DV = 128

N_EXPERTS = 64
TOP_K = 6
N_EXPERT_GROUPS = 8
TOPK_GROUPS = 4
D_EXPERT = 1408
ROUTED_SCALE = 2.5

W_NSA_Q = NSA_HEADS * NSA_DK
W_NSA_KV = 3 * NSA_GROUPS * (NSA_DK + NSA_DV)
W_NSA_GATE = 3 * NSA_HEADS
W_MLA_CQ = MLA_Q_RANK
W_MLA_CKV = MLA_KV_RANK
W_MLA_KR = MLA_ROPE
W_MERGE = 2 * D_MODEL
D_IN = W_NSA_Q + W_NSA_KV + W_NSA_GATE + W_MLA_CQ + W_MLA_CKV + W_MLA_KR + W_MERGE

DN_ALPHA = (2.0 * DEPTH) ** 0.25
DN_BETA = (8.0 * DEPTH) ** -0.25

V7X_VMEM_LIMIT_BYTES = 48 * 1024 * 1024
LANES = 128
MXU_WIDTH_V7X = 256
PROJ_TM = 512

BF16 = jnp.bfloat16
F32 = jnp.float32


def _params(sem):
    return pltpu.CompilerParams(dimension_semantics=sem, vmem_limit_bytes=V7X_VMEM_LIMIT_BYTES)


def _resident(shape):
    return pl.BlockSpec(shape, lambda *_: (0,) * len(shape), pipeline_mode=pl.Buffered(1))


def _rope_tables_128(pos):
    half = LANES // 2
    inv = jnp.power(ROPE_THETA, -jnp.arange(half, dtype=F32) * 2.0 / LANES)
    ang = pos.astype(F32)[:, None] * inv[None, :]
    cos, sin = jnp.cos(ang), jnp.sin(ang)
    return jnp.concatenate([cos, cos], -1), jnp.concatenate([-sin, sin], -1)


def _rope_tables_64(pos):
    half = MLA_ROPE // 2
    inv = jnp.power(ROPE_THETA, -jnp.arange(half, dtype=F32) * 2.0 / MLA_ROPE)
    ang = pos.astype(F32)[:, None] * inv[None, :]
    cos, sin = jnp.cos(ang), jnp.sin(ang)
    zero = jnp.zeros_like(sin)
    return (jnp.concatenate([cos, cos, cos, cos], -1),
            jnp.concatenate([-sin, zero, -sin, zero], -1),
            jnp.concatenate([zero, sin, zero, sin], -1))


def _rope128(z, cos, sin):
    return z * cos + pltpu.roll(z, LANES // 2, 1) * sin


def _rope64(z, cos, sin_lo, sin_hi):
    return z * cos + pltpu.roll(z, LANES - MLA_ROPE // 2, 1) * sin_lo + pltpu.roll(z, MLA_ROPE // 2, 1) * sin_hi


def _nsa_proj_kernel(x_ref, w_ref, cos_ref, sin_ref, o_ref, *, rope_heads, n_q_heads, scale):
    x = x_ref[...]
    cos, sin = cos_ref[...], sin_ref[...]
    n_heads = w_ref.shape[1] // LANES
    for t in range(n_heads // 2):
        c0 = t * MXU_WIDTH_V7X
        z = jnp.dot(x, w_ref[:, c0:c0 + MXU_WIDTH_V7X], preferred_element_type=F32)
        for hh in range(2):
            head = 2 * t + hh
            zh = z[:, hh * LANES:(hh + 1) * LANES]
            if head in rope_heads:
                zh = _rope128(zh, cos, sin)
            if head < n_q_heads:
                zh = zh * scale
            o_ref[:, head * LANES:(head + 1) * LANES] = zh.astype(o_ref.dtype)


def _nsa_proj(xb, w, cos, sin, S):
    T, D = xb.shape
    n = w.shape[1]
    tm = min(PROJ_TM, S)
    n_pos = S // tm
    kv0 = NSA_HEADS
    per = NSA_GROUPS * 2
    rope_heads = tuple(range(NSA_HEADS)) + tuple(kv0 + per * br + g for br in (1, 2) for g in range(NSA_GROUPS))
    return pl.pallas_call(
        functools.partial(_nsa_proj_kernel, rope_heads=rope_heads, n_q_heads=NSA_HEADS, scale=NSA_DK ** -0.5),
        grid=(T // tm,),
        in_specs=[pl.BlockSpec((tm, D), lambda i: (i, 0)),
                  _resident((D, n)),
                  pl.BlockSpec((tm, LANES), lambda i: (i % n_pos, 0)),
                  pl.BlockSpec((tm, LANES), lambda i: (i % n_pos, 0))],
        out_specs=pl.BlockSpec((tm, n), lambda i: (i, 0)),
        out_shape=jax.ShapeDtypeStruct((T, n), BF16),
        compiler_params=_params(("parallel",)),
        name="nsa_proj",
    )(xb, w, cos, sin)


def _rmsnorm_rows(z, g):
    return z * lax.rsqrt(jnp.mean(jnp.square(z), -1, keepdims=True) + RMS_EPS) * g


def _mla_in_kernel(x_ref, w_ref, gq_ref, gkv_ref, wuq_ref, wukv_ref, cos_ref, slo_ref, shi_ref,
                   q_ref, kv_ref, kr_ref, gate_ref, *, scale):
    x = x_ref[...]
    cos, slo, shi = cos_ref[...], slo_ref[...], shi_ref[...]
    c1 = MLA_Q_RANK
    c2 = c1 + MLA_KV_RANK
    cq = jnp.dot(x, w_ref[:, :c1], preferred_element_type=F32)
    cqn = _rmsnorm_rows(cq, gq_ref[...]).astype(BF16)
    ckv = jnp.dot(x, w_ref[:, c1:c2], preferred_element_type=F32)
    ckvn = _rmsnorm_rows(ckv, gkv_ref[...]).astype(BF16)
    tail = jnp.dot(x, w_ref[:, c2:c2 + 2 * LANES], preferred_element_type=F32)
    kr_ref[...] = _rope64(tail[:, :LANES], cos, slo, shi).astype(kr_ref.dtype)
    gate_ref[...] = jax.nn.sigmoid(tail[:, LANES:])
    for h in range(MLA_HEADS):
        c0 = h * MXU_WIDTH_V7X
        z = jnp.dot(cqn, wuq_ref[:, c0:c0 + MXU_WIDTH_V7X], preferred_element_type=F32)
        q_ref[:, c0:c0 + LANES] = (z[:, :LANES] * scale).astype(q_ref.dtype)
        q_ref[:, c0 + LANES:c0 + 2 * LANES] = (_rope64(z[:, LANES:], cos, slo, shi) * scale).astype(q_ref.dtype)
    n_kv = wukv_ref.shape[1]
    for c0 in range(0, n_kv, MXU_WIDTH_V7X):
        kv_ref[:, c0:c0 + MXU_WIDTH_V7X] = jnp.dot(
            ckvn, wukv_ref[:, c0:c0 + MXU_WIDTH_V7X], preferred_element_type=F32).astype(kv_ref.dtype)


def _mla_in(xb, w, gq, gkv, wuq, wukv, cos, slo, shi, S):
    T, D = xb.shape
    tm = min(PROJ_TM, S)
    n_pos = S // tm
    tbl = pl.BlockSpec((tm, LANES), lambda i: (i % n_pos, 0))
    nq, nkv = wuq.shape[1], wukv.shape[1]
    return pl.pallas_call(
        functools.partial(_mla_in_kernel, scale=(MLA_NOPE + MLA_ROPE) ** -0.5),
        grid=(T // tm,),
        in_specs=[pl.BlockSpec((tm, D), lambda i: (i, 0)),
                  _resident(w.shape), _resident(gq.shape), _resident(gkv.shape),
                  _resident(wuq.shape), _resident(wukv.shape), tbl, tbl, tbl],
        out_specs=[pl.BlockSpec((tm, nq), lambda i: (i, 0)),
                   pl.BlockSpec((tm, nkv), lambda i: (i, 0)),
                   pl.BlockSpec((tm, LANES), lambda i: (i, 0)),
                   pl.BlockSpec((tm, LANES), lambda i: (i, 0))],
        out_shape=[jax.ShapeDtypeStruct((T, nq), BF16), jax.ShapeDtypeStruct((T, nkv), BF16),
                   jax.ShapeDtypeStruct((T, LANES), BF16), jax.ShapeDtypeStruct((T, LANES), F32)],
        compiler_params=_params(("parallel",)),
        name="mla_in",
    )(xb, w, gq, gkv, wuq, wukv, cos, slo, shi)


MASK_BIG = 16384.0


def _softmax_reset(m_s, l_s, acc_s):
    m_s[...] = jnp.full(m_s.shape, NEG_INF, F32)
    l_s[...] = jnp.zeros(l_s.shape, F32)
    acc_s[...] = jnp.zeros(acc_s.shape, F32)


def _online_softmax_step(s, v, m_s, l_s, acc_s):
    m_prev = m_s[...]
    m_new = jnp.maximum(m_prev, jnp.max(s, axis=-1, keepdims=True))
    alpha = jnp.exp(m_prev - m_new)
    p = jnp.exp(s - m_new)
    l_s[...] = alpha * l_s[...] + jnp.sum(p, axis=-1, keepdims=True)
    acc_s[...] = alpha * acc_s[...] + jnp.dot(p.astype(BF16), v, preferred_element_type=F32)
    m_s[...] = m_new


def _dot_nt(a, b):
    return lax.dot_general(a, b, (((1,), (1,)), ((), ())), preferred_element_type=F32)


MLA_TQ = 512
MLA_TK = 512


MLA_HPS = 2


def _mla_attn_kernel(q_ref, kn_ref, kr_ref, v_ref, o_ref, m_s, l_s, acc_s, *, tq, tk, hps):
    q0 = pl.program_id(2) * tq
    qs = [q_ref[0, :, hh * 2 * LANES:(hh + 1) * 2 * LANES] for hh in range(hps)]
    _softmax_reset(m_s, l_s, acc_s)

    def tile(j, causal):
        k0 = pl.multiple_of(j * tk, tk)
        kr = kr_ref[0, pl.ds(k0, tk), :]
        for hh in range(hps):
            hs = slice(hh * LANES, (hh + 1) * LANES)
            ka = jnp.concatenate([kn_ref[0, pl.ds(k0, tk), hs], kr], axis=1)
            s = _dot_nt(qs[hh], ka)
            if causal:
                qpos = q0 + lax.broadcasted_iota(jnp.int32, (tq, tk), 0)
                kpos = k0 + lax.broadcasted_iota(jnp.int32, (tq, tk), 1)
                s = jnp.where(kpos <= qpos, s, -MASK_BIG)
            _online_softmax_step(s, v_ref[0, pl.ds(k0, tk), hs], m_s.at[hh], l_s.at[hh], acc_s.at[hh])

    j_diag = q0 // tk

    def body(j, carry):
        tile(j, False)
        return carry

    lax.fori_loop(0, j_diag, body, 0)
    tile(j_diag, True)
    for hh in range(hps):
        o_ref[0, :, hh * LANES:(hh + 1) * LANES] = (acc_s[hh] / l_s[hh]).astype(o_ref.dtype)


def _mla_attention(q, kv, kr):
    B, S, _ = q.shape
    H = MLA_HEADS
    hps = MLA_HPS
    tq = tk = min(MLA_TQ, S)
    return pl.pallas_call(
        functools.partial(_mla_attn_kernel, tq=tq, tk=tk, hps=hps),
        grid=(B, H // hps, S // tq),
        in_specs=[pl.BlockSpec((1, tq, hps * 2 * LANES), lambda b, h, i: (b, i, h)),
                  pl.BlockSpec((1, S, hps * LANES), lambda b, h, i: (b, 0, h)),
                  pl.BlockSpec((1, S, LANES), lambda b, h, i: (b, 0, 0)),
                  pl.BlockSpec((1, S, hps * LANES), lambda b, h, i: (b, 0, H // hps + h))],
        out_specs=pl.BlockSpec((1, tq, hps * LANES), lambda b, h, i: (b, i, h)),
        out_shape=jax.ShapeDtypeStruct((B, S, H * MLA_DV), BF16),
        scratch_shapes=[pltpu.VMEM((hps, tq, 1), F32), pltpu.VMEM((hps, tq, 1), F32),
                        pltpu.VMEM((hps, tq, MLA_DV), F32)],
        compiler_params=_params(("parallel", "parallel", "arbitrary")),
        name="mla_attn",
    )(q, kv, kr, kv)


def _gelu_tanh(x):
    return 0.5 * x * (1.0 + jnp.tanh(math.sqrt(2.0 / math.pi) * (x + 0.044715 * (x * x * x))))


def _compress_kernel(z_ref, pe_ref, w1_ref, w2_ref, cos_ref, sin_ref, o_ref, *, hid):
    which = pl.program_id(1)
    z = z_ref[0, 0, 0]
    n_chunks = z.shape[0]
    cd = z.shape[1]
    ab = jnp.dot(z, w1_ref[0], preferred_element_type=F32)
    a_part = ab[:, :hid]
    b_next = pltpu.roll(ab[:, hid:], n_chunks - 1, 0)
    pe_a = jnp.dot(pe_ref[0, :, :cd], w1_ref[0, :, :hid], preferred_element_type=F32)
    pe_b = jnp.dot(pe_ref[0, :, cd:], w1_ref[0, :, hid:], preferred_element_type=F32)
    h = _gelu_tanh(a_part + b_next + pe_a[0:1] + pe_b[0:1])
    out = jnp.dot(h.astype(BF16), w2_ref[0], preferred_element_type=F32)
    roped = _rope128(out, cos_ref[...], sin_ref[...])
    o_ref[0, 0, 0] = jnp.where(which == 0, roped, out).astype(o_ref.dtype)


def _nsa_compress(zc, pe, w1, w2, cos_c, sin_c):
    B, _, G, n_chunks, cd = zc.shape
    hid = w2.shape[1]
    d = w2.shape[2]
    return pl.pallas_call(
        functools.partial(_compress_kernel, hid=hid),
        grid=(B, 2, G),
        in_specs=[pl.BlockSpec((1, 1, 1, n_chunks, cd), lambda b, t, g: (b, t, g, 0, 0)),
                  pl.BlockSpec((1, 8, 2 * cd), lambda b, t, g: (t, 0, 0)),
                  pl.BlockSpec((1, cd, 2 * hid), lambda b, t, g: (t, 0, 0)),
                  pl.BlockSpec((1, hid, d), lambda b, t, g: (t, 0, 0)),
                  pl.BlockSpec((n_chunks, LANES), lambda b, t, g: (0, 0)),
                  pl.BlockSpec((n_chunks, LANES), lambda b, t, g: (0, 0))],
        out_specs=pl.BlockSpec((1, 1, 1, n_chunks, d), lambda b, t, g: (b, t, g, 0, 0)),
        out_shape=jax.ShapeDtypeStruct((B, 2, G, n_chunks, d), BF16),
        compiler_params=_params(("parallel", "parallel", "parallel")),
        name="nsa_compress",
    )(zc, pe, w1, w2, cos_c, sin_c)


NSA_TQ = 256
NSA_TK = 512


def _nsa_kernel(q_ref, g_ref, kc_ref, vc_ref, ks_ref, vs_ref, kw_ref, vw_ref, o_ref,
                m_s, l_s, acc_s, out_s, *, tq, tk, hpg, dk, n_cmp, n_slc):
    q0 = pl.program_id(2) * tq
    rows = hpg * tq
    i32 = jnp.int32
    q = jnp.concatenate([q_ref[0, :, h * dk:(h + 1) * dk] for h in range(hpg)], axis=0)
    gates = g_ref[0, 0]

    def gate_col(branch):
        return jnp.concatenate([gates[:, 3 * h + branch:3 * h + branch + 1] for h in range(hpg)], axis=0)

    n_pad = kc_ref.shape[3]
    sc = _dot_nt(q, kc_ref[0, 0, 0])
    qpos_r = q0 + lax.rem(lax.broadcasted_iota(i32, (rows, n_pad), 0), tq)
    n_io = lax.broadcasted_iota(i32, (rows, n_pad), 1)
    vis = (n_io * CMP_STRIDE + (CMP_BLK - 1) <= qpos_r) & (n_io < n_cmp)
    sc = jnp.where(vis, sc, NEG_INF)
    e = jnp.where(vis, jnp.exp(sc - jnp.max(sc, axis=-1, keepdims=True)), 0.0)
    p = e / jnp.maximum(jnp.sum(e, axis=-1, keepdims=True), 1e-30)
    o_cmp = jnp.dot(p.astype(BF16), vc_ref[0, 0, 0], preferred_element_type=F32)
    out_s[...] = gate_col(0) * o_cmp

    imp = p[0:tq]
    for h in range(1, hpg):
        imp = imp + p[h * tq:(h + 1) * tq]
    ratio = SLC_BLK // CMP_STRIDE
    span = CMP_BLK // CMP_STRIDE
    j_p = lax.broadcasted_iota(i32, (n_slc, n_pad), 0)
    c_p = lax.broadcasted_iota(i32, (n_slc, n_pad), 1)
    lo_c = ratio * j_p - (span - 1)
    pool = jnp.where((c_p >= lo_c) & (c_p < lo_c + ratio + span - 1), 1.0, 0.0).astype(BF16)
    hi = imp.astype(BF16)
    r1 = imp - hi.astype(F32)
    mid = r1.astype(BF16)
    lo = (r1 - mid.astype(F32)).astype(BF16)
    p_slc_t = _dot_nt(pool, hi) + _dot_nt(pool, mid) + _dot_nt(pool, lo)

    j_io = lax.broadcasted_iota(i32, (n_slc, tq), 0)
    qpos_l = q0 + lax.broadcasted_iota(i32, (n_slc, tq), 1)
    cur = qpos_l // SLC_BLK
    forced = (j_io == 0) | (j_io == cur) | (j_io == cur - 1)
    valid = j_io * SLC_BLK <= qpos_l
    score = jnp.where(forced, FORCE_SCORE, jnp.where(valid, p_slc_t, -FORCE_SCORE))
    cnt = jnp.zeros((n_slc, tq), i32)
    for i in range(n_slc):
        row = score[i:i + 1, :]
        beats = (row > score) | ((row == score) & (j_io > i))
        cnt = cnt + beats.astype(i32)
    bias_t = jnp.where(cnt < N_SEL, 0.0, -MASK_BIG)
    if n_slc < LANES:
        bias_t = jnp.concatenate([bias_t, jnp.full((LANES - n_slc, tq), -MASK_BIG, F32)], axis=0)
    bias = jnp.transpose(bias_t).astype(BF16)
    qa = jnp.concatenate([q, jnp.concatenate([bias] * hpg, axis=0)], axis=1)

    qpos_k = q0 + lax.rem(lax.broadcasted_iota(i32, (rows, tk), 0), tq)
    k_io = lax.broadcasted_iota(i32, (rows, tk), 1)

    _softmax_reset(m_s, l_s, acc_s)

    def slc_tile(j, causal):
        k0 = pl.multiple_of(j * tk, tk)
        kblk = (k0 + lax.broadcasted_iota(i32, (tk, LANES), 0)) // SLC_BLK
        onehot = jnp.where(kblk == lax.broadcasted_iota(i32, (tk, LANES), 1), 1.0, 0.0).astype(BF16)
        ka = jnp.concatenate([ks_ref[0, pl.ds(k0, tk), :], onehot], axis=1)
        s = _dot_nt(qa, ka)
        if causal:
            s = jnp.where(k0 + k_io <= qpos_k, s, -MASK_BIG)
        _online_softmax_step(s, vs_ref[0, pl.ds(k0, tk), :], m_s, l_s, acc_s)

    j_diag = q0 // tk

    def body(j, carry):
        slc_tile(j, False)
        return carry

    lax.fori_loop(0, j_diag, body, 0)
    slc_tile(j_diag, True)
    out_s[...] += gate_col(1) * (acc_s[...] / l_s[...])

    _softmax_reset(m_s, l_s, acc_s)

    def win_tile(j):
        k0 = pl.multiple_of(j * tk, tk)
        s = _dot_nt(q, kw_ref[0, pl.ds(k0, tk), :])
        kpos = k0 + k_io
        s = jnp.where((kpos <= qpos_k) & (kpos > qpos_k - WIN), s, -MASK_BIG)
        _online_softmax_step(s, vw_ref[0, pl.ds(k0, tk), :], m_s, l_s, acc_s)

    @pl.when(j_diag >= 1)
    def _():
        win_tile(j_diag - 1)

    win_tile(j_diag)
    out = out_s[...] + gate_col(2) * (acc_s[...] / l_s[...])
    o_ref[0] = jnp.concatenate([out[h * tq:(h + 1) * tq] for h in range(hpg)], axis=1).astype(o_ref.dtype)


def _nsa_attention(qkv, gates, cmp_kv):
    B, S, _ = qkv.shape
    G, hpg, dk = NSA_GROUPS, NSA_HPG, NSA_DK
    n_pad = cmp_kv.shape[3]
    n_cmp = (S - CMP_BLK) // CMP_STRIDE + 1
    tq, tk = min(NSA_TQ, S), min(NSA_TK, S)
    assert tk == WIN and tq <= tk
    rows = hpg * tq
    kv0 = NSA_HEADS

    def kv_spec(branch, is_v):
        base = kv0 + 2 * G * branch + G * is_v
        return pl.BlockSpec((1, S, dk), lambda b, g, i: (b, 0, base + g))

    return pl.pallas_call(
        functools.partial(_nsa_kernel, tq=tq, tk=tk, hpg=hpg, dk=dk, n_cmp=n_cmp, n_slc=S // SLC_BLK),
        grid=(B, G, S // tq),
        in_specs=[pl.BlockSpec((1, tq, hpg * dk), lambda b, g, i: (b, i, g)),
                  pl.BlockSpec((1, 1, tq, LANES), lambda b, g, i: (b, g, i, 0)),
                  pl.BlockSpec((1, 1, 1, n_pad, dk), lambda b, g, i: (b, 0, g, 0, 0)),
                  pl.BlockSpec((1, 1, 1, n_pad, dk), lambda b, g, i: (b, 1, g, 0, 0)),
                  kv_spec(1, 0), kv_spec(1, 1), kv_spec(2, 0), kv_spec(2, 1)],
        out_specs=pl.BlockSpec((1, tq, hpg * dk), lambda b, g, i: (b, i, g)),
        out_shape=jax.ShapeDtypeStruct((B, S, NSA_HEADS * NSA_DV), BF16),
        scratch_shapes=[pltpu.VMEM((rows, 1), F32), pltpu.VMEM((rows, 1), F32),
                        pltpu.VMEM((rows, dk), F32), pltpu.VMEM((rows, dk), F32)],
        compiler_params=_params(("parallel", "parallel", "arbitrary")),
        name="nsa_attn",
    )(qkv, gates, cmp_kv, cmp_kv, qkv, qkv, qkv, qkv)


MOE_TM = 256


def _hidden_chunks(f):
    chunks, c0 = [], 0
    while c0 < f:
        cs = min(MXU_WIDTH_V7X, f - c0)
        chunks.append((c0, cs))
        c0 += cs
    return tuple(chunks)


def _swiglu(x, wg_ref, wu_ref, wd_ref, chunks):
    acc = None
    for c0, cs in chunks:
        g = jnp.dot(x, wg_ref[0, :, c0:c0 + cs], preferred_element_type=F32)
        u = jnp.dot(x, wu_ref[0, :, c0:c0 + cs], preferred_element_type=F32)
        h = (g * jax.nn.sigmoid(g) * u).astype(BF16)
        y = jnp.dot(h, wd_ref[0, c0:c0 + cs, :], preferred_element_type=F32)
        acc = y if acc is None else acc + y
    return acc


def _moe_ffn_kernel(blk_e_ref, n_used_ref, x_ref, wg_ref, wu_ref, wd_ref, o_ref, *, chunks):
    del blk_e_ref

    @pl.when(pl.program_id(0) < n_used_ref[0])
    def _():
        o_ref[...] = _swiglu(x_ref[...], wg_ref, wu_ref, wd_ref, chunks).astype(o_ref.dtype)


def _cast_kernel(w_ref, o_ref):
    o_ref[...] = w_ref[...].astype(o_ref.dtype)


def _cast_bf16(w):
    E, R, C = w.shape
    return pl.pallas_call(
        _cast_kernel,
        grid=(E,),
        in_specs=[pl.BlockSpec((1, R, C), lambda e: (e, 0, 0))],
        out_specs=pl.BlockSpec((1, R, C), lambda e: (e, 0, 0)),
        out_shape=jax.ShapeDtypeStruct(w.shape, BF16),
        compiler_params=_params(("parallel",)),
        name="cast_bf16",
    )(w)


def _moe_routed(xb, eidx, w_gate, w_up, w_down):
    T, D = xb.shape
    K = eidx.shape[1]
    E, _, F = w_gate.shape
    tm = MOE_TM
    n = T * K
    n_blk = -(-n // tm) + E
    i32 = jnp.int32

    onehot = jnp.any(eidx[..., None] == jnp.arange(E, dtype=eidx.dtype), axis=1).astype(i32)
    cum = jnp.cumsum(onehot, axis=0)
    counts = cum[-1]
    rank = jnp.take_along_axis(cum, eidx, axis=1) - 1
    padded = (counts + tm - 1) // tm * tm
    pad_end = jnp.cumsum(padded)
    pad_start = pad_end - padded
    dest = (pad_start[eidx] + rank).astype(i32)
    flat_t = jnp.arange(n, dtype=i32) // K
    tok = jnp.zeros((n_blk * tm,), i32).at[dest.reshape(n)].set(flat_t)
    n_used = (pad_end[-1] // tm).astype(i32).reshape(1)
    blk_e = jnp.minimum(jnp.searchsorted(pad_end, jnp.arange(n_blk, dtype=i32) * tm, side='right'),
                        E - 1).astype(i32)

    xs = jnp.take(xb, tok, axis=0)

    def row_map(i, be, nu):
        return (jnp.minimum(i, nu[0] - 1), 0)

    y = pl.pallas_call(
        functools.partial(_moe_ffn_kernel, chunks=_hidden_chunks(F)),
        grid_spec=pltpu.PrefetchScalarGridSpec(
            num_scalar_prefetch=2,
            grid=(n_blk,),
            in_specs=[pl.BlockSpec((tm, D), row_map),
                      pl.BlockSpec((1, D, F), lambda i, be, nu: (be[i], 0, 0)),
                      pl.BlockSpec((1, D, F), lambda i, be, nu: (be[i], 0, 0)),
                      pl.BlockSpec((1, F, D), lambda i, be, nu: (be[i], 0, 0))],
            out_specs=pl.BlockSpec((tm, D), row_map)),
        out_shape=jax.ShapeDtypeStruct((n_blk * tm, D), BF16),
        compiler_params=_params(("arbitrary",)),
        name="moe_ffn",
    )(blk_e, n_used, xs, w_gate, w_up, w_down)

    return jnp.take(y, dest.T.reshape(n), axis=0).reshape(K, T, D)


def _layernorm_rows(z, g, b):
    mu = jnp.mean(z, -1, keepdims=True)
    zc = z - mu
    var = jnp.mean(jnp.square(zc), -1, keepdims=True)
    return zc * lax.rsqrt(var + LN_EPS) * g + b


def _merge_kernel(x_ref, yn_ref, ym_ref, wg_ref, wbn_ref, wbm_ref, o_ref):
    x, yn, ym = x_ref[...], yn_ref[...], ym_ref[...]
    d = o_ref.shape[1]
    for c0 in range(0, d, MXU_WIDTH_V7X):
        cs = slice(c0, c0 + MXU_WIDTH_V7X)
        gn = jax.nn.sigmoid(jnp.dot(x, wg_ref[:, cs], preferred_element_type=F32))
        gm = jax.nn.sigmoid(jnp.dot(x, wg_ref[:, d + c0:d + c0 + MXU_WIDTH_V7X], preferred_element_type=F32))
        bn = jnp.dot(yn, wbn_ref[:, cs], preferred_element_type=F32)
        bm = jnp.dot(ym, wbm_ref[:, cs], preferred_element_type=F32)
        o_ref[:, cs] = (gn * bn + gm * bm).astype(o_ref.dtype)


def _merge(xb, y_nsa, y_mla, w_g, wb_nsa, wb_mla):
    T, D = xb.shape
    tm = PROJ_TM
    row = lambda a: pl.BlockSpec((tm, a.shape[1]), lambda i: (i, 0))
    return pl.pallas_call(
        _merge_kernel,
        grid=(T // tm,),
        in_specs=[row(xb), row(y_nsa), row(y_mla), _resident(w_g.shape), _resident(wb_nsa.shape),
                  _resident(wb_mla.shape)],
        out_specs=pl.BlockSpec((tm, D), lambda i: (i, 0)),
        out_shape=jax.ShapeDtypeStruct((T, D), BF16),
        compiler_params=_params(("parallel",)),
        name="merge",
    )(xb, y_nsa, y_mla, w_g, wb_nsa, wb_mla)


def _rank_before(v):
    n = v.shape[0]
    j_io = lax.broadcasted_iota(jnp.int32, v.shape, 0)
    cnt = jnp.zeros(v.shape, jnp.int32)
    for i in range(n):
        row = v[i:i + 1, :]
        cnt = cnt + ((row > v) | ((row == v) & (j_io > i))).astype(jnp.int32)
    return cnt


def _route_tile(logits_t, rb):
    E, tm = logits_t.shape
    per = E // N_EXPERT_GROUPS
    scores = jax.nn.sigmoid(logits_t)
    biased = scores + rb
    grp = []
    for g in range(N_EXPERT_GROUPS):
        blk = biased[g * per:(g + 1) * per, :]
        m1 = jnp.max(blk, axis=0, keepdims=True)
        eq = blk == m1
        n_eq = jnp.sum(eq.astype(F32), axis=0, keepdims=True)
        m2 = jnp.max(jnp.where(eq, NEG_INF, blk), axis=0, keepdims=True)
        grp.append(m1 + jnp.where(n_eq >= 2.0, m1, m2))
    gsel = _rank_before(jnp.concatenate(grp, axis=0)) < TOPK_GROUPS
    masked = jnp.concatenate(
        [jnp.where(gsel[g:g + 1, :], biased[g * per:(g + 1) * per, :], NEG_INF) for g in range(N_EXPERT_GROUPS)],
        axis=0)
    sel = _rank_before(masked) < TOP_K
    w_sel = jnp.where(sel, scores, 0.0)
    w_sel = w_sel / jnp.sum(w_sel, axis=0, keepdims=True) * ROUTED_SCALE
    tri = jnp.where(lax.broadcasted_iota(jnp.int32, (E, E), 1) <= lax.broadcasted_iota(jnp.int32, (E, E), 0),
                    1.0, 0.0).astype(BF16)
    slot = jnp.dot(tri, jnp.where(sel, 1.0, 0.0).astype(BF16), preferred_element_type=F32)
    e_io = lax.broadcasted_iota(jnp.int32, (E, tm), 0)
    ids, wts = [], []
    for k in range(TOP_K):
        mk = sel & (slot == float(k + 1))
        ids.append(jnp.sum(jnp.where(mk, e_io, 0), axis=0, keepdims=True))
        wts.append(jnp.sum(jnp.where(mk, w_sel, 0.0), axis=0, keepdims=True))
    pad = 8 - TOP_K
    ids.append(jnp.zeros((pad, tm), jnp.int32))
    wts.append(jnp.zeros((pad, tm), F32))
    return jnp.concatenate(ids, axis=0), jnp.concatenate(wts, axis=0)


def _out_ln1_kernel(m_ref, x_ref, wo_ref, g_ref, b_ref, wrt_ref, rb_ref, x1_ref, x1b_ref, eid_ref, wts_ref):
    m = m_ref[...]
    d = x1_ref.shape[1]
    for c0 in range(0, d, MXU_WIDTH_V7X):
        cs = slice(c0, c0 + MXU_WIDTH_V7X)
        x1_ref[:, cs] = DN_ALPHA * x_ref[:, cs] + jnp.dot(m, wo_ref[:, cs], preferred_element_type=F32)
    x1 = _layernorm_rows(x1_ref[...], g_ref[...], b_ref[...])
    x1_ref[...] = x1
    x1b = x1.astype(BF16)
    x1b_ref[...] = x1b
    logits_t = _dot_nt(wrt_ref[...], x1b)
    eid_ref[...], wts_ref[...] = _route_tile(logits_t, rb_ref[...])


def _out_ln1(merged, x2, w_out, g, b, w_router_t, router_b):
    T, D = x2.shape
    tm = MOE_TM
    row = lambda w: pl.BlockSpec((tm, w), lambda i: (i, 0))
    col = pl.BlockSpec((8, tm), lambda i: (0, i))
    return pl.pallas_call(
        _out_ln1_kernel,
        grid=(T // tm,),
        in_specs=[row(D), row(D), _resident(w_out.shape), _resident(g.shape), _resident(b.shape),
                  _resident(w_router_t.shape), _resident(router_b.shape)],
        out_specs=[row(D), row(D), col, col],
        out_shape=[jax.ShapeDtypeStruct((T, D), F32), jax.ShapeDtypeStruct((T, D), BF16),
                   jax.ShapeDtypeStruct((8, T), jnp.int32), jax.ShapeDtypeStruct((8, T), F32)],
        compiler_params=_params(("parallel",)),
        name="out_ln1",
    )(merged, x2, w_out, g, b, w_router_t, router_b)


def _ffn_ln2_kernel(x1_ref, yk_ref, w_ref, sg_ref, su_ref, sd_ref, g_ref, b_ref, x2_ref, x2b_ref, *, chunks):
    x1 = x1_ref[...]
    acc = _swiglu(x1.astype(BF16), sg_ref, su_ref, sd_ref, chunks)
    w = w_ref[...]
    for k in range(yk_ref.shape[0]):
        acc = acc + yk_ref[k].astype(F32) * w[:, k:k + 1]
    x2 = _layernorm_rows(DN_ALPHA * x1 + acc, g_ref[...], b_ref[...])
    x2_ref[...] = x2
    x2b_ref[...] = x2.astype(BF16)


FFN_LN2_VMEM_LIMIT_BYTES = 56 * 1024 * 1024


def _ffn_ln2(x1, yk, wts, sh_gate, sh_up, sh_down, g, b):
    T, D = x1.shape
    K = yk.shape[0]
    tm = MOE_TM
    F = sh_gate.shape[-1]
    row = lambda w: pl.BlockSpec((tm, w), lambda i: (i, 0))
    return pl.pallas_call(
        functools.partial(_ffn_ln2_kernel, chunks=_hidden_chunks(F)),
        grid=(T // tm,),
        in_specs=[row(D), pl.BlockSpec((K, tm, D), lambda i: (0, i, 0)), row(wts.shape[1]),
                  _resident(sh_gate.shape), _resident(sh_up.shape), _resident(sh_down.shape),
                  _resident(g.shape), _resident(b.shape)],
        out_specs=[row(D), row(D)],
        out_shape=[jax.ShapeDtypeStruct((T, D), F32), jax.ShapeDtypeStruct((T, D), BF16)],
        compiler_params=pltpu.CompilerParams(dimension_semantics=("parallel",),
                                             vmem_limit_bytes=FFN_LN2_VMEM_LIMIT_BYTES),
        name="ffn_ln2",
    )(x1, yk, wts, sh_gate, sh_up, sh_down, g, b)


def _ple_kernel(x2_ref, x2b_ref, p_ref, wpg_ref, wpp_ref, o_ref):
    x2b = x2b_ref[...]
    pb = p_ref[...].astype(BF16)
    d = o_ref.shape[1]
    for c0 in range(0, d, MXU_WIDTH_V7X):
        cs = slice(c0, c0 + MXU_WIDTH_V7X)
        gate = jax.nn.sigmoid(jnp.dot(x2b, wpg_ref[:, cs], preferred_element_type=F32))
        pp = jnp.dot(pb, wpp_ref[:, cs], preferred_element_type=F32)
        o_ref[:, cs] = x2_ref[:, cs] + gate * pp


def _ple(x2, x2b, p2, w_pg, w_pp):
    T, D = x2.shape
    tm = PROJ_TM
    row = lambda w: pl.BlockSpec((tm, w), lambda i: (i, 0))
    return pl.pallas_call(
        _ple_kernel,
        grid=(T // tm,),
        in_specs=[row(D), row(D), row(p2.shape[1]), _resident(w_pg.shape), _resident(w_pp.shape)],
        out_specs=row(D),
        out_shape=jax.ShapeDtypeStruct((T, D), F32),
        compiler_params=_params(("parallel",)),
        name="ple",
    )(x2, x2b, p2, w_pg, w_pp)


def _attention_branches(xb, S, w_in, pe_k, pe_v, wk1, wk2, wv1, wv2, q_norm, w_uq, kv_norm, w_uk, w_uv):
    T, D = xb.shape
    B = T // S
    c0 = W_NSA_Q
    c1 = c0 + W_NSA_KV
    c2 = c1 + W_NSA_GATE
    c3 = c2 + W_MLA_CQ
    c4 = c3 + W_MLA_CKV
    c5 = c4 + W_MLA_KR
    G, dk = NSA_GROUPS, NSA_DK
    pos = jnp.arange(S)
    cos128, sin128 = _rope_tables_128(pos)
    cos64, slo64, shi64 = _rope_tables_64(pos)

    qkv = _nsa_proj(xb, w_in[:, :c1].astype(BF16), cos128, sin128, S)

    pad_kr = jnp.zeros((D, LANES - W_MLA_KR), w_in.dtype)
    pad_g = jnp.zeros((D, LANES - W_NSA_GATE), w_in.dtype)
    w_m = jnp.concatenate([w_in[:, c2:c4], w_in[:, c4:c5], pad_kr, w_in[:, c1:c2], pad_g], axis=1).astype(BF16)
    H = MLA_HEADS
    wq = w_uq.reshape(MLA_Q_RANK, H, MLA_NOPE + MLA_ROPE)
    wq = jnp.concatenate([wq, jnp.zeros((MLA_Q_RANK, H, 2 * LANES - MLA_NOPE - MLA_ROPE), wq.dtype)], axis=-1)
    wq = wq.reshape(MLA_Q_RANK, H * 2 * LANES).astype(BF16)
    wukv = jnp.concatenate([w_uk, w_uv], axis=1).astype(BF16)
    q_m, kv_m, kr_m, gates = _mla_in(xb, w_m, q_norm.reshape(1, -1), kv_norm.reshape(1, -1), wq, wukv,
                                     cos64, slo64, shi64, S)

    n_chunks = S // CMP_STRIDE
    kvc = qkv[:, NSA_HEADS * dk:(NSA_HEADS + 2 * G) * dk]
    zc = kvc.reshape(B, n_chunks, CMP_STRIDE, 2, G, dk).transpose(0, 3, 4, 1, 2, 5)
    zc = zc.reshape(B, 2, G, n_chunks, CMP_STRIDE * dk)
    pe = jnp.stack([pe_k.reshape(-1), pe_v.reshape(-1)])
    pe = jnp.broadcast_to(pe[:, None, :], (2, 8, pe.shape[-1])).astype(BF16)
    half = CMP_STRIDE * dk
    w1 = jnp.stack([jnp.concatenate([wk1[:half], wk1[half:]], axis=1),
                    jnp.concatenate([wv1[:half], wv1[half:]], axis=1)]).astype(BF16)
    w2 = jnp.stack([wk2, wv2]).astype(BF16)
    pos_c = jnp.arange(n_chunks) * CMP_STRIDE + (CMP_BLK - 1)
    cos_c, sin_c = _rope_tables_128(pos_c)
    cmp_kv = _nsa_compress(zc, pe, w1, w2, cos_c, sin_c)

    g3 = gates[:, :W_NSA_GATE].reshape(B, S, G, NSA_HPG * 3).transpose(0, 2, 1, 3)
    g3 = jnp.pad(g3, ((0, 0), (0, 0), (0, 0), (0, LANES - NSA_HPG * 3)))

    y_nsa = _nsa_attention(qkv.reshape(B, S, -1), g3, cmp_kv)
    y_mla = _mla_attention(q_m.reshape(B, S, -1), kv_m.reshape(B, S, -1), kr_m.reshape(B, S, -1))
    return y_nsa, y_mla


def kernel(x, p, w_in, nsa_pe_k, nsa_pe_v, nsa_cmp_k_w1, nsa_cmp_k_w2, nsa_cmp_v_w1, nsa_cmp_v_w2, mla_q_norm, mla_w_uq, mla_kv_norm, mla_w_uk, mla_w_uv, w_branch_nsa, w_branch_mla, w_out, ln1_g, ln1_b, router_w, router_b, exp_w_gate, exp_w_up, exp_w_down, sh_w_gate, sh_w_up, sh_w_down, ln2_g, ln2_b, ple_w_proj, ple_w_gate):
    B, S, D = x.shape
    T = B * S
    c5 = D_IN - W_MERGE
    for i in range(DEPTH):
        x2 = x.reshape(T, D)
        xb = x2.astype(BF16)
        y_nsa, y_mla = _attention_branches(
            xb, S, w_in[i], nsa_pe_k[i], nsa_pe_v[i], nsa_cmp_k_w1[i], nsa_cmp_k_w2[i],
            nsa_cmp_v_w1[i], nsa_cmp_v_w2[i], mla_q_norm[i], mla_w_uq[i], mla_kv_norm[i],
            mla_w_uk[i], mla_w_uv[i])

        merged = _merge(xb, y_nsa.reshape(T, -1), y_mla.reshape(T, -1), w_in[i][:, c5:].astype(BF16),
                        w_branch_nsa[i].astype(BF16), w_branch_mla[i].astype(BF16))
        x1, x1b, eid_t, wts_t = _out_ln1(merged, x2, w_out[i].astype(BF16), ln1_g[i].reshape(1, D),
                                         ln1_b[i].reshape(1, D), router_w[i].T.astype(BF16),
                                         router_b[i].reshape(N_EXPERTS, 1).astype(F32))

        yk = _moe_routed(x1b, eid_t[:TOP_K].T, _cast_bf16(exp_w_gate[i]), _cast_bf16(exp_w_up[i]),
                         _cast_bf16(exp_w_down[i]))
        x2n, x2b = _ffn_ln2(x1, yk, wts_t.T, sh_w_gate[i][None].astype(BF16), sh_w_up[i][None].astype(BF16),
                            sh_w_down[i][None].astype(BF16), ln2_g[i].reshape(1, D), ln2_b[i].reshape(1, D))
        out = _ple(x2n, x2b, p[i].reshape(T, PLE_DIM), ple_w_gate[i].astype(BF16), ple_w_proj[i].astype(BF16))
        x = out.reshape(B, S, D)
    return x
```

```python
import functools
import math

import jax
import jax.numpy as jnp
from jax import lax
from jax.experimental import pallas as pl
from jax.experimental.pallas import tpu as pltpu

D_MODEL = 2048
BATCH = 8
SEQ = 4096
DEPTH = 1

PLE_DIM = 256
ROPE_THETA = 10000.0
LN_EPS = 1e-5
RMS_EPS = 1e-6
NEG_INF = -1e30

NSA_HEADS = 8
NSA_GROUPS = 2
NSA_HPG = NSA_HEADS // NSA_GROUPS
NSA_DK = 128
NSA_DV = 128
CMP_BLK = 32
CMP_STRIDE = 16
CMP_HID = 256
SLC_BLK = 64
N_SEL = 16
WIN = 512
FORCE_SCORE = 1e4

MLA_HEADS = 8
MLA_Q_RANK = 768
MLA_KV_RANK = 512
MLA_NOPE = 128
MLA_ROPE = 64
MLA_DV = 128

N_EXPERTS = 64
TOP_K = 6
N_EXPERT_GROUPS = 8
TOPK_GROUPS = 4
D_EXPERT = 1408
ROUTED_SCALE = 2.5

W_NSA_Q = NSA_HEADS * NSA_DK
W_NSA_KV = 3 * NSA_GROUPS * (NSA_DK + NSA_DV)
W_NSA_GATE = 3 * NSA_HEADS
W_MLA_CQ = MLA_Q_RANK
W_MLA_CKV = MLA_KV_RANK
W_MLA_KR = MLA_ROPE
W_MERGE = 2 * D_MODEL
D_IN = W_NSA_Q + W_NSA_KV + W_NSA_GATE + W_MLA_CQ + W_MLA_CKV + W_MLA_KR + W_MERGE

DN_ALPHA = (2.0 * DEPTH) ** 0.25
DN_BETA = (8.0 * DEPTH) ** -0.25

V7X_VMEM_LIMIT_BYTES = 48 * 1024 * 1024
LANES = 128
MXU_WIDTH_V7X = 256
PROJ_TM = 512

BF16 = jnp.bfloat16
F32 = jnp.float32


def _params(sem):
    return pltpu.CompilerParams(dimension_semantics=sem, vmem_limit_bytes=V7X_VMEM_LIMIT_BYTES)


def _resident(shape):
    return pl.BlockSpec(shape, lambda *_: (0,) * len(shape), pipeline_mode=pl.Buffered(1))


def _rope_tables_128(pos):
    half = LANES // 2
    inv = jnp.power(ROPE_THETA, -jnp.arange(half, dtype=F32) * 2.0 / LANES)
    ang = pos.astype(F32)[:, None] * inv[None, :]
    cos, sin = jnp.cos(ang), jnp.sin(ang)
    return jnp.concatenate([cos, cos], -1), jnp.concatenate([-sin, sin], -1)


def _rope_tables_64(pos):
    half = MLA_ROPE // 2
    inv = jnp.power(ROPE_THETA, -jnp.arange(half, dtype=F32) * 2.0 / MLA_ROPE)
    ang = pos.astype(F32)[:, None] * inv[None, :]
    cos, sin = jnp.cos(ang), jnp.sin(ang)
    zero = jnp.zeros_like(sin)
    return (jnp.concatenate([cos, cos, cos, cos], -1),
            jnp.concatenate([-sin, zero, -sin, zero], -1),
            jnp.concatenate([zero, sin, zero, sin], -1))


def _rope128(z, cos, sin):
    return z * cos + pltpu.roll(z, LANES // 2, 1) * sin


def _rope64(z, cos, sin_lo, sin_hi):
    return z * cos + pltpu.roll(z, LANES - MLA_ROPE // 2, 1) * sin_lo + pltpu.roll(z, MLA_ROPE // 2, 1) * sin_hi


def _nsa_proj_kernel(x_ref, w_ref, cos_ref, sin_ref, o_ref, *, rope_heads, n_q_heads, scale):
    x = x_ref[...]
    cos, sin = cos_ref[...], sin_ref[...]
    n_heads = w_ref.shape[1] // LANES
    for t in range(n_heads // 2):
        c0 = t * MXU_WIDTH_V7X
        z = jnp.dot(x, w_ref[:, c0:c0 + MXU_WIDTH_V7X], preferred_element_type=F32)
        for hh in range(2):
            head = 2 * t + hh
            zh = z[:, hh * LANES:(hh + 1) * LANES]
            if head in rope_heads:
                zh = _rope128(zh, cos, sin)
            if head < n_q_heads:
                zh = zh * scale
            o_ref[:, head * LANES:(head + 1) * LANES] = zh.astype(o_ref.dtype)


def _nsa_proj(xb, w, cos, sin, S):
    T, D = xb.shape
    n = w.shape[1]
    tm = min(PROJ_TM, S)
    n_pos = S // tm
    kv0 = NSA_HEADS
    per = NSA_GROUPS * 2
    rope_heads = tuple(range(NSA_HEADS)) + tuple(kv0 + per * br + g for br in (1, 2) for g in range(NSA_GROUPS))
    return pl.pallas_call(
        functools.partial(_nsa_proj_kernel, rope_heads=rope_heads, n_q_heads=NSA_HEADS, scale=NSA_DK ** -0.5),
        grid=(T // tm,),
        in_specs=[pl.BlockSpec((tm, D), lambda i: (i, 0)),
                  _resident((D, n)),
                  pl.BlockSpec((tm, LANES), lambda i: (i % n_pos, 0)),
                  pl.BlockSpec((tm, LANES), lambda i: (i % n_pos, 0))],
        out_specs=pl.BlockSpec((tm, n), lambda i: (i, 0)),
        out_shape=jax.ShapeDtypeStruct((T, n), BF16),
        compiler_params=_params(("parallel",)),
        name="nsa_proj",
    )(xb, w, cos, sin)


def _rmsnorm_rows(z, g):
    return z * lax.rsqrt(jnp.mean(jnp.square(z), -1, keepdims=True) + RMS_EPS) * g


def _mla_in_kernel(x_ref, w_ref, gq_ref, gkv_ref, wuq_ref, wukv_ref, cos_ref, slo_ref, shi_ref,
                   q_ref, kv_ref, kr_ref, gate_ref, *, scale):
    x = x_ref[...]
    cos, slo, shi = cos_ref[...], slo_ref[...], shi_ref[...]
    c1 = MLA_Q_RANK
    c2 = c1 + MLA_KV_RANK
    cq = jnp.dot(x, w_ref[:, :c1], preferred_element_type=F32)
    cqn = _rmsnorm_rows(cq, gq_ref[...]).astype(BF16)
    ckv = jnp.dot(x, w_ref[:, c1:c2], preferred_element_type=F32)
    ckvn = _rmsnorm_rows(ckv, gkv_ref[...]).astype(BF16)
    tail = jnp.dot(x, w_ref[:, c2:c2 + 2 * LANES], preferred_element_type=F32)
    kr_ref[...] = _rope64(tail[:, :LANES], cos, slo, shi).astype(kr_ref.dtype)
    gate_ref[...] = jax.nn.sigmoid(tail[:, LANES:])
    for h in range(MLA_HEADS):
        c0 = h * MXU_WIDTH_V7X
        z = jnp.dot(cqn, wuq_ref[:, c0:c0 + MXU_WIDTH_V7X], preferred_element_type=F32)
        q_ref[:, c0:c0 + LANES] = (z[:, :LANES] * scale).astype(q_ref.dtype)
        q_ref[:, c0 + LANES:c0 + 2 * LANES] = (_rope64(z[:, LANES:], cos, slo, shi) * scale).astype(q_ref.dtype)
    n_kv = wukv_ref.shape[1]
    for c0 in range(0, n_kv, MXU_WIDTH_V7X):
        kv_ref[:, c0:c0 + MXU_WIDTH_V7X] = jnp.dot(
            ckvn, wukv_ref[:, c0:c0 + MXU_WIDTH_V7X], preferred_element_type=F32).astype(kv_ref.dtype)


def _mla_in(xb, w, gq, gkv, wuq, wukv, cos, slo, shi, S):
    T, D = xb.shape
    tm = min(PROJ_TM, S)
    n_pos = S // tm
    tbl = pl.BlockSpec((tm, LANES), lambda i: (i % n_pos, 0))
    nq, nkv = wuq.shape[1], wukv.shape[1]
    return pl.pallas_call(
        functools.partial(_mla_in_kernel, scale=(MLA_NOPE + MLA_ROPE) ** -0.5),
        grid=(T // tm,),
        in_specs=[pl.BlockSpec((tm, D), lambda i: (i, 0)),
                  _resident(w.shape), _resident(gq.shape), _resident(gkv.shape),
                  _resident(wuq.shape), _resident(wukv.shape), tbl, tbl, tbl],
        out_specs=[pl.BlockSpec((tm, nq), lambda i: (i, 0)),
                   pl.BlockSpec((tm, nkv), lambda i: (i, 0)),
                   pl.BlockSpec((tm, LANES), lambda i: (i, 0)),
                   pl.BlockSpec((tm, LANES), lambda i: (i, 0))],
        out_shape=[jax.ShapeDtypeStruct((T, nq), BF16), jax.ShapeDtypeStruct((T, nkv), BF16),
                   jax.ShapeDtypeStruct((T, LANES), BF16), jax.ShapeDtypeStruct((T, LANES), F32)],
        compiler_params=_params(("parallel",)),
        name="mla_in",
    )(xb, w, gq, gkv, wuq, wukv, cos, slo, shi)


MASK_BIG = 16384.0


def _softmax_reset(m_s, l_s, acc_s):
    m_s[...] = jnp.full(m_s.shape, NEG_INF, F32)
    l_s[...] = jnp.zeros(l_s.shape, F32)
    acc_s[...] = jnp.zeros(acc_s.shape, F32)


def _online_softmax_step(s, v, m_s, l_s, acc_s):
    m_prev = m_s[...]
    m_new = jnp.maximum(m_prev, jnp.max(s, axis=-1, keepdims=True))
    alpha = jnp.exp(m_prev - m_new)
    p = jnp.exp(s - m_new)
    l_s[...] = alpha * l_s[...] + jnp.sum(p, axis=-1, keepdims=True)
    acc_s[...] = alpha * acc_s[...] + jnp.dot(p.astype(BF16), v, preferred_element_type=F32)
    m_s[...] = m_new


def _online_softmax_step_t(st, vt, m_s, l_s, acc_s):
    m_prev = m_s[...]
    m_new = jnp.maximum(m_prev, jnp.max(st, axis=0, keepdims=True))
    alpha = jnp.exp(m_prev - m_new)
    p = jnp.exp(st - m_new)
    l_s[...] = alpha * l_s[...] + jnp.sum(p, axis=0, keepdims=True)
    acc_s[...] = alpha * acc_s[...] + jnp.dot(vt, p.astype(BF16), preferred_element_type=F32)
    m_s[...] = m_new


def _dot_nt(a, b):
    return lax.dot_general(a, b, (((1,), (1,)), ((), ())), preferred_element_type=F32)


MLA_TQ = 512
MLA_TK = 512


MLA_HPS = 2


def _mla_attn_kernel(qt_ref, kn_ref, kr_ref, vt_ref, o_ref, m_s, l_s, acc_s, *, tq, tk, hps):
    q0 = pl.program_id(2) * tq
    qts = [qt_ref[0, hh * 2 * LANES:(hh + 1) * 2 * LANES, :] for hh in range(hps)]
    _softmax_reset(m_s, l_s, acc_s)

    def tile(j, causal):
        k0 = pl.multiple_of(j * tk, tk)
        kr = kr_ref[0, pl.ds(k0, tk), :]
        for hh in range(hps):
            hs = slice(hh * LANES, (hh + 1) * LANES)
            ka = jnp.concatenate([kn_ref[0, pl.ds(k0, tk), hs], kr], axis=1)
            st = jnp.dot(ka, qts[hh], preferred_element_type=F32)
            if causal:
                kpos = k0 + lax.broadcasted_iota(jnp.int32, (tk, tq), 0)
                qpos = q0 + lax.broadcasted_iota(jnp.int32, (tk, tq), 1)
                st = jnp.where(kpos <= qpos, st, -MASK_BIG)
            _online_softmax_step_t(st, vt_ref[0, hs, pl.ds(k0, tk)], m_s.at[hh], l_s.at[hh], acc_s.at[hh])

    j_diag = q0 // tk

    def body(j, carry):
        tile(j, False)
        return carry

    lax.fori_loop(0, j_diag, body, 0)
    tile(j_diag, True)
    for hh in range(hps):
        o_ref[0, :, hh * LANES:(hh + 1) * LANES] = jnp.transpose(acc_s[hh] / l_s[hh]).astype(o_ref.dtype)


def _mla_attention(qt, kv, kr, vt):
    B, _, S = qt.shape
    H = MLA_HEADS
    hps = MLA_HPS
    tq = tk = min(MLA_TQ, S)
    return pl.pallas_call(
        functools.partial(_mla_attn_kernel, tq=tq, tk=tk, hps=hps),
        grid=(B, H // hps, S // tq),
        in_specs=[pl.BlockSpec((1, hps * 2 * LANES, tq), lambda b, h, i: (b, h, i)),
                  pl.BlockSpec((1, S, hps * LANES), lambda b, h, i: (b, 0, h)),
                  pl.BlockSpec((1, S, LANES), lambda b, h, i: (b, 0, 0)),
                  pl.BlockSpec((1, hps * LANES, S), lambda b, h, i: (b, h, 0))],
        out_specs=pl.BlockSpec((1, tq, hps * LANES), lambda b, h, i: (b, i, h)),
        out_shape=jax.ShapeDtypeStruct((B, S, H * MLA_DV), BF16),
        scratch_shapes=[pltpu.VMEM((hps, 1, tq), F32), pltpu.VMEM((hps, 1, tq), F32),
                        pltpu.VMEM((hps, MLA_DV, tq), F32)],
        compiler_params=_params(("parallel", "parallel", "arbitrary")),
        name="mla_attn",
    )(qt, kv, kr, vt)


def _gelu_tanh(x):
    return 0.5 * x * (1.0 + jnp.tanh(math.sqrt(2.0 / math.pi) * (x + 0.044715 * (x * x * x))))


def _compress_kernel(z_ref, pe_ref, w1_ref, w2_ref, cos_ref, sin_ref, o_ref, *, hid):
    which = pl.program_id(1)
    z = z_ref[0, 0, 0]
    n_chunks = z.shape[0]
    cd = z.shape[1]
    ab = jnp.dot(z, w1_ref[0], preferred_element_type=F32)
    a_part = ab[:, :hid]
    b_next = pltpu.roll(ab[:, hid:], n_chunks - 1, 0)
    pe_a = jnp.dot(pe_ref[0, :, :cd], w1_ref[0, :, :hid], preferred_element_type=F32)
    pe_b = jnp.dot(pe_ref[0, :, cd:], w1_ref[0, :, hid:], preferred_element_type=F32)
    h = _gelu_tanh(a_part + b_next + pe_a[0:1] + pe_b[0:1])
    out = jnp.dot(h.astype(BF16), w2_ref[0], preferred_element_type=F32)
    roped = _rope128(out, cos_ref[...], sin_ref[...])
    o_ref[0, 0, 0] = jnp.where(which == 0, roped, out).astype(o_ref.dtype)


def _nsa_compress(zc, pe, w1, w2, cos_c, sin_c):
    B, _, G, n_chunks, cd = zc.shape
    hid = w2.shape[1]
    d = w2.shape[2]
    return pl.pallas_call(
        functools.partial(_compress_kernel, hid=hid),
        grid=(B, 2, G),
        in_specs=[pl.BlockSpec((1, 1, 1, n_chunks, cd), lambda b, t, g: (b, t, g, 0, 0)),
                  pl.BlockSpec((1, 8, 2 * cd), lambda b, t, g: (t, 0, 0)),
                  pl.BlockSpec((1, cd, 2 * hid), lambda b, t, g: (t, 0, 0)),
                  pl.BlockSpec((1, hid, d), lambda b, t, g: (t, 0, 0)),
                  pl.BlockSpec((n_chunks, LANES), lambda b, t, g: (0, 0)),
                  pl.BlockSpec((n_chunks, LANES), lambda b, t, g: (0, 0))],
        out_specs=pl.BlockSpec((1, 1, 1, n_chunks, d), lambda b, t, g: (b, t, g, 0, 0)),
        out_shape=jax.ShapeDtypeStruct((B, 2, G, n_chunks, d), BF16),
        compiler_params=_params(("parallel", "parallel", "parallel")),
        name="nsa_compress",
    )(zc, pe, w1, w2, cos_c, sin_c)


NSA_TQ = 256
NSA_TK = 512


def _nsa_kernel(qt_ref, g_ref, kc_ref, vct_ref, ks_ref, vst_ref, kw_ref, vwt_ref, o_ref,
                m_s, l_s, acc_s, out_s, *, tq, tk, hpg, dk, n_cmp, n_slc):
    q0 = pl.program_id(2) * tq
    cols = hpg * tq
    i32 = jnp.int32
    qt = jnp.concatenate([qt_ref[0, h * dk:(h + 1) * dk, :] for h in range(hpg)], axis=1)
    gates = g_ref[0, 0]

    def gate_row(branch):
        return jnp.concatenate([gates[3 * h + branch:3 * h + branch + 1, :] for h in range(hpg)], axis=1)

    def qpos_of(n_rows):
        return q0 + lax.rem(lax.broadcasted_iota(i32, (n_rows, cols), 1), tq)

    n_pad = kc_ref.shape[3]
    sc = jnp.dot(kc_ref[0, 0, 0], qt, preferred_element_type=F32)
    n_io = lax.broadcasted_iota(i32, (n_pad, cols), 0)
    vis = (n_io * CMP_STRIDE + (CMP_BLK - 1) <= qpos_of(n_pad)) & (n_io < n_cmp)
    sc = jnp.where(vis, sc, NEG_INF)
    e = jnp.where(vis, jnp.exp(sc - jnp.max(sc, axis=0, keepdims=True)), 0.0)
    p = e / jnp.maximum(jnp.sum(e, axis=0, keepdims=True), 1e-30)
    o_cmp = jnp.dot(vct_ref[0, 0], p.astype(BF16), preferred_element_type=F32)
    out_s[...] = gate_row(0) * o_cmp

    imp = p[:, 0:tq]
    for h in range(1, hpg):
        imp = imp + p[:, h * tq:(h + 1) * tq]
    ratio = SLC_BLK // CMP_STRIDE
    span = CMP_BLK // CMP_STRIDE
    j_p = lax.broadcasted_iota(i32, (n_slc, n_pad), 0)
    c_p = lax.broadcasted_iota(i32, (n_slc, n_pad), 1)
    lo_c = ratio * j_p - (span - 1)
    pool = jnp.where((c_p >= lo_c) & (c_p < lo_c + ratio + span - 1), 1.0, 0.0).astype(BF16)
    hi = imp.astype(BF16)
    r1 = imp - hi.astype(F32)
    mid = r1.astype(BF16)
    lo = (r1 - mid.astype(F32)).astype(BF16)
    p_slc = (jnp.dot(pool, hi, preferred_element_type=F32) + jnp.dot(pool, mid, preferred_element_type=F32)
             + jnp.dot(pool, lo, preferred_element_type=F32))

    j_io = lax.broadcasted_iota(i32, (n_slc, tq), 0)
    qpos_l = q0 + lax.broadcasted_iota(i32, (n_slc, tq), 1)
    cur = qpos_l // SLC_BLK
    forced = (j_io == 0) | (j_io == cur) | (j_io == cur - 1)
    valid = j_io * SLC_BLK <= qpos_l
    score = jnp.where(forced, FORCE_SCORE, jnp.where(valid, p_slc, -FORCE_SCORE))
    bias = jnp.where(_rank_before(score) < N_SEL, 0.0, -MASK_BIG)
    if n_slc < LANES:
        bias = jnp.concatenate([bias, jnp.full((LANES - n_slc, tq), -MASK_BIG, F32)], axis=0)
    bias = bias.astype(BF16)
    qa = jnp.concatenate([qt, jnp.concatenate([bias] * hpg, axis=1)], axis=0)

    qpos_k = qpos_of(tk)
    k_io = lax.broadcasted_iota(i32, (tk, cols), 0)

    _softmax_reset(m_s, l_s, acc_s)

    def slc_tile(j, causal):
        k0 = pl.multiple_of(j * tk, tk)
        kblk = (k0 + lax.broadcasted_iota(i32, (tk, LANES), 0)) // SLC_BLK
        onehot = jnp.where(kblk == lax.broadcasted_iota(i32, (tk, LANES), 1), 1.0, 0.0).astype(BF16)
        ka = jnp.concatenate([ks_ref[0, pl.ds(k0, tk), :], onehot], axis=1)
        s = jnp.dot(ka, qa, preferred_element_type=F32)
        if causal:
            s = jnp.where(k0 + k_io <= qpos_k, s, -MASK_BIG)
        _online_softmax_step_t(s, vst_ref[0, :, pl.ds(k0, tk)], m_s, l_s, acc_s)

    j_diag = q0 // tk

    def body(j, carry):
        slc_tile(j, False)
        return carry

    lax.fori_loop(0, j_diag, body, 0)
    slc_tile(j_diag, True)
    out_s[...] += gate_row(1) * (acc_s[...] / l_s[...])

    _softmax_reset(m_s, l_s, acc_s)

    def win_tile(j):
        k0 = pl.multiple_of(j * tk, tk)
        s = jnp.dot(kw_ref[0, pl.ds(k0, tk), :], qt, preferred_element_type=F32)
        kpos = k0 + k_io
        s = jnp.where((kpos <= qpos_k) & (kpos > qpos_k - WIN), s, -MASK_BIG)
        _online_softmax_step_t(s, vwt_ref[0, :, pl.ds(k0, tk)], m_s, l_s, acc_s)

    @pl.when(j_diag >= 1)
    def _():
        win_tile(j_diag - 1)

    win_tile(j_diag)
    out = out_s[...] + gate_row(2) * (acc_s[...] / l_s[...])
    for h in range(hpg):
        o_ref[0, :, h * dk:(h + 1) * dk] = jnp.transpose(out[:, h * tq:(h + 1) * tq]).astype(o_ref.dtype)


def _nsa_attention(qkv, qt, gates_t, cmp_kv, vct, vt):
    B, S, _ = qkv.shape
    G, hpg, dk = NSA_GROUPS, NSA_HPG, NSA_DK
    n_pad = cmp_kv.shape[3]
    n_cmp = (S - CMP_BLK) // CMP_STRIDE + 1
    tq, tk = min(NSA_TQ, S), min(NSA_TK, S)
    assert tk == WIN and tq <= tk
    cols = hpg * tq
    kv0 = NSA_HEADS

    def k_spec(branch):
        base = kv0 + 2 * G * branch
        return pl.BlockSpec((1, S, dk), lambda b, g, i: (b, 0, base + g))

    def vt_spec(branch):
        return pl.BlockSpec((1, dk, S), lambda b, g, i: (b, (branch - 1) * G + g, 0))

    return pl.pallas_call(
        functools.partial(_nsa_kernel, tq=tq, tk=tk, hpg=hpg, dk=dk, n_cmp=n_cmp, n_slc=S // SLC_BLK),
        grid=(B, G, S // tq),
        in_specs=[pl.BlockSpec((1, hpg * dk, tq), lambda b, g, i: (b, g, i)),
                  pl.BlockSpec((1, 1, LANES, tq), lambda b, g, i: (b, g, 0, i)),
                  pl.BlockSpec((1, 1, 1, n_pad, dk), lambda b, g, i: (b, 0, g, 0, 0)),
                  pl.BlockSpec((1, 1, dk, n_pad), lambda b, g, i: (b, g, 0, 0)),
                  k_spec(1), vt_spec(1), k_spec(2), vt_spec(2)],
        out_specs=pl.BlockSpec((1, tq, hpg * dk), lambda b, g, i: (b, i, g)),
        out_shape=jax.ShapeDtypeStruct((B, S, NSA_HEADS * NSA_DV), BF16),
        scratch_shapes=[pltpu.VMEM((1, cols), F32), pltpu.VMEM((1, cols), F32),
                        pltpu.VMEM((dk, cols), F32), pltpu.VMEM((dk, cols), F32)],
        compiler_params=_params(("parallel", "parallel", "arbitrary")),
        name="nsa_attn",
    )(qt, gates_t, cmp_kv, vct, qkv, vt, qkv, vt)


MOE_TM = 256


def _hidden_chunks(f):
    chunks, c0 = [], 0
    while c0 < f:
        cs = min(MXU_WIDTH_V7X, f - c0)
        chunks.append((c0, cs))
        c0 += cs
    return tuple(chunks)


def _swiglu(x, wg_ref, wu_ref, wd_ref, chunks):
    acc = None
    for c0, cs in chunks:
        g = jnp.dot(x, wg_ref[0, :, c0:c0 + cs], preferred_element_type=F32)
        u = jnp.dot(x, wu_ref[0, :, c0:c0 + cs], preferred_element_type=F32)
        h = (g * jax.nn.sigmoid(g) * u).astype(BF16)
        y = jnp.dot(h, wd_ref[0, c0:c0 + cs, :], preferred_element_type=F32)
        acc = y if acc is None else acc + y
    return acc


def _moe_ffn_kernel(blk_e_ref, n_used_ref, x_ref, wg_ref, wu_ref, wd_ref, o_ref, *, chunks):
    del blk_e_ref

    @pl.when(pl.program_id(0) < n_used_ref[0])
    def _():
        o_ref[...] = _swiglu(x_ref[...], wg_ref, wu_ref, wd_ref, chunks).astype(o_ref.dtype)


def _cast_kernel(w_ref, o_ref):
    o_ref[...] = w_ref[...].astype(o_ref.dtype)


def _cast_bf16(w):
    E, R, C = w.shape
    return pl.pallas_call(
        _cast_kernel,
        grid=(E,),
        in_specs=[pl.BlockSpec((1, R, C), lambda e: (e, 0, 0))],
        out_specs=pl.BlockSpec((1, R, C), lambda e: (e, 0, 0)),
        out_shape=jax.ShapeDtypeStruct(w.shape, BF16),
        compiler_params=_params(("parallel",)),
        name="cast_bf16",
    )(w)


def _moe_routed(xb, eidx, w_gate, w_up, w_down):
    T, D = xb.shape
    K = eidx.shape[1]
    E, _, F = w_gate.shape
    tm = MOE_TM
    n = T * K
    n_blk = -(-n // tm) + E
    i32 = jnp.int32

    onehot = jnp.any(eidx[..., None] == jnp.arange(E, dtype=eidx.dtype), axis=1).astype(i32)
    cum = jnp.cumsum(onehot, axis=0)
    counts = cum[-1]
    rank = jnp.take_along_axis(cum, eidx, axis=1) - 1
    padded = (counts + tm - 1) // tm * tm
    pad_end = jnp.cumsum(padded)
    pad_start = pad_end - padded
    dest = (pad_start[eidx] + rank).astype(i32)
    flat_t = jnp.arange(n, dtype=i32) // K
    tok = jnp.zeros((n_blk * tm,), i32).at[dest.reshape(n)].set(flat_t)
    n_used = (pad_end[-1] // tm).astype(i32).reshape(1)
    blk_e = jnp.minimum(jnp.searchsorted(pad_end, jnp.arange(n_blk, dtype=i32) * tm, side='right'),
                        E - 1).astype(i32)

    xs = jnp.take(xb, tok, axis=0)

    def row_map(i, be, nu):
        return (jnp.minimum(i, nu[0] - 1), 0)

    y = pl.pallas_call(
        functools.partial(_moe_ffn_kernel, chunks=_hidden_chunks(F)),
        grid_spec=pltpu.PrefetchScalarGridSpec(
            num_scalar_prefetch=2,
            grid=(n_blk,),
            in_specs=[pl.BlockSpec((tm, D), row_map),
                      pl.BlockSpec((1, D, F), lambda i, be, nu: (be[i], 0, 0)),
                      pl.BlockSpec((1, D, F), lambda i, be, nu: (be[i], 0, 0)),
                      pl.BlockSpec((1, F, D), lambda i, be, nu: (be[i], 0, 0))],
            out_specs=pl.BlockSpec((tm, D), row_map)),
        out_shape=jax.ShapeDtypeStruct((n_blk * tm, D), BF16),
        compiler_params=_params(("arbitrary",)),
        name="moe_ffn",
    )(blk_e, n_used, xs, w_gate, w_up, w_down)

    return jnp.take(y, dest.T.reshape(n), axis=0).reshape(K, T, D)


def _layernorm_rows(z, g, b):
    mu = jnp.mean(z, -1, keepdims=True)
    zc = z - mu
    var = jnp.mean(jnp.square(zc), -1, keepdims=True)
    return zc * lax.rsqrt(var + LN_EPS) * g + b


def _merge_kernel(x_ref, yn_ref, ym_ref, wg_ref, wbn_ref, wbm_ref, o_ref):
    x, yn, ym = x_ref[...], yn_ref[...], ym_ref[...]
    d = o_ref.shape[1]
    for c0 in range(0, d, MXU_WIDTH_V7X):
        cs = slice(c0, c0 + MXU_WIDTH_V7X)
        gn = jax.nn.sigmoid(jnp.dot(x, wg_ref[:, cs], preferred_element_type=F32))
        gm = jax.nn.sigmoid(jnp.dot(x, wg_ref[:, d + c0:d + c0 + MXU_WIDTH_V7X], preferred_element_type=F32))
        bn = jnp.dot(yn, wbn_ref[:, cs], preferred_element_type=F32)
        bm = jnp.dot(ym, wbm_ref[:, cs], preferred_element_type=F32)
        o_ref[:, cs] = (gn * bn + gm * bm).astype(o_ref.dtype)


def _merge(xb, y_nsa, y_mla, w_g, wb_nsa, wb_mla):
    T, D = xb.shape
    tm = PROJ_TM
    row = lambda a: pl.BlockSpec((tm, a.shape[1]), lambda i: (i, 0))
    return pl.pallas_call(
        _merge_kernel,
        grid=(T // tm,),
        in_specs=[row(xb), row(y_nsa), row(y_mla), _resident(w_g.shape), _resident(wb_nsa.shape),
                  _resident(wb_mla.shape)],
        out_specs=pl.BlockSpec((tm, D), lambda i: (i, 0)),
        out_shape=jax.ShapeDtypeStruct((T, D), BF16),
        compiler_params=_params(("parallel",)),
        name="merge",
    )(xb, y_nsa, y_mla, w_g, wb_nsa, wb_mla)


def _rank_before(v):
    n = v.shape[0]
    j_io = lax.broadcasted_iota(jnp.int32, v.shape, 0)
    cnt = jnp.zeros(v.shape, jnp.int32)
    for i in range(n):
        row = v[i:i + 1, :]
        cnt = cnt + ((row > v) | ((row == v) & (j_io > i))).astype(jnp.int32)
    return cnt


def _route_tile(logits_t, rb):
    E, tm = logits_t.shape
    per = E // N_EXPERT_GROUPS
    scores = jax.nn.sigmoid(logits_t)
    biased = scores + rb
    grp = []
    for g in range(N_EXPERT_GROUPS):
        blk = biased[g * per:(g + 1) * per, :]
        m1 = jnp.max(blk, axis=0, keepdims=True)
        eq = blk == m1
        n_eq = jnp.sum(eq.astype(F32), axis=0, keepdims=True)
        m2 = jnp.max(jnp.where(eq, NEG_INF, blk), axis=0, keepdims=True)
        grp.append(m1 + jnp.where(n_eq >= 2.0, m1, m2))
    gsel = _rank_before(jnp.concatenate(grp, axis=0)) < TOPK_GROUPS
    masked = jnp.concatenate(
        [jnp.where(gsel[g:g + 1, :], biased[g * per:(g + 1) * per, :], NEG_INF) for g in range(N_EXPERT_GROUPS)],
        axis=0)
    sel = _rank_before(masked) < TOP_K
    w_sel = jnp.where(sel, scores, 0.0)
    w_sel = w_sel / jnp.sum(w_sel, axis=0, keepdims=True) * ROUTED_SCALE
    tri = jnp.where(lax.broadcasted_iota(jnp.int32, (E, E), 1) <= lax.broadcasted_iota(jnp.int32, (E, E), 0),
                    1.0, 0.0).astype(BF16)
    slot = jnp.dot(tri, jnp.where(sel, 1.0, 0.0).astype(BF16), preferred_element_type=F32)
    e_io = lax.broadcasted_iota(jnp.int32, (E, tm), 0)
    ids, wts = [], []
    for k in range(TOP_K):
        mk = sel & (slot == float(k + 1))
        ids.append(jnp.sum(jnp.where(mk, e_io, 0), axis=0, keepdims=True))
        wts.append(jnp.sum(jnp.where(mk, w_sel, 0.0), axis=0, keepdims=True))
    pad = 8 - TOP_K
    ids.append(jnp.zeros((pad, tm), jnp.int32))
    wts.append(jnp.zeros((pad, tm), F32))
    return jnp.concatenate(ids, axis=0), jnp.concatenate(wts, axis=0)


def _out_ln1_kernel(m_ref, x_ref, wo_ref, g_ref, b_ref, wrt_ref, rb_ref, x1_ref, x1b_ref, eid_ref, wts_ref):
    m = m_ref[...]
    d = x1_ref.shape[1]
    for c0 in range(0, d, MXU_WIDTH_V7X):
        cs = slice(c0, c0 + MXU_WIDTH_V7X)
        x1_ref[:, cs] = DN_ALPHA * x_ref[:, cs] + jnp.dot(m, wo_ref[:, cs], preferred_element_type=F32)
    x1 = _layernorm_rows(x1_ref[...], g_ref[...], b_ref[...])
    x1_ref[...] = x1
    x1b = x1.astype(BF16)
    x1b_ref[...] = x1b
    logits_t = _dot_nt(wrt_ref[...], x1b)
    eid_ref[...], wts_ref[...] = _route_tile(logits_t, rb_ref[...])


def _out_ln1(merged, x2, w_out, g, b, w_router_t, router_b):
    T, D = x2.shape
    tm = MOE_TM
    row = lambda w: pl.BlockSpec((tm, w), lambda i: (i, 0))
    col = pl.BlockSpec((8, tm), lambda i: (0, i))
    return pl.pallas_call(
        _out_ln1_kernel,
        grid=(T // tm,),
        in_specs=[row(D), row(D), _resident(w_out.shape), _resident(g.shape), _resident(b.shape),
                  _resident(w_router_t.shape), _resident(router_b.shape)],
        out_specs=[row(D), row(D), col, col],
        out_shape=[jax.ShapeDtypeStruct((T, D), F32), jax.ShapeDtypeStruct((T, D), BF16),
                   jax.ShapeDtypeStruct((8, T), jnp.int32), jax.ShapeDtypeStruct((8, T), F32)],
        compiler_params=_params(("parallel",)),
        name="out_ln1",
    )(merged, x2, w_out, g, b, w_router_t, router_b)


def _ffn_ln2_kernel(x1_ref, yk_ref, w_ref, sg_ref, su_ref, sd_ref, g_ref, b_ref, x2_ref, x2b_ref, *, chunks):
    x1 = x1_ref[...]
    acc = _swiglu(x1.astype(BF16), sg_ref, su_ref, sd_ref, chunks)
    w = w_ref[...]
    for k in range(yk_ref.shape[0]):
        acc = acc + yk_ref[k].astype(F32) * w[:, k:k + 1]
    x2 = _layernorm_rows(DN_ALPHA * x1 + acc, g_ref[...], b_ref[...])
    x2_ref[...] = x2
    x2b_ref[...] = x2.astype(BF16)


FFN_LN2_VMEM_LIMIT_BYTES = 56 * 1024 * 1024


def _ffn_ln2(x1, yk, wts, sh_gate, sh_up, sh_down, g, b):
    T, D = x1.shape
    K = yk.shape[0]
    tm = MOE_TM
    F = sh_gate.shape[-1]
    row = lambda w: pl.BlockSpec((tm, w), lambda i: (i, 0))
    return pl.pallas_call(
        functools.partial(_ffn_ln2_kernel, chunks=_hidden_chunks(F)),
        grid=(T // tm,),
        in_specs=[row(D), pl.BlockSpec((K, tm, D), lambda i: (0, i, 0)), row(wts.shape[1]),
                  _resident(sh_gate.shape), _resident(sh_up.shape), _resident(sh_down.shape),
                  _resident(g.shape), _resident(b.shape)],
        out_specs=[row(D), row(D)],
        out_shape=[jax.ShapeDtypeStruct((T, D), F32), jax.ShapeDtypeStruct((T, D), BF16)],
        compiler_params=pltpu.CompilerParams(dimension_semantics=("parallel",),
                                             vmem_limit_bytes=FFN_LN2_VMEM_LIMIT_BYTES),
        name="ffn_ln2",
    )(x1, yk, wts, sh_gate, sh_up, sh_down, g, b)


def _ple_kernel(x2_ref, x2b_ref, p_ref, wpg_ref, wpp_ref, o_ref):
    x2b = x2b_ref[...]
    pb = p_ref[...].astype(BF16)
    d = o_ref.shape[1]
    for c0 in range(0, d, MXU_WIDTH_V7X):
        cs = slice(c0, c0 + MXU_WIDTH_V7X)
        gate = jax.nn.sigmoid(jnp.dot(x2b, wpg_ref[:, cs], preferred_element_type=F32))
        pp = jnp.dot(pb, wpp_ref[:, cs], preferred_element_type=F32)
        o_ref[:, cs] = x2_ref[:, cs] + gate * pp


def _ple(x2, x2b, p2, w_pg, w_pp):
    T, D = x2.shape
    tm = PROJ_TM
    row = lambda w: pl.BlockSpec((tm, w), lambda i: (i, 0))
    return pl.pallas_call(
        _ple_kernel,
        grid=(T // tm,),
        in_specs=[row(D), row(D), row(p2.shape[1]), _resident(w_pg.shape), _resident(w_pp.shape)],
        out_specs=row(D),
        out_shape=jax.ShapeDtypeStruct((T, D), F32),
        compiler_params=_params(("parallel",)),
        name="ple",
    )(x2, x2b, p2, w_pg, w_pp)


def _attention_branches(xb, S, w_in, pe_k, pe_v, wk1, wk2, wv1, wv2, q_norm, w_uq, kv_norm, w_uk, w_uv):
    T, D = xb.shape
    B = T // S
    c0 = W_NSA_Q
    c1 = c0 + W_NSA_KV
    c2 = c1 + W_NSA_GATE
    c3 = c2 + W_MLA_CQ
    c4 = c3 + W_MLA_CKV
    c5 = c4 + W_MLA_KR
    G, dk = NSA_GROUPS, NSA_DK
    pos = jnp.arange(S)
    cos128, sin128 = _rope_tables_128(pos)
    cos64, slo64, shi64 = _rope_tables_64(pos)

    qkv = _nsa_proj(xb, w_in[:, :c1].astype(BF16), cos128, sin128, S)

    pad_kr = jnp.zeros((D, LANES - W_MLA_KR), w_in.dtype)
    pad_g = jnp.zeros((D, LANES - W_NSA_GATE), w_in.dtype)
    w_m = jnp.concatenate([w_in[:, c2:c4], w_in[:, c4:c5], pad_kr, w_in[:, c1:c2], pad_g], axis=1).astype(BF16)
    H = MLA_HEADS
    wq = w_uq.reshape(MLA_Q_RANK, H, MLA_NOPE + MLA_ROPE)
    wq = jnp.concatenate([wq, jnp.zeros((MLA_Q_RANK, H, 2 * LANES - MLA_NOPE - MLA_ROPE), wq.dtype)], axis=-1)
    wq = wq.reshape(MLA_Q_RANK, H * 2 * LANES).astype(BF16)
    wukv = jnp.concatenate([w_uk, w_uv], axis=1).astype(BF16)
    q_m, kv_m, kr_m, gates = _mla_in(xb, w_m, q_norm.reshape(1, -1), kv_norm.reshape(1, -1), wq, wukv,
                                     cos64, slo64, shi64, S)

    n_chunks = S // CMP_STRIDE
    kvc = qkv[:, NSA_HEADS * dk:(NSA_HEADS + 2 * G) * dk]
    zc = kvc.reshape(B, n_chunks, CMP_STRIDE, 2, G, dk).transpose(0, 3, 4, 1, 2, 5)
    zc = zc.reshape(B, 2, G, n_chunks, CMP_STRIDE * dk)
    pe = jnp.stack([pe_k.reshape(-1), pe_v.reshape(-1)])
    pe = jnp.broadcast_to(pe[:, None, :], (2, 8, pe.shape[-1])).astype(BF16)
    half = CMP_STRIDE * dk
    w1 = jnp.stack([jnp.concatenate([wk1[:half], wk1[half:]], axis=1),
                    jnp.concatenate([wv1[:half], wv1[half:]], axis=1)]).astype(BF16)
    w2 = jnp.stack([wk2, wv2]).astype(BF16)
    pos_c = jnp.arange(n_chunks) * CMP_STRIDE + (CMP_BLK - 1)
    cos_c, sin_c = _rope_tables_128(pos_c)
    cmp_kv = _nsa_compress(zc, pe, w1, w2, cos_c, sin_c)

    g3 = gates[:, :W_NSA_GATE].reshape(B, S, G, NSA_HPG * 3).transpose(0, 2, 3, 1)
    g3 = jnp.pad(g3, ((0, 0), (0, 0), (0, LANES - NSA_HPG * 3), (0, 0)))

    qkv3 = qkv.reshape(B, S, -1)
    qt = jnp.swapaxes(qkv3[:, :, :NSA_HEADS * dk], 1, 2)
    v0 = NSA_HEADS + 2 * G
    v_sw = jnp.concatenate([qkv3[:, :, (v0 + G) * dk:(v0 + 2 * G) * dk],
                            qkv3[:, :, (v0 + 3 * G) * dk:(v0 + 4 * G) * dk]], axis=-1)
    vt = jnp.swapaxes(v_sw, 1, 2)
    vct = jnp.swapaxes(cmp_kv[:, 1], 2, 3)
    y_nsa = _nsa_attention(qkv3, qt, g3, cmp_kv, vct, vt)
    kv_m = kv_m.reshape(B, S, -1)
    qt_m = jnp.swapaxes(q_m.reshape(B, S, -1), 1, 2)
    vt_m = jnp.swapaxes(kv_m[:, :, MLA_HEADS * MLA_NOPE:], 1, 2)
    y_mla = _mla_attention(qt_m, kv_m, kr_m.reshape(B, S, -1), vt_m)
    return y_nsa, y_mla


def kernel(x, p, w_in, nsa_pe_k, nsa_pe_v, nsa_cmp_k_w1, nsa_cmp_k_w2, nsa_cmp_v_w1, nsa_cmp_v_w2, mla_q_norm, mla_w_uq, mla_kv_norm, mla_w_uk, mla_w_uv, w_branch_nsa, w_branch_mla, w_out, ln1_g, ln1_b, router_w, router_b, exp_w_gate, exp_w_up, exp_w_down, sh_w_gate, sh_w_up, sh_w_down, ln2_g, ln2_b, ple_w_proj, ple_w_gate):
    B, S, D = x.shape
    T = B * S
    c5 = D_IN - W_MERGE
    for i in range(DEPTH):
        x2 = x.reshape(T, D)
        xb = x2.astype(BF16)
        y_nsa, y_mla = _attention_branches(
            xb, S, w_in[i], nsa_pe_k[i], nsa_pe_v[i], nsa_cmp_k_w1[i], nsa_cmp_k_w2[i],
            nsa_cmp_v_w1[i], nsa_cmp_v_w2[i], mla_q_norm[i], mla_w_uq[i], mla_kv_norm[i],
            mla_w_uk[i], mla_w_uv[i])

        merged = _merge(xb, y_nsa.reshape(T, -1), y_mla.reshape(T, -1), w_in[i][:, c5:].astype(BF16),
                        w_branch_nsa[i].astype(BF16), w_branch_mla[i].astype(BF16))
        x1, x1b, eid_t, wts_t = _out_ln1(merged, x2, w_out[i].astype(BF16), ln1_g[i].reshape(1, D),
                                         ln1_b[i].reshape(1, D), router_w[i].T.astype(BF16),
                                         router_b[i].reshape(N_EXPERTS, 1).astype(F32))

        yk = _moe_routed(x1b, eid_t[:TOP_K].T, _cast_bf16(exp_w_gate[i]), _cast_bf16(exp_w_up[i]),
                         _cast_bf16(exp_w_down[i]))
        x2n, x2b = _ffn_ln2(x1, yk, wts_t.T, sh_w_gate[i][None].astype(BF16), sh_w_up[i][None].astype(BF16),
                            sh_w_down[i][None].astype(BF16), ln2_g[i].reshape(1, D), ln2_b[i].reshape(1, D))
        out = _ple(x2n, x2b, p[i].reshape(T, PLE_DIM), ple_w_gate[i].astype(BF16), ple_w_proj[i].astype(BF16))
        x = out.reshape(B, S, D)
    return x
```

```python
import functools
import math

import jax
import jax.numpy as jnp
from jax import lax
from jax.experimental import pallas as pl
from jax.experimental.pallas import tpu as pltpu

D_MODEL = 2048
BATCH = 8
SEQ = 4096
DEPTH = 1

PLE_DIM = 256
ROPE_THETA = 10000.0
LN_EPS = 1e-5
RMS_EPS = 1e-6
NEG_INF = -1e30

NSA_HEADS = 8
NSA_GROUPS = 2
NSA_HPG = NSA_HEADS // NSA_GROUPS
NSA_DK = 128
NSA_DV = 128
CMP_BLK = 32
CMP_STRIDE = 16
CMP_HID = 256
SLC_BLK = 64
N_SEL = 16
WIN = 512
FORCE_SCORE = 1e4

MLA_HEADS = 8
MLA_Q_RANK = 768
MLA_KV_RANK = 512
MLA_NOPE = 128
MLA_ROPE = 64
MLA_DV = 128

N_EXPERTS = 64
TOP_K = 6
N_EXPERT_GROUPS = 8
TOPK_GROUPS = 4
D_EXPERT = 1408
ROUTED_SCALE = 2.5

W_NSA_Q = NSA_HEADS * NSA_DK
W_NSA_KV = 3 * NSA_GROUPS * (NSA_DK + NSA_DV)
W_NSA_GATE = 3 * NSA_HEADS
W_MLA_CQ = MLA_Q_RANK
W_MLA_CKV = MLA_KV_RANK
W_MLA_KR = MLA_ROPE
W_MERGE = 2 * D_MODEL
D_IN = W_NSA_Q + W_NSA_KV + W_NSA_GATE + W_MLA_CQ + W_MLA_CKV + W_MLA_KR + W_MERGE

DN_ALPHA = (2.0 * DEPTH) ** 0.25
DN_BETA = (8.0 * DEPTH) ** -0.25

V7X_VMEM_LIMIT_BYTES = 48 * 1024 * 1024
LANES = 128
MXU_WIDTH_V7X = 256
PROJ_TM = 512

BF16 = jnp.bfloat16
F32 = jnp.float32


def _params(sem):
    return pltpu.CompilerParams(dimension_semantics=sem, vmem_limit_bytes=V7X_VMEM_LIMIT_BYTES)


def _resident(shape):
    return pl.BlockSpec(shape, lambda *_: (0,) * len(shape), pipeline_mode=pl.Buffered(1))


def _rope_tables_128(pos):
    half = LANES // 2
    inv = jnp.power(ROPE_THETA, -jnp.arange(half, dtype=F32) * 2.0 / LANES)
    ang = pos.astype(F32)[:, None] * inv[None, :]
    cos, sin = jnp.cos(ang), jnp.sin(ang)
    return jnp.concatenate([cos, cos], -1), jnp.concatenate([-sin, sin], -1)


def _rope_tables_64(pos):
    half = MLA_ROPE // 2
    inv = jnp.power(ROPE_THETA, -jnp.arange(half, dtype=F32) * 2.0 / MLA_ROPE)
    ang = pos.astype(F32)[:, None] * inv[None, :]
    cos, sin = jnp.cos(ang), jnp.sin(ang)
    zero = jnp.zeros_like(sin)
    return (jnp.concatenate([cos, cos, cos, cos], -1),
            jnp.concatenate([-sin, zero, -sin, zero], -1),
            jnp.concatenate([zero, sin, zero, sin], -1))


def _rope128(z, cos, sin):
    return z * cos + pltpu.roll(z, LANES // 2, 1) * sin


def _rope64(z, cos, sin_lo, sin_hi):
    return z * cos + pltpu.roll(z, LANES - MLA_ROPE // 2, 1) * sin_lo + pltpu.roll(z, MLA_ROPE // 2, 1) * sin_hi


def _nsa_proj_kernel(x_ref, w_ref, cos_ref, sin_ref, o_ref, *, rope_heads, n_q_heads, scale):
    x = x_ref[...]
    cos, sin = cos_ref[...], sin_ref[...]
    n_heads = w_ref.shape[1] // LANES
    for t in range(n_heads // 2):
        c0 = t * MXU_WIDTH_V7X
        z = jnp.dot(x, w_ref[:, c0:c0 + MXU_WIDTH_V7X], preferred_element_type=F32)
        for hh in range(2):
            head = 2 * t + hh
            zh = z[:, hh * LANES:(hh + 1) * LANES]
            if head in rope_heads:
                zh = _rope128(zh, cos, sin)
            if head < n_q_heads:
                zh = zh * scale
            o_ref[:, head * LANES:(head + 1) * LANES] = zh.astype(o_ref.dtype)


def _nsa_proj(xb, w, cos, sin, S):
    T, D = xb.shape
    n = w.shape[1]
    tm = min(PROJ_TM, S)
    n_pos = S // tm
    kv0 = NSA_HEADS
    per = NSA_GROUPS * 2
    rope_heads = tuple(range(NSA_HEADS)) + tuple(kv0 + per * br + g for br in (1, 2) for g in range(NSA_GROUPS))
    return pl.pallas_call(
        functools.partial(_nsa_proj_kernel, rope_heads=rope_heads, n_q_heads=NSA_HEADS, scale=NSA_DK ** -0.5),
        grid=(T // tm,),
        in_specs=[pl.BlockSpec((tm, D), lambda i: (i, 0)),
                  _resident((D, n)),
                  pl.BlockSpec((tm, LANES), lambda i: (i % n_pos, 0)),
                  pl.BlockSpec((tm, LANES), lambda i: (i % n_pos, 0))],
        out_specs=pl.BlockSpec((tm, n), lambda i: (i, 0)),
        out_shape=jax.ShapeDtypeStruct((T, n), BF16),
        compiler_params=_params(("parallel",)),
        name="nsa_proj",
    )(xb, w, cos, sin)


def _rmsnorm_rows(z, g):
    return z * lax.rsqrt(jnp.mean(jnp.square(z), -1, keepdims=True) + RMS_EPS) * g


def _mla_in_kernel(x_ref, w_ref, gq_ref, gkv_ref, wuq_ref, wukv_ref, cos_ref, slo_ref, shi_ref,
                   q_ref, kv_ref, kr_ref, gate_ref, *, scale):
    x = x_ref[...]
    cos, slo, shi = cos_ref[...], slo_ref[...], shi_ref[...]
    c1 = MLA_Q_RANK
    c2 = c1 + MLA_KV_RANK
    cq = jnp.dot(x, w_ref[:, :c1], preferred_element_type=F32)
    cqn = _rmsnorm_rows(cq, gq_ref[...]).astype(BF16)
    ckv = jnp.dot(x, w_ref[:, c1:c2], preferred_element_type=F32)
    ckvn = _rmsnorm_rows(ckv, gkv_ref[...]).astype(BF16)
    tail = jnp.dot(x, w_ref[:, c2:c2 + 2 * LANES], preferred_element_type=F32)
    kr_ref[...] = _rope64(tail[:, :LANES], cos, slo, shi).astype(kr_ref.dtype)
    gate_ref[...] = jax.nn.sigmoid(tail[:, LANES:])
    for h in range(MLA_HEADS):
        c0 = h * MXU_WIDTH_V7X
        z = jnp.dot(cqn, wuq_ref[:, c0:c0 + MXU_WIDTH_V7X], preferred_element_type=F32)
        q_ref[:, c0:c0 + LANES] = (z[:, :LANES] * scale).astype(q_ref.dtype)
        q_ref[:, c0 + LANES:c0 + 2 * LANES] = (_rope64(z[:, LANES:], cos, slo, shi) * scale).astype(q_ref.dtype)
    n_kv = wukv_ref.shape[1]
    for c0 in range(0, n_kv, MXU_WIDTH_V7X):
        kv_ref[:, c0:c0 + MXU_WIDTH_V7X] = jnp.dot(
            ckvn, wukv_ref[:, c0:c0 + MXU_WIDTH_V7X], preferred_element_type=F32).astype(kv_ref.dtype)


def _mla_in(xb, w, gq, gkv, wuq, wukv, cos, slo, shi, S):
    T, D = xb.shape
    tm = min(PROJ_TM, S)
    n_pos = S // tm
    tbl = pl.BlockSpec((tm, LANES), lambda i: (i % n_pos, 0))
    nq, nkv = wuq.shape[1], wukv.shape[1]
    return pl.pallas_call(
        functools.partial(_mla_in_kernel, scale=(MLA_NOPE + MLA_ROPE) ** -0.5),
        grid=(T // tm,),
        in_specs=[pl.BlockSpec((tm, D), lambda i: (i, 0)),
                  _resident(w.shape), _resident(gq.shape), _resident(gkv.shape),
                  _resident(wuq.shape), _resident(wukv.shape), tbl, tbl, tbl],
        out_specs=[pl.BlockSpec((tm, nq), lambda i: (i, 0)),
                   pl.BlockSpec((tm, nkv), lambda i: (i, 0)),
                   pl.BlockSpec((tm, LANES), lambda i: (i, 0)),
                   pl.BlockSpec((tm, LANES), lambda i: (i, 0))],
        out_shape=[jax.ShapeDtypeStruct((T, nq), BF16), jax.ShapeDtypeStruct((T, nkv), BF16),
                   jax.ShapeDtypeStruct((T, LANES), BF16), jax.ShapeDtypeStruct((T, LANES), F32)],
        compiler_params=_params(("parallel",)),
        name="mla_in",
    )(xb, w, gq, gkv, wuq, wukv, cos, slo, shi)


MASK_BIG = 16384.0


def _softmax_reset(m_s, l_s, acc_s):
    m_s[...] = jnp.full(m_s.shape, NEG_INF, F32)
    l_s[...] = jnp.zeros(l_s.shape, F32)
    acc_s[...] = jnp.zeros(acc_s.shape, F32)


def _online_softmax_step(s, v, m_s, l_s, acc_s):
    m_prev = m_s[...]
    m_new = jnp.maximum(m_prev, jnp.max(s, axis=-1, keepdims=True))
    alpha = jnp.exp(m_prev - m_new)
    p = jnp.exp(s - m_new)
    l_s[...] = alpha * l_s[...] + jnp.sum(p, axis=-1, keepdims=True)
    acc_s[...] = alpha * acc_s[...] + jnp.dot(p.astype(BF16), v, preferred_element_type=F32)
    m_s[...] = m_new


def _online_softmax_step_t(st, vt, m_s, l_s, acc_s):
    m_prev = m_s[...]
    m_new = jnp.maximum(m_prev, jnp.max(st, axis=0, keepdims=True))
    alpha = jnp.exp(m_prev - m_new)
    p = jnp.exp(st - m_new)
    l_s[...] = alpha * l_s[...] + jnp.sum(p, axis=0, keepdims=True)
    acc_s[...] = alpha * acc_s[...] + jnp.dot(vt, p.astype(BF16), preferred_element_type=F32)
    m_s[...] = m_new


def _dot_nt(a, b):
    return lax.dot_general(a, b, (((1,), (1,)), ((), ())), preferred_element_type=F32)


MLA_TQ = 512
MLA_TK = 512


MLA_HPS = 2


def _mla_attn_kernel(qt_ref, kn_ref, kr_ref, vt_ref, o_ref, m_s, l_s, acc_s, *, tq, tk, hps):
    q0 = pl.program_id(2) * tq
    qts = [qt_ref[0, hh * 2 * LANES:(hh + 1) * 2 * LANES, :] for hh in range(hps)]
    _softmax_reset(m_s, l_s, acc_s)

    def tile(j, causal):
        k0 = pl.multiple_of(j * tk, tk)
        kr = kr_ref[0, pl.ds(k0, tk), :]
        for hh in range(hps):
            hs = slice(hh * LANES, (hh + 1) * LANES)
            ka = jnp.concatenate([kn_ref[0, pl.ds(k0, tk), hs], kr], axis=1)
            st = jnp.dot(ka, qts[hh], preferred_element_type=F32)
            if causal:
                kpos = k0 + lax.broadcasted_iota(jnp.int32, (tk, tq), 0)
                qpos = q0 + lax.broadcasted_iota(jnp.int32, (tk, tq), 1)
                st = jnp.where(kpos <= qpos, st, -MASK_BIG)
            _online_softmax_step_t(st, vt_ref[0, hs, pl.ds(k0, tk)], m_s.at[hh], l_s.at[hh], acc_s.at[hh])

    j_diag = q0 // tk

    def body(j, carry):
        tile(j, False)
        return carry

    lax.fori_loop(0, j_diag, body, 0)
    tile(j_diag, True)
    for hh in range(hps):
        o_ref[0, :, hh * LANES:(hh + 1) * LANES] = jnp.transpose(acc_s[hh] / l_s[hh]).astype(o_ref.dtype)


def _mla_attention(qt, kv, kr, vt):
    B, _, S = qt.shape
    H = MLA_HEADS
    hps = MLA_HPS
    tq = tk = min(MLA_TQ, S)
    return pl.pallas_call(
        functools.partial(_mla_attn_kernel, tq=tq, tk=tk, hps=hps),
        grid=(B, H // hps, S // tq),
        in_specs=[pl.BlockSpec((1, hps * 2 * LANES, tq), lambda b, h, i: (b, h, i)),
                  pl.BlockSpec((1, S, hps * LANES), lambda b, h, i: (b, 0, h)),
                  pl.BlockSpec((1, S, LANES), lambda b, h, i: (b, 0, 0)),
                  pl.BlockSpec((1, hps * LANES, S), lambda b, h, i: (b, h, 0))],
        out_specs=pl.BlockSpec((1, tq, hps * LANES), lambda b, h, i: (b, i, h)),
        out_shape=jax.ShapeDtypeStruct((B, S, H * MLA_DV), BF16),
        scratch_shapes=[pltpu.VMEM((hps, 1, tq), F32), pltpu.VMEM((hps, 1, tq), F32),
                        pltpu.VMEM((hps, MLA_DV, tq), F32)],
        compiler_params=_params(("parallel", "parallel", "arbitrary")),
        name="mla_attn",
    )(qt, kv, kr, vt)


def _gelu_tanh(x):
    return 0.5 * x * (1.0 + jnp.tanh(math.sqrt(2.0 / math.pi) * (x + 0.044715 * (x * x * x))))


def _compress_kernel(z_ref, pe_ref, w1_ref, w2_ref, cos_ref, sin_ref, o_ref, *, hid):
    which = pl.program_id(1)
    z = z_ref[0, 0, 0]
    n_chunks = z.shape[0]
    cd = z.shape[1]
    ab = jnp.dot(z, w1_ref[0], preferred_element_type=F32)
    a_part = ab[:, :hid]
    b_next = pltpu.roll(ab[:, hid:], n_chunks - 1, 0)
    pe_a = jnp.dot(pe_ref[0, :, :cd], w1_ref[0, :, :hid], preferred_element_type=F32)
    pe_b = jnp.dot(pe_ref[0, :, cd:], w1_ref[0, :, hid:], preferred_element_type=F32)
    h = _gelu_tanh(a_part + b_next + pe_a[0:1] + pe_b[0:1])
    out = jnp.dot(h.astype(BF16), w2_ref[0], preferred_element_type=F32)
    roped = _rope128(out, cos_ref[...], sin_ref[...])
    o_ref[0, 0, 0] = jnp.where(which == 0, roped, out).astype(o_ref.dtype)


def _nsa_compress(zc, pe, w1, w2, cos_c, sin_c):
    B, _, G, n_chunks, cd = zc.shape
    hid = w2.shape[1]
    d = w2.shape[2]
    return pl.pallas_call(
        functools.partial(_compress_kernel, hid=hid),
        grid=(B, 2, G),
        in_specs=[pl.BlockSpec((1, 1, 1, n_chunks, cd), lambda b, t, g: (b, t, g, 0, 0)),
                  pl.BlockSpec((1, 8, 2 * cd), lambda b, t, g: (t, 0, 0)),
                  pl.BlockSpec((1, cd, 2 * hid), lambda b, t, g: (t, 0, 0)),
                  pl.BlockSpec((1, hid, d), lambda b, t, g: (t, 0, 0)),
                  pl.BlockSpec((n_chunks, LANES), lambda b, t, g: (0, 0)),
                  pl.BlockSpec((n_chunks, LANES), lambda b, t, g: (0, 0))],
        out_specs=pl.BlockSpec((1, 1, 1, n_chunks, d), lambda b, t, g: (b, t, g, 0, 0)),
        out_shape=jax.ShapeDtypeStruct((B, 2, G, n_chunks, d), BF16),
        compiler_params=_params(("parallel", "parallel", "parallel")),
        name="nsa_compress",
    )(zc, pe, w1, w2, cos_c, sin_c)


NSA_TQ = 256
NSA_TK = 512


def _nsa_kernel(qt_ref, g_ref, kc_ref, vct_ref, ks_ref, vst_ref, kw_ref, vwt_ref, o_ref,
                m_s, l_s, acc_s, out_s, *, tq, tk, hpg, dk, n_cmp, n_slc):
    q0 = pl.program_id(2) * tq
    cols = hpg * tq
    i32 = jnp.int32
    qt = jnp.concatenate([qt_ref[0, h * dk:(h + 1) * dk, :] for h in range(hpg)], axis=1)
    gates = g_ref[0, 0]

    def gate_row(branch):
        return jnp.concatenate([gates[3 * h + branch:3 * h + branch + 1, :] for h in range(hpg)], axis=1)

    def qpos_of(n_rows):
        return q0 + lax.rem(lax.broadcasted_iota(i32, (n_rows, cols), 1), tq)

    n_pad = kc_ref.shape[3]
    sc = jnp.dot(kc_ref[0, 0, 0], qt, preferred_element_type=F32)
    n_io = lax.broadcasted_iota(i32, (n_pad, cols), 0)
    vis = (n_io * CMP_STRIDE + (CMP_BLK - 1) <= qpos_of(n_pad)) & (n_io < n_cmp)
    sc = jnp.where(vis, sc, NEG_INF)
    e = jnp.where(vis, jnp.exp(sc - jnp.max(sc, axis=0, keepdims=True)), 0.0)
    p = e / jnp.maximum(jnp.sum(e, axis=0, keepdims=True), 1e-30)
    o_cmp = jnp.dot(vct_ref[0, 0], p.astype(BF16), preferred_element_type=F32)
    out_s[...] = gate_row(0) * o_cmp

    imp = p[:, 0:tq]
    for h in range(1, hpg):
        imp = imp + p[:, h * tq:(h + 1) * tq]
    ratio = SLC_BLK // CMP_STRIDE
    span = CMP_BLK // CMP_STRIDE
    j_p = lax.broadcasted_iota(i32, (n_slc, n_pad), 0)
    c_p = lax.broadcasted_iota(i32, (n_slc, n_pad), 1)
    lo_c = ratio * j_p - (span - 1)
    pool = jnp.where((c_p >= lo_c) & (c_p < lo_c + ratio + span - 1), 1.0, 0.0).astype(BF16)
    hi = imp.astype(BF16)
    r1 = imp - hi.astype(F32)
    mid = r1.astype(BF16)
    lo = (r1 - mid.astype(F32)).astype(BF16)
    p_slc = (jnp.dot(pool, hi, preferred_element_type=F32) + jnp.dot(pool, mid, preferred_element_type=F32)
             + jnp.dot(pool, lo, preferred_element_type=F32))

    j_io = lax.broadcasted_iota(i32, (n_slc, tq), 0)
    qpos_l = q0 + lax.broadcasted_iota(i32, (n_slc, tq), 1)
    cur = qpos_l // SLC_BLK
    forced = (j_io == 0) | (j_io == cur) | (j_io == cur - 1)
    valid = j_io * SLC_BLK <= qpos_l
    score = jnp.where(forced, FORCE_SCORE, jnp.where(valid, p_slc, -FORCE_SCORE))
    bias = jnp.where(_rank_before(score) < N_SEL, 0.0, -MASK_BIG)
    if n_slc < LANES:
        bias = jnp.concatenate([bias, jnp.full((LANES - n_slc, tq), -MASK_BIG, F32)], axis=0)
    bias = bias.astype(BF16)
    qa = jnp.concatenate([qt, jnp.concatenate([bias] * hpg, axis=1)], axis=0)

    qpos_k = qpos_of(tk)
    k_io = lax.broadcasted_iota(i32, (tk, cols), 0)

    _softmax_reset(m_s, l_s, acc_s)

    def slc_tile(j, causal):
        k0 = pl.multiple_of(j * tk, tk)
        kblk = (k0 + lax.broadcasted_iota(i32, (tk, LANES), 0)) // SLC_BLK
        onehot = jnp.where(kblk == lax.broadcasted_iota(i32, (tk, LANES), 1), 1.0, 0.0).astype(BF16)
        ka = jnp.concatenate([ks_ref[0, pl.ds(k0, tk), :], onehot], axis=1)
        s = jnp.dot(ka, qa, preferred_element_type=F32)
        if causal:
            s = jnp.where(k0 + k_io <= qpos_k, s, -MASK_BIG)
        _online_softmax_step_t(s, vst_ref[0, :, pl.ds(k0, tk)], m_s, l_s, acc_s)

    j_diag = q0 // tk

    def body(j, carry):
        slc_tile(j, False)
        return carry

    lax.fori_loop(0, j_diag, body, 0)
    slc_tile(j_diag, True)
    out_s[...] += gate_row(1) * (acc_s[...] / l_s[...])

    _softmax_reset(m_s, l_s, acc_s)

    def win_tile(j):
        k0 = pl.multiple_of(j * tk, tk)
        s = jnp.dot(kw_ref[0, pl.ds(k0, tk), :], qt, preferred_element_type=F32)
        kpos = k0 + k_io
        s = jnp.where((kpos <= qpos_k) & (kpos > qpos_k - WIN), s, -MASK_BIG)
        _online_softmax_step_t(s, vwt_ref[0, :, pl.ds(k0, tk)], m_s, l_s, acc_s)

    @pl.when(j_diag >= 1)
    def _():
        win_tile(j_diag - 1)

    win_tile(j_diag)
    out = out_s[...] + gate_row(2) * (acc_s[...] / l_s[...])
    for h in range(hpg):
        o_ref[0, :, h * dk:(h + 1) * dk] = jnp.transpose(out[:, h * tq:(h + 1) * tq]).astype(o_ref.dtype)


def _nsa_attention(qkv, qt, gates_t, cmp_kv, vct, vt):
    B, S, _ = qkv.shape
    G, hpg, dk = NSA_GROUPS, NSA_HPG, NSA_DK
    n_pad = cmp_kv.shape[3]
    n_cmp = (S - CMP_BLK) // CMP_STRIDE + 1
    tq, tk = min(NSA_TQ, S), min(NSA_TK, S)
    assert tk == WIN and tq <= tk
    cols = hpg * tq
    kv0 = NSA_HEADS

    def k_spec(branch):
        base = kv0 + 2 * G * branch
        return pl.BlockSpec((1, S, dk), lambda b, g, i: (b, 0, base + g))

    def vt_spec(branch):
        return pl.BlockSpec((1, dk, S), lambda b, g, i: (b, (branch - 1) * G + g, 0))

    return pl.pallas_call(
        functools.partial(_nsa_kernel, tq=tq, tk=tk, hpg=hpg, dk=dk, n_cmp=n_cmp, n_slc=S // SLC_BLK),
        grid=(B, G, S // tq),
        in_specs=[pl.BlockSpec((1, hpg * dk, tq), lambda b, g, i: (b, g, i)),
                  pl.BlockSpec((1, 1, LANES, tq), lambda b, g, i: (b, g, 0, i)),
                  pl.BlockSpec((1, 1, 1, n_pad, dk), lambda b, g, i: (b, 0, g, 0, 0)),
                  pl.BlockSpec((1, 1, dk, n_pad), lambda b, g, i: (b, g, 0, 0)),
                  k_spec(1), vt_spec(1), k_spec(2), vt_spec(2)],
        out_specs=pl.BlockSpec((1, tq, hpg * dk), lambda b, g, i: (b, i, g)),
        out_shape=jax.ShapeDtypeStruct((B, S, NSA_HEADS * NSA_DV), BF16),
        scratch_shapes=[pltpu.VMEM((1, cols), F32), pltpu.VMEM((1, cols), F32),
                        pltpu.VMEM((dk, cols), F32), pltpu.VMEM((dk, cols), F32)],
        compiler_params=_params(("parallel", "parallel", "arbitrary")),
        name="nsa_attn",
    )(qt, gates_t, cmp_kv, vct, qkv, vt, qkv, vt)


MOE_TM = 256
MOE_FFN_TM = 256


def _hidden_chunks(f):
    chunks, c0 = [], 0
    while c0 < f:
        cs = min(MXU_WIDTH_V7X, f - c0)
        chunks.append((c0, cs))
        c0 += cs
    return tuple(chunks)


def _swiglu(x, wgu_ref, wd_ref, h_s, chunks):
    for c0, cs in chunks:
        gu = jnp.dot(x, wgu_ref[0, :, 2 * c0:2 * c0 + 2 * cs], preferred_element_type=F32)
        g, u = gu[:, :cs], gu[:, cs:]
        h_s[:, c0:c0 + cs] = (g * jax.nn.sigmoid(g) * u).astype(BF16)
    return jnp.dot(h_s[...], wd_ref[0], preferred_element_type=F32)


def _moe_ffn_kernel(blk_e_ref, n_used_ref, x_ref, wgu_ref, wd_ref, o_ref, h_s, *, chunks):
    del blk_e_ref

    @pl.when(pl.program_id(0) < n_used_ref[0])
    def _():
        o_ref[...] = _swiglu(x_ref[...], wgu_ref, wd_ref, h_s, chunks).astype(o_ref.dtype)


def _cast_kernel(w_ref, o_ref):
    o_ref[...] = w_ref[...].astype(o_ref.dtype)


def _cast_bf16(w):
    E, R, C = w.shape
    return pl.pallas_call(
        _cast_kernel,
        grid=(E,),
        in_specs=[pl.BlockSpec((1, R, C), lambda e: (e, 0, 0))],
        out_specs=pl.BlockSpec((1, R, C), lambda e: (e, 0, 0)),
        out_shape=jax.ShapeDtypeStruct(w.shape, BF16),
        compiler_params=_params(("parallel",)),
        name="cast_bf16",
    )(w)


def _pack_gate_up_kernel(wg_ref, wu_ref, o_ref, *, chunks):
    for c0, cs in chunks:
        o_ref[0, :, 2 * c0:2 * c0 + cs] = wg_ref[0, :, c0:c0 + cs].astype(o_ref.dtype)
        o_ref[0, :, 2 * c0 + cs:2 * c0 + 2 * cs] = wu_ref[0, :, c0:c0 + cs].astype(o_ref.dtype)


PACK_ROWS = 512


def _pack_gate_up(w_gate, w_up):
    E, D, F = w_gate.shape
    rows = min(PACK_ROWS, D)
    spec = pl.BlockSpec((1, rows, F), lambda e, r: (e, r, 0))
    return pl.pallas_call(
        functools.partial(_pack_gate_up_kernel, chunks=_hidden_chunks(F)),
        grid=(E, D // rows),
        in_specs=[spec, spec],
        out_specs=pl.BlockSpec((1, rows, 2 * F), lambda e, r: (e, r, 0)),
        out_shape=jax.ShapeDtypeStruct((E, D, 2 * F), BF16),
        compiler_params=_params(("parallel", "parallel")),
        name="pack_gate_up",
    )(w_gate, w_up)


def _moe_routed(xb, eid_t, rank_t, counts, w_gate_up, w_down):
    T, D = xb.shape
    K = eid_t.shape[0]
    E, F, _ = w_down.shape
    tm = MOE_FFN_TM
    n = T * K
    n_blk = -(-n // tm) + E
    i32 = jnp.int32

    padded = (counts + tm - 1) // tm * tm
    pad_end = jnp.cumsum(padded)
    pad_start = pad_end - padded
    start_t = jnp.zeros_like(eid_t)
    for e in range(E):
        start_t = jnp.where(eid_t == e, pad_start[e], start_t)
    dest = (start_t + rank_t).reshape(n)
    tok = jnp.zeros((n_blk * tm,), i32).at[dest].set(jnp.tile(jnp.arange(T, dtype=i32), K))
    n_used = (pad_end[-1] // tm).astype(i32).reshape(1)
    blk_e = jnp.minimum(jnp.searchsorted(pad_end, jnp.arange(n_blk, dtype=i32) * tm, side='right'),
                        E - 1).astype(i32)

    xs = jnp.take(xb, tok, axis=0)

    def row_map(i, be, nu):
        return (jnp.minimum(i, nu[0] - 1), 0)

    y = pl.pallas_call(
        functools.partial(_moe_ffn_kernel, chunks=_hidden_chunks(F)),
        grid_spec=pltpu.PrefetchScalarGridSpec(
            num_scalar_prefetch=2,
            grid=(n_blk,),
            in_specs=[pl.BlockSpec((tm, D), row_map),
                      pl.BlockSpec((1, D, 2 * F), lambda i, be, nu: (be[i], 0, 0)),
                      pl.BlockSpec((1, F, D), lambda i, be, nu: (be[i], 0, 0))],
            out_specs=pl.BlockSpec((tm, D), row_map),
            scratch_shapes=[pltpu.VMEM((tm, F), BF16)]),
        out_shape=jax.ShapeDtypeStruct((n_blk * tm, D), BF16),
        compiler_params=_params(("arbitrary",)),
        name="moe_ffn",
    )(blk_e, n_used, xs, w_gate_up, w_down)

    return jnp.take(y, dest, axis=0).reshape(K, T, D)


def _layernorm_rows(z, g, b):
    mu = jnp.mean(z, -1, keepdims=True)
    zc = z - mu
    var = jnp.mean(jnp.square(zc), -1, keepdims=True)
    return zc * lax.rsqrt(var + LN_EPS) * g + b


def _merge_kernel(x_ref, yn_ref, ym_ref, wg_ref, wbn_ref, wbm_ref, o_ref):
    x, yn, ym = x_ref[...], yn_ref[...], ym_ref[...]
    d = o_ref.shape[1]
    for c0 in range(0, d, MXU_WIDTH_V7X):
        cs = slice(c0, c0 + MXU_WIDTH_V7X)
        gn = jax.nn.sigmoid(jnp.dot(x, wg_ref[:, cs], preferred_element_type=F32))
        gm = jax.nn.sigmoid(jnp.dot(x, wg_ref[:, d + c0:d + c0 + MXU_WIDTH_V7X], preferred_element_type=F32))
        bn = jnp.dot(yn, wbn_ref[:, cs], preferred_element_type=F32)
        bm = jnp.dot(ym, wbm_ref[:, cs], preferred_element_type=F32)
        o_ref[:, cs] = (gn * bn + gm * bm).astype(o_ref.dtype)


def _merge(xb, y_nsa, y_mla, w_g, wb_nsa, wb_mla):
    T, D = xb.shape
    tm = PROJ_TM
    row = lambda a: pl.BlockSpec((tm, a.shape[1]), lambda i: (i, 0))
    return pl.pallas_call(
        _merge_kernel,
        grid=(T // tm,),
        in_specs=[row(xb), row(y_nsa), row(y_mla), _resident(w_g.shape), _resident(wb_nsa.shape),
                  _resident(wb_mla.shape)],
        out_specs=pl.BlockSpec((tm, D), lambda i: (i, 0)),
        out_shape=jax.ShapeDtypeStruct((T, D), BF16),
        compiler_params=_params(("parallel",)),
        name="merge",
    )(xb, y_nsa, y_mla, w_g, wb_nsa, wb_mla)


def _rank_before(v):
    n = v.shape[0]
    j_io = lax.broadcasted_iota(jnp.int32, v.shape, 0)
    cnt = jnp.zeros(v.shape, jnp.int32)
    for i in range(n):
        row = v[i:i + 1, :]
        cnt = cnt + ((row > v) | ((row == v) & (j_io > i))).astype(jnp.int32)
    return cnt


def _route_tile(logits_t, rb, cnt_s):
    E, tm = logits_t.shape
    per = E // N_EXPERT_GROUPS
    scores = jax.nn.sigmoid(logits_t)
    biased = scores + rb
    grp = []
    for g in range(N_EXPERT_GROUPS):
        blk = biased[g * per:(g + 1) * per, :]
        m1 = jnp.max(blk, axis=0, keepdims=True)
        eq = blk == m1
        n_eq = jnp.sum(eq.astype(F32), axis=0, keepdims=True)
        m2 = jnp.max(jnp.where(eq, NEG_INF, blk), axis=0, keepdims=True)
        grp.append(m1 + jnp.where(n_eq >= 2.0, m1, m2))
    gsel = _rank_before(jnp.concatenate(grp, axis=0)) < TOPK_GROUPS
    masked = jnp.concatenate(
        [jnp.where(gsel[g:g + 1, :], biased[g * per:(g + 1) * per, :], NEG_INF) for g in range(N_EXPERT_GROUPS)],
        axis=0)
    sel = _rank_before(masked) < TOP_K
    w_sel = jnp.where(sel, scores, 0.0)
    w_sel = w_sel / jnp.sum(w_sel, axis=0, keepdims=True) * ROUTED_SCALE
    tri = jnp.where(lax.broadcasted_iota(jnp.int32, (E, E), 1) <= lax.broadcasted_iota(jnp.int32, (E, E), 0),
                    1.0, 0.0).astype(BF16)
    sel_b = jnp.where(sel, 1.0, 0.0).astype(BF16)
    slot = jnp.dot(tri, sel_b, preferred_element_type=F32)
    upper = jnp.where(lax.broadcasted_iota(jnp.int32, (tm, tm), 0) <= lax.broadcasted_iota(jnp.int32, (tm, tm), 1),
                      1.0, 0.0).astype(BF16)
    incl = jnp.dot(sel_b, upper, preferred_element_type=F32)
    pos = cnt_s[...] + incl - 1.0
    cnt_s[...] = cnt_s[...] + incl[:, tm - 1:tm]
    e_io = lax.broadcasted_iota(jnp.int32, (E, tm), 0)
    ids, wts, ranks = [], [], []
    for k in range(TOP_K):
        mk = sel & (slot == float(k + 1))
        ids.append(jnp.sum(jnp.where(mk, e_io, 0), axis=0, keepdims=True))
        wts.append(jnp.sum(jnp.where(mk, w_sel, 0.0), axis=0, keepdims=True))
        ranks.append(jnp.sum(jnp.where(mk, pos, 0.0), axis=0, keepdims=True).astype(jnp.int32))
    pad = 8 - TOP_K
    ids.append(jnp.zeros((pad, tm), jnp.int32))
    wts.append(jnp.zeros((pad, tm), F32))
    ranks.append(jnp.zeros((pad, tm), jnp.int32))
    return jnp.concatenate(ids, axis=0), jnp.concatenate(wts, axis=0), jnp.concatenate(ranks, axis=0)


def _out_ln1_kernel(m_ref, x_ref, wo_ref, g_ref, b_ref, wrt_ref, rb_ref,
                    x1_ref, x1b_ref, eid_ref, wts_ref, rank_ref, cnt_ref, cnt_s):
    @pl.when(pl.program_id(0) == 0)
    def _():
        cnt_s[...] = jnp.zeros(cnt_s.shape, F32)

    m = m_ref[...]
    d = x1_ref.shape[1]
    for c0 in range(0, d, MXU_WIDTH_V7X):
        cs = slice(c0, c0 + MXU_WIDTH_V7X)
        x1_ref[:, cs] = DN_ALPHA * x_ref[:, cs] + jnp.dot(m, wo_ref[:, cs], preferred_element_type=F32)
    x1 = _layernorm_rows(x1_ref[...], g_ref[...], b_ref[...])
    x1_ref[...] = x1
    x1b = x1.astype(BF16)
    x1b_ref[...] = x1b
    logits_t = _dot_nt(wrt_ref[...], x1b)
    eid_ref[...], wts_ref[...], rank_ref[...] = _route_tile(logits_t, rb_ref[...], cnt_s)
    cnt_ref[...] = cnt_s[...]


def _out_ln1(merged, x2, w_out, g, b, w_router_t, router_b):
    T, D = x2.shape
    E = w_router_t.shape[0]
    tm = MOE_TM
    row = lambda w: pl.BlockSpec((tm, w), lambda i: (i, 0))
    col = pl.BlockSpec((8, tm), lambda i: (0, i))
    return pl.pallas_call(
        _out_ln1_kernel,
        grid=(T // tm,),
        in_specs=[row(D), row(D), _resident(w_out.shape), _resident(g.shape), _resident(b.shape),
                  _resident(w_router_t.shape), _resident(router_b.shape)],
        out_specs=[row(D), row(D), col, col, col, pl.BlockSpec((E, 1), lambda i: (0, 0))],
        out_shape=[jax.ShapeDtypeStruct((T, D), F32), jax.ShapeDtypeStruct((T, D), BF16),
                   jax.ShapeDtypeStruct((8, T), jnp.int32), jax.ShapeDtypeStruct((8, T), F32),
                   jax.ShapeDtypeStruct((8, T), jnp.int32), jax.ShapeDtypeStruct((E, 1), F32)],
        scratch_shapes=[pltpu.VMEM((E, 1), F32)],
        compiler_params=_params(("arbitrary",)),
        name="out_ln1",
    )(merged, x2, w_out, g, b, w_router_t, router_b)


def _ffn_ln2_kernel(x1_ref, yk_ref, w_ref, sgu_ref, sd_ref, g_ref, b_ref, x2_ref, x2b_ref, h_s, *, chunks):
    x1 = x1_ref[...]
    acc = _swiglu(x1.astype(BF16), sgu_ref, sd_ref, h_s, chunks)
    w = w_ref[...]
    for k in range(yk_ref.shape[0]):
        acc = acc + yk_ref[k].astype(F32) * w[:, k:k + 1]
    x2 = _layernorm_rows(DN_ALPHA * x1 + acc, g_ref[...], b_ref[...])
    x2_ref[...] = x2
    x2b_ref[...] = x2.astype(BF16)


FFN_LN2_VMEM_LIMIT_BYTES = 56 * 1024 * 1024


def _ffn_ln2(x1, yk, wts, sh_gate_up, sh_down, g, b):
    T, D = x1.shape
    K = yk.shape[0]
    tm = MOE_TM
    F = sh_down.shape[1]
    row = lambda w: pl.BlockSpec((tm, w), lambda i: (i, 0))
    return pl.pallas_call(
        functools.partial(_ffn_ln2_kernel, chunks=_hidden_chunks(F)),
        grid=(T // tm,),
        in_specs=[row(D), pl.BlockSpec((K, tm, D), lambda i: (0, i, 0)), row(wts.shape[1]),
                  _resident(sh_gate_up.shape), _resident(sh_down.shape), _resident(g.shape), _resident(b.shape)],
        out_specs=[row(D), row(D)],
        out_shape=[jax.ShapeDtypeStruct((T, D), F32), jax.ShapeDtypeStruct((T, D), BF16)],
        scratch_shapes=[pltpu.VMEM((tm, F), BF16)],
        compiler_params=pltpu.CompilerParams(dimension_semantics=("parallel",),
                                             vmem_limit_bytes=FFN_LN2_VMEM_LIMIT_BYTES),
        name="ffn_ln2",
    )(x1, yk, wts, sh_gate_up, sh_down, g, b)


def _ple_kernel(x2_ref, x2b_ref, p_ref, wpg_ref, wpp_ref, o_ref):
    x2b = x2b_ref[...]
    pb = p_ref[...].astype(BF16)
    d = o_ref.shape[1]
    for c0 in range(0, d, MXU_WIDTH_V7X):
        cs = slice(c0, c0 + MXU_WIDTH_V7X)
        gate = jax.nn.sigmoid(jnp.dot(x2b, wpg_ref[:, cs], preferred_element_type=F32))
        pp = jnp.dot(pb, wpp_ref[:, cs], preferred_element_type=F32)
        o_ref[:, cs] = x2_ref[:, cs] + gate * pp


def _ple(x2, x2b, p2, w_pg, w_pp):
    T, D = x2.shape
    tm = PROJ_TM
    row = lambda w: pl.BlockSpec((tm, w), lambda i: (i, 0))
    return pl.pallas_call(
        _ple_kernel,
        grid=(T // tm,),
        in_specs=[row(D), row(D), row(p2.shape[1]), _resident(w_pg.shape), _resident(w_pp.shape)],
        out_specs=row(D),
        out_shape=jax.ShapeDtypeStruct((T, D), F32),
        compiler_params=_params(("parallel",)),
        name="ple",
    )(x2, x2b, p2, w_pg, w_pp)


def _attention_branches(xb, S, w_in, pe_k, pe_v, wk1, wk2, wv1, wv2, q_norm, w_uq, kv_norm, w_uk, w_uv):
    T, D = xb.shape
    B = T // S
    c0 = W_NSA_Q
    c1 = c0 + W_NSA_KV
    c2 = c1 + W_NSA_GATE
    c3 = c2 + W_MLA_CQ
    c4 = c3 + W_MLA_CKV
    c5 = c4 + W_MLA_KR
    G, dk = NSA_GROUPS, NSA_DK
    pos = jnp.arange(S)
    cos128, sin128 = _rope_tables_128(pos)
    cos64, slo64, shi64 = _rope_tables_64(pos)

    qkv = _nsa_proj(xb, w_in[:, :c1].astype(BF16), cos128, sin128, S)

    pad_kr = jnp.zeros((D, LANES - W_MLA_KR), w_in.dtype)
    pad_g = jnp.zeros((D, LANES - W_NSA_GATE), w_in.dtype)
    w_m = jnp.concatenate([w_in[:, c2:c4], w_in[:, c4:c5], pad_kr, w_in[:, c1:c2], pad_g], axis=1).astype(BF16)
    H = MLA_HEADS
    wq = w_uq.reshape(MLA_Q_RANK, H, MLA_NOPE + MLA_ROPE)
    wq = jnp.concatenate([wq, jnp.zeros((MLA_Q_RANK, H, 2 * LANES - MLA_NOPE - MLA_ROPE), wq.dtype)], axis=-1)
    wq = wq.reshape(MLA_Q_RANK, H * 2 * LANES).astype(BF16)
    wukv = jnp.concatenate([w_uk, w_uv], axis=1).astype(BF16)
    q_m, kv_m, kr_m, gates = _mla_in(xb, w_m, q_norm.reshape(1, -1), kv_norm.reshape(1, -1), wq, wukv,
                                     cos64, slo64, shi64, S)

    n_chunks = S // CMP_STRIDE
    kvc = qkv[:, NSA_HEADS * dk:(NSA_HEADS + 2 * G) * dk]
    zc = kvc.reshape(B, n_chunks, CMP_STRIDE, 2, G, dk).transpose(0, 3, 4, 1, 2, 5)
    zc = zc.reshape(B, 2, G, n_chunks, CMP_STRIDE * dk)
    pe = jnp.stack([pe_k.reshape(-1), pe_v.reshape(-1)])
    pe = jnp.broadcast_to(pe[:, None, :], (2, 8, pe.shape[-1])).astype(BF16)
    half = CMP_STRIDE * dk
    w1 = jnp.stack([jnp.concatenate([wk1[:half], wk1[half:]], axis=1),
                    jnp.concatenate([wv1[:half], wv1[half:]], axis=1)]).astype(BF16)
    w2 = jnp.stack([wk2, wv2]).astype(BF16)
    pos_c = jnp.arange(n_chunks) * CMP_STRIDE + (CMP_BLK - 1)
    cos_c, sin_c = _rope_tables_128(pos_c)
    cmp_kv = _nsa_compress(zc, pe, w1, w2, cos_c, sin_c)

    g3 = gates[:, :W_NSA_GATE].reshape(B, S, G, NSA_HPG * 3).transpose(0, 2, 3, 1)
    g3 = jnp.pad(g3, ((0, 0), (0, 0), (0, LANES - NSA_HPG * 3), (0, 0)))

    qkv3 = qkv.reshape(B, S, -1)
    qt = jnp.swapaxes(qkv3[:, :, :NSA_HEADS * dk], 1, 2)
    v0 = NSA_HEADS + 2 * G
    v_sw = jnp.concatenate([qkv3[:, :, (v0 + G) * dk:(v0 + 2 * G) * dk],
                            qkv3[:, :, (v0 + 3 * G) * dk:(v0 + 4 * G) * dk]], axis=-1)
    vt = jnp.swapaxes(v_sw, 1, 2)
    vct = jnp.swapaxes(cmp_kv[:, 1], 2, 3)
    y_nsa = _nsa_attention(qkv3, qt, g3, cmp_kv, vct, vt)
    kv_m = kv_m.reshape(B, S, -1)
    qt_m = jnp.swapaxes(q_m.reshape(B, S, -1), 1, 2)
    vt_m = jnp.swapaxes(kv_m[:, :, MLA_HEADS * MLA_NOPE:], 1, 2)
    y_mla = _mla_attention(qt_m, kv_m, kr_m.reshape(B, S, -1), vt_m)
    return y_nsa, y_mla


def kernel(x, p, w_in, nsa_pe_k, nsa_pe_v, nsa_cmp_k_w1, nsa_cmp_k_w2, nsa_cmp_v_w1, nsa_cmp_v_w2, mla_q_norm, mla_w_uq, mla_kv_norm, mla_w_uk, mla_w_uv, w_branch_nsa, w_branch_mla, w_out, ln1_g, ln1_b, router_w, router_b, exp_w_gate, exp_w_up, exp_w_down, sh_w_gate, sh_w_up, sh_w_down, ln2_g, ln2_b, ple_w_proj, ple_w_gate):
    B, S, D = x.shape
    T = B * S
    c5 = D_IN - W_MERGE
    for i in range(DEPTH):
        x2 = x.reshape(T, D)
        xb = x2.astype(BF16)
        y_nsa, y_mla = _attention_branches(
            xb, S, w_in[i], nsa_pe_k[i], nsa_pe_v[i], nsa_cmp_k_w1[i], nsa_cmp_k_w2[i],
            nsa_cmp_v_w1[i], nsa_cmp_v_w2[i], mla_q_norm[i], mla_w_uq[i], mla_kv_norm[i],
            mla_w_uk[i], mla_w_uv[i])

        merged = _merge(xb, y_nsa.reshape(T, -1), y_mla.reshape(T, -1), w_in[i][:, c5:].astype(BF16),
                        w_branch_nsa[i].astype(BF16), w_branch_mla[i].astype(BF16))
        x1, x1b, eid_t, wts_t, rank_t, counts = _out_ln1(
            merged, x2, w_out[i].astype(BF16), ln1_g[i].reshape(1, D), ln1_b[i].reshape(1, D),
            router_w[i].T.astype(BF16), router_b[i].reshape(N_EXPERTS, 1).astype(F32))

        yk = _moe_routed(x1b, eid_t[:TOP_K], rank_t[:TOP_K], counts.reshape(N_EXPERTS).astype(jnp.int32),
                         _pack_gate_up(exp_w_gate[i], exp_w_up[i]), _cast_bf16(exp_w_down[i]))
        x2n, x2b = _ffn_ln2(x1, yk, wts_t.T, _pack_gate_up(sh_w_gate[i][None], sh_w_up[i][None]),
                            sh_w_down[i][None].astype(BF16), ln2_g[i].reshape(1, D), ln2_b[i].reshape(1, D))
        out = _ple(x2n, x2b, p[i].reshape(T, PLE_DIM), ple_w_gate[i].astype(BF16), ple_w_proj[i].astype(BF16))
        x = out.reshape(B, S, D)
    return x
```

```python
import functools
import math

import jax
import jax.numpy as jnp
from jax import lax
from jax.experimental import pallas as pl
from jax.experimental.pallas import tpu as pltpu

D_MODEL = 2048
BATCH = 8
SEQ = 4096
DEPTH = 1

PLE_DIM = 256
ROPE_THETA = 10000.0
LN_EPS = 1e-5
RMS_EPS = 1e-6
NEG_INF = -1e30

NSA_HEADS = 8
NSA_GROUPS = 2
NSA_HPG = NSA_HEADS // NSA_GROUPS
NSA_DK = 128
NSA_DV = 128
CMP_BLK = 32
CMP_STRIDE = 16
CMP_HID = 256
SLC_BLK = 64
N_SEL = 16
WIN = 512
FORCE_SCORE = 1e4

MLA_HEADS = 8
MLA_Q_RANK = 768
MLA_KV_RANK = 512
MLA_NOPE = 128
MLA_ROPE = 64
MLA_DV = 128

N_EXPERTS = 64
TOP_K = 6
N_EXPERT_GROUPS = 8
TOPK_GROUPS = 4
D_EXPERT = 1408
ROUTED_SCALE = 2.5

W_NSA_Q = NSA_HEADS * NSA_DK
W_NSA_KV = 3 * NSA_GROUPS * (NSA_DK + NSA_DV)
W_NSA_GATE = 3 * NSA_HEADS
W_MLA_CQ = MLA_Q_RANK
W_MLA_CKV = MLA_KV_RANK
W_MLA_KR = MLA_ROPE
W_MERGE = 2 * D_MODEL
D_IN = W_NSA_Q + W_NSA_KV + W_NSA_GATE + W_MLA_CQ + W_MLA_CKV + W_MLA_KR + W_MERGE

DN_ALPHA = (2.0 * DEPTH) ** 0.25
DN_BETA = (8.0 * DEPTH) ** -0.25

V7X_VMEM_LIMIT_BYTES = 48 * 1024 * 1024
LANES = 128
MXU_WIDTH_V7X = 256
PROJ_TM = 512

BF16 = jnp.bfloat16
F32 = jnp.float32


def _params(sem):
    return pltpu.CompilerParams(dimension_semantics=sem, vmem_limit_bytes=V7X_VMEM_LIMIT_BYTES)


def _resident(shape):
    return pl.BlockSpec(shape, lambda *_: (0,) * len(shape), pipeline_mode=pl.Buffered(1))


def _rope_tables_128(pos):
    half = LANES // 2
    inv = jnp.power(ROPE_THETA, -jnp.arange(half, dtype=F32) * 2.0 / LANES)
    ang = pos.astype(F32)[:, None] * inv[None, :]
    cos, sin = jnp.cos(ang), jnp.sin(ang)
    return jnp.concatenate([cos, cos], -1), jnp.concatenate([-sin, sin], -1)


def _rope_tables_64(pos):
    half = MLA_ROPE // 2
    inv = jnp.power(ROPE_THETA, -jnp.arange(half, dtype=F32) * 2.0 / MLA_ROPE)
    ang = pos.astype(F32)[:, None] * inv[None, :]
    cos, sin = jnp.cos(ang), jnp.sin(ang)
    zero = jnp.zeros_like(sin)
    return (jnp.concatenate([cos, cos, cos, cos], -1),
            jnp.concatenate([-sin, zero, -sin, zero], -1),
            jnp.concatenate([zero, sin, zero, sin], -1))


def _rope128(z, cos, sin):
    return z * cos + pltpu.roll(z, LANES // 2, 1) * sin


def _rope64(z, cos, sin_lo, sin_hi):
    return z * cos + pltpu.roll(z, LANES - MLA_ROPE // 2, 1) * sin_lo + pltpu.roll(z, MLA_ROPE // 2, 1) * sin_hi


def _nsa_proj_kernel(x_ref, w_ref, cos_ref, sin_ref, o_ref, *, rope_heads, n_q_heads, scale):
    x = x_ref[...]
    cos, sin = cos_ref[...], sin_ref[...]
    n_heads = w_ref.shape[1] // LANES
    for t in range(n_heads // 2):
        c0 = t * MXU_WIDTH_V7X
        z = jnp.dot(x, w_ref[:, c0:c0 + MXU_WIDTH_V7X], preferred_element_type=F32)
        for hh in range(2):
            head = 2 * t + hh
            zh = z[:, hh * LANES:(hh + 1) * LANES]
            if head in rope_heads:
                zh = _rope128(zh, cos, sin)
            if head < n_q_heads:
                zh = zh * scale
            o_ref[:, head * LANES:(head + 1) * LANES] = zh.astype(o_ref.dtype)


def _nsa_proj(xb, w, cos, sin, S):
    T, D = xb.shape
    n = w.shape[1]
    tm = min(PROJ_TM, S)
    n_pos = S // tm
    kv0 = NSA_HEADS
    per = NSA_GROUPS * 2
    rope_heads = tuple(range(NSA_HEADS)) + tuple(kv0 + per * br + g for br in (1, 2) for g in range(NSA_GROUPS))
    return pl.pallas_call(
        functools.partial(_nsa_proj_kernel, rope_heads=rope_heads, n_q_heads=NSA_HEADS, scale=NSA_DK ** -0.5),
        grid=(T // tm,),
        in_specs=[pl.BlockSpec((tm, D), lambda i: (i, 0)),
                  _resident((D, n)),
                  pl.BlockSpec((tm, LANES), lambda i: (i % n_pos, 0)),
                  pl.BlockSpec((tm, LANES), lambda i: (i % n_pos, 0))],
        out_specs=pl.BlockSpec((tm, n), lambda i: (i, 0)),
        out_shape=jax.ShapeDtypeStruct((T, n), BF16),
        compiler_params=_params(("parallel",)),
        name="nsa_proj",
    )(xb, w, cos, sin)


def _rmsnorm_rows(z, g):
    return z * lax.rsqrt(jnp.mean(jnp.square(z), -1, keepdims=True) + RMS_EPS) * g


def _mla_in_kernel(x_ref, w_ref, gq_ref, gkv_ref, wuq_ref, wukv_ref, cos_ref, slo_ref, shi_ref,
                   q_ref, kv_ref, kr_ref, gate_ref, *, scale):
    x = x_ref[...]
    cos, slo, shi = cos_ref[...], slo_ref[...], shi_ref[...]
    c1 = MLA_Q_RANK
    c2 = c1 + MLA_KV_RANK
    cq = jnp.dot(x, w_ref[:, :c1], preferred_element_type=F32)
    cqn = _rmsnorm_rows(cq, gq_ref[...]).astype(BF16)
    ckv = jnp.dot(x, w_ref[:, c1:c2], preferred_element_type=F32)
    ckvn = _rmsnorm_rows(ckv, gkv_ref[...]).astype(BF16)
    tail = jnp.dot(x, w_ref[:, c2:c2 + 2 * LANES], preferred_element_type=F32)
    kr_ref[...] = _rope64(tail[:, :LANES], cos, slo, shi).astype(kr_ref.dtype)
    gate_ref[...] = jax.nn.sigmoid(tail[:, LANES:])
    for h in range(MLA_HEADS):
        c0 = h * MXU_WIDTH_V7X
        z = jnp.dot(cqn, wuq_ref[:, c0:c0 + MXU_WIDTH_V7X], preferred_element_type=F32)
        q_ref[:, c0:c0 + LANES] = (z[:, :LANES] * scale).astype(q_ref.dtype)
        q_ref[:, c0 + LANES:c0 + 2 * LANES] = (_rope64(z[:, LANES:], cos, slo, shi) * scale).astype(q_ref.dtype)
    n_kv = wukv_ref.shape[1]
    for c0 in range(0, n_kv, MXU_WIDTH_V7X):
        kv_ref[:, c0:c0 + MXU_WIDTH_V7X] = jnp.dot(
            ckvn, wukv_ref[:, c0:c0 + MXU_WIDTH_V7X], preferred_element_type=F32).astype(kv_ref.dtype)


def _mla_in(xb, w, gq, gkv, wuq, wukv, cos, slo, shi, S):
    T, D = xb.shape
    tm = min(PROJ_TM, S)
    n_pos = S // tm
    tbl = pl.BlockSpec((tm, LANES), lambda i: (i % n_pos, 0))
    nq, nkv = wuq.shape[1], wukv.shape[1]
    return pl.pallas_call(
        functools.partial(_mla_in_kernel, scale=(MLA_NOPE + MLA_ROPE) ** -0.5),
        grid=(T // tm,),
        in_specs=[pl.BlockSpec((tm, D), lambda i: (i, 0)),
                  _resident(w.shape), _resident(gq.shape), _resident(gkv.shape),
                  _resident(wuq.shape), _resident(wukv.shape), tbl, tbl, tbl],
        out_specs=[pl.BlockSpec((tm, nq), lambda i: (i, 0)),
                   pl.BlockSpec((tm, nkv), lambda i: (i, 0)),
                   pl.BlockSpec((tm, LANES), lambda i: (i, 0)),
                   pl.BlockSpec((tm, LANES), lambda i: (i, 0))],
        out_shape=[jax.ShapeDtypeStruct((T, nq), BF16), jax.ShapeDtypeStruct((T, nkv), BF16),
                   jax.ShapeDtypeStruct((T, LANES), BF16), jax.ShapeDtypeStruct((T, LANES), F32)],
        compiler_params=_params(("parallel",)),
        name="mla_in",
    )(xb, w, gq, gkv, wuq, wukv, cos, slo, shi)


MASK_BIG = 16384.0


def _softmax_reset(m_s, l_s, acc_s):
    m_s[...] = jnp.full(m_s.shape, NEG_INF, F32)
    l_s[...] = jnp.zeros(l_s.shape, F32)
    acc_s[...] = jnp.zeros(acc_s.shape, F32)


def _online_softmax_step(s, v, m_s, l_s, acc_s):
    m_prev = m_s[...]
    m_new = jnp.maximum(m_prev, jnp.max(s, axis=-1, keepdims=True))
    alpha = jnp.exp(m_prev - m_new)
    p = jnp.exp(s - m_new)
    l_s[...] = alpha * l_s[...] + jnp.sum(p, axis=-1, keepdims=True)
    acc_s[...] = alpha * acc_s[...] + jnp.dot(p.astype(BF16), v, preferred_element_type=F32)
    m_s[...] = m_new


def _online_softmax_step_t(st, vt, m_s, l_s, acc_s):
    m_prev = m_s[...]
    m_new = jnp.maximum(m_prev, jnp.max(st, axis=0, keepdims=True))
    alpha = jnp.exp(m_prev - m_new)
    p = jnp.exp(st - m_new)
    l_s[...] = alpha * l_s[...] + jnp.sum(p, axis=0, keepdims=True)
    acc_s[...] = alpha * acc_s[...] + jnp.dot(vt, p.astype(BF16), preferred_element_type=F32)
    m_s[...] = m_new


def _dot_nt(a, b):
    return lax.dot_general(a, b, (((1,), (1,)), ((), ())), preferred_element_type=F32)


MLA_TQ = 512
MLA_TK = 512


MLA_HPS = 2


def _mla_attn_kernel(qt_ref, kn_ref, kr_ref, vt_ref, o_ref, m_s, l_s, acc_s, *, tq, tk, hps):
    q0 = pl.program_id(2) * tq
    qts = [qt_ref[0, hh * 2 * LANES:(hh + 1) * 2 * LANES, :] for hh in range(hps)]
    _softmax_reset(m_s, l_s, acc_s)

    def tile(j, causal):
        k0 = pl.multiple_of(j * tk, tk)
        kr = kr_ref[0, pl.ds(k0, tk), :]
        for hh in range(hps):
            hs = slice(hh * LANES, (hh + 1) * LANES)
            ka = jnp.concatenate([kn_ref[0, pl.ds(k0, tk), hs], kr], axis=1)
            st = jnp.dot(ka, qts[hh], preferred_element_type=F32)
            if causal:
                kpos = k0 + lax.broadcasted_iota(jnp.int32, (tk, tq), 0)
                qpos = q0 + lax.broadcasted_iota(jnp.int32, (tk, tq), 1)
                st = jnp.where(kpos <= qpos, st, -MASK_BIG)
            _online_softmax_step_t(st, vt_ref[0, hs, pl.ds(k0, tk)], m_s.at[hh], l_s.at[hh], acc_s.at[hh])

    j_diag = q0 // tk

    def body(j, carry):
        tile(j, False)
        return carry

    lax.fori_loop(0, j_diag, body, 0)
    tile(j_diag, True)
    for hh in range(hps):
        o_ref[0, :, hh * LANES:(hh + 1) * LANES] = jnp.transpose(acc_s[hh] / l_s[hh]).astype(o_ref.dtype)


def _mla_attention(qt, kv, kr, vt):
    B, _, S = qt.shape
    H = MLA_HEADS
    hps = MLA_HPS
    tq = tk = min(MLA_TQ, S)
    return pl.pallas_call(
        functools.partial(_mla_attn_kernel, tq=tq, tk=tk, hps=hps),
        grid=(B, H // hps, S // tq),
        in_specs=[pl.BlockSpec((1, hps * 2 * LANES, tq), lambda b, h, i: (b, h, i)),
                  pl.BlockSpec((1, S, hps * LANES), lambda b, h, i: (b, 0, h)),
                  pl.BlockSpec((1, S, LANES), lambda b, h, i: (b, 0, 0)),
                  pl.BlockSpec((1, hps * LANES, S), lambda b, h, i: (b, h, 0))],
        out_specs=pl.BlockSpec((1, tq, hps * LANES), lambda b, h, i: (b, i, h)),
        out_shape=jax.ShapeDtypeStruct((B, S, H * MLA_DV), BF16),
        scratch_shapes=[pltpu.VMEM((hps, 1, tq), F32), pltpu.VMEM((hps, 1, tq), F32),
                        pltpu.VMEM((hps, MLA_DV, tq), F32)],
        compiler_params=_params(("parallel", "parallel", "arbitrary")),
        name="mla_attn",
    )(qt, kv, kr, vt)


def _gelu_tanh(x):
    return 0.5 * x * (1.0 + jnp.tanh(math.sqrt(2.0 / math.pi) * (x + 0.044715 * (x * x * x))))


def _compress_kernel(z_ref, pe_ref, w1_ref, w2_ref, cos_ref, sin_ref, o_ref, *, hid):
    which = pl.program_id(1)
    z = z_ref[0, 0, 0]
    n_chunks = z.shape[0]
    cd = z.shape[1]
    ab = jnp.dot(z, w1_ref[0], preferred_element_type=F32)
    a_part = ab[:, :hid]
    b_next = pltpu.roll(ab[:, hid:], n_chunks - 1, 0)
    pe_a = jnp.dot(pe_ref[0, :, :cd], w1_ref[0, :, :hid], preferred_element_type=F32)
    pe_b = jnp.dot(pe_ref[0, :, cd:], w1_ref[0, :, hid:], preferred_element_type=F32)
    h = _gelu_tanh(a_part + b_next + pe_a[0:1] + pe_b[0:1])
    out = jnp.dot(h.astype(BF16), w2_ref[0], preferred_element_type=F32)
    roped = _rope128(out, cos_ref[...], sin_ref[...])
    o_ref[0, 0, 0] = jnp.where(which == 0, roped, out).astype(o_ref.dtype)


def _nsa_compress(zc, pe, w1, w2, cos_c, sin_c):
    B, _, G, n_chunks, cd = zc.shape
    hid = w2.shape[1]
    d = w2.shape[2]
    return pl.pallas_call(
        functools.partial(_compress_kernel, hid=hid),
        grid=(B, 2, G),
        in_specs=[pl.BlockSpec((1, 1, 1, n_chunks, cd), lambda b, t, g: (b, t, g, 0, 0)),
                  pl.BlockSpec((1, 8, 2 * cd), lambda b, t, g: (t, 0, 0)),
                  pl.BlockSpec((1, cd, 2 * hid), lambda b, t, g: (t, 0, 0)),
                  pl.BlockSpec((1, hid, d), lambda b, t, g: (t, 0, 0)),
                  pl.BlockSpec((n_chunks, LANES), lambda b, t, g: (0, 0)),
                  pl.BlockSpec((n_chunks, LANES), lambda b, t, g: (0, 0))],
        out_specs=pl.BlockSpec((1, 1, 1, n_chunks, d), lambda b, t, g: (b, t, g, 0, 0)),
        out_shape=jax.ShapeDtypeStruct((B, 2, G, n_chunks, d), BF16),
        compiler_params=_params(("parallel", "parallel", "parallel")),
        name="nsa_compress",
    )(zc, pe, w1, w2, cos_c, sin_c)


NSA_TQ = 256
NSA_TK = 512


def _nsa_kernel(qt_ref, g_ref, kc_ref, vct_ref, ks_ref, vst_ref, kw_ref, vwt_ref, o_ref,
                m_s, l_s, acc_s, out_s, *, tq, tk, hpg, dk, n_cmp, n_slc):
    q0 = pl.program_id(2) * tq
    cols = hpg * tq
    i32 = jnp.int32
    qt = jnp.concatenate([qt_ref[0, h * dk:(h + 1) * dk, :] for h in range(hpg)], axis=1)
    gates = g_ref[0, 0]

    def gate_row(branch):
        return jnp.concatenate([gates[3 * h + branch:3 * h + branch + 1, :] for h in range(hpg)], axis=1)

    def qpos_of(n_rows):
        return q0 + lax.rem(lax.broadcasted_iota(i32, (n_rows, cols), 1), tq)

    n_pad = kc_ref.shape[3]
    sc = jnp.dot(kc_ref[0, 0, 0], qt, preferred_element_type=F32)
    n_io = lax.broadcasted_iota(i32, (n_pad, cols), 0)
    vis = (n_io * CMP_STRIDE + (CMP_BLK - 1) <= qpos_of(n_pad)) & (n_io < n_cmp)
    sc = jnp.where(vis, sc, NEG_INF)
    e = jnp.where(vis, jnp.exp(sc - jnp.max(sc, axis=0, keepdims=True)), 0.0)
    p = e / jnp.maximum(jnp.sum(e, axis=0, keepdims=True), 1e-30)
    o_cmp = jnp.dot(vct_ref[0, 0], p.astype(BF16), preferred_element_type=F32)
    out_s[...] = gate_row(0) * o_cmp

    imp = p[:, 0:tq]
    for h in range(1, hpg):
        imp = imp + p[:, h * tq:(h + 1) * tq]
    ratio = SLC_BLK // CMP_STRIDE
    span = CMP_BLK // CMP_STRIDE
    j_p = lax.broadcasted_iota(i32, (n_slc, n_pad), 0)
    c_p = lax.broadcasted_iota(i32, (n_slc, n_pad), 1)
    lo_c = ratio * j_p - (span - 1)
    pool = jnp.where((c_p >= lo_c) & (c_p < lo_c + ratio + span - 1), 1.0, 0.0).astype(BF16)
    hi = imp.astype(BF16)
    r1 = imp - hi.astype(F32)
    mid = r1.astype(BF16)
    lo = (r1 - mid.astype(F32)).astype(BF16)
    p_slc = (jnp.dot(pool, hi, preferred_element_type=F32) + jnp.dot(pool, mid, preferred_element_type=F32)
             + jnp.dot(pool, lo, preferred_element_type=F32))

    j_io = lax.broadcasted_iota(i32, (n_slc, tq), 0)
    qpos_l = q0 + lax.broadcasted_iota(i32, (n_slc, tq), 1)
    cur = qpos_l // SLC_BLK
    forced = (j_io == 0) | (j_io == cur) | (j_io == cur - 1)
    valid = j_io * SLC_BLK <= qpos_l
    score = jnp.where(forced, FORCE_SCORE, jnp.where(valid, p_slc, -FORCE_SCORE))
    bias = jnp.where(_rank_before(score) < N_SEL, 0.0, -MASK_BIG)
    if n_slc < LANES:
        bias = jnp.concatenate([bias, jnp.full((LANES - n_slc, tq), -MASK_BIG, F32)], axis=0)
    bias = bias.astype(BF16)
    qa = jnp.concatenate([qt, jnp.concatenate([bias] * hpg, axis=1)], axis=0)

    qpos_k = qpos_of(tk)
    k_io = lax.broadcasted_iota(i32, (tk, cols), 0)

    _softmax_reset(m_s, l_s, acc_s)

    def slc_tile(j, causal):
        k0 = pl.multiple_of(j * tk, tk)
        kblk = (k0 + lax.broadcasted_iota(i32, (tk, LANES), 0)) // SLC_BLK
        onehot = jnp.where(kblk == lax.broadcasted_iota(i32, (tk, LANES), 1), 1.0, 0.0).astype(BF16)
        ka = jnp.concatenate([ks_ref[0, pl.ds(k0, tk), :], onehot], axis=1)
        s = jnp.dot(ka, qa, preferred_element_type=F32)
        if causal:
            s = jnp.where(k0 + k_io <= qpos_k, s, -MASK_BIG)
        _online_softmax_step_t(s, vst_ref[0, :, pl.ds(k0, tk)], m_s, l_s, acc_s)

    j_diag = q0 // tk

    def body(j, carry):
        slc_tile(j, False)
        return carry

    lax.fori_loop(0, j_diag, body, 0)
    slc_tile(j_diag, True)
    out_s[...] += gate_row(1) * (acc_s[...] / l_s[...])

    _softmax_reset(m_s, l_s, acc_s)

    def win_tile(j):
        k0 = pl.multiple_of(j * tk, tk)
        s = jnp.dot(kw_ref[0, pl.ds(k0, tk), :], qt, preferred_element_type=F32)
        kpos = k0 + k_io
        s = jnp.where((kpos <= qpos_k) & (kpos > qpos_k - WIN), s, -MASK_BIG)
        _online_softmax_step_t(s, vwt_ref[0, :, pl.ds(k0, tk)], m_s, l_s, acc_s)

    @pl.when(j_diag >= 1)
    def _():
        win_tile(j_diag - 1)

    win_tile(j_diag)
    out = out_s[...] + gate_row(2) * (acc_s[...] / l_s[...])
    for h in range(hpg):
        o_ref[0, :, h * dk:(h + 1) * dk] = jnp.transpose(out[:, h * tq:(h + 1) * tq]).astype(o_ref.dtype)


def _nsa_attention(qkv, qt, gates_t, cmp_kv, vct, vt):
    B, S, _ = qkv.shape
    G, hpg, dk = NSA_GROUPS, NSA_HPG, NSA_DK
    n_pad = cmp_kv.shape[3]
    n_cmp = (S - CMP_BLK) // CMP_STRIDE + 1
    tq, tk = min(NSA_TQ, S), min(NSA_TK, S)
    assert tk == WIN and tq <= tk
    cols = hpg * tq
    kv0 = NSA_HEADS

    def k_spec(branch):
        base = kv0 + 2 * G * branch
        return pl.BlockSpec((1, S, dk), lambda b, g, i: (b, 0, base + g))

    def vt_spec(branch):
        return pl.BlockSpec((1, dk, S), lambda b, g, i: (b, (branch - 1) * G + g, 0))

    return pl.pallas_call(
        functools.partial(_nsa_kernel, tq=tq, tk=tk, hpg=hpg, dk=dk, n_cmp=n_cmp, n_slc=S // SLC_BLK),
        grid=(B, G, S // tq),
        in_specs=[pl.BlockSpec((1, hpg * dk, tq), lambda b, g, i: (b, g, i)),
                  pl.BlockSpec((1, 1, LANES, tq), lambda b, g, i: (b, g, 0, i)),
                  pl.BlockSpec((1, 1, 1, n_pad, dk), lambda b, g, i: (b, 0, g, 0, 0)),
                  pl.BlockSpec((1, 1, dk, n_pad), lambda b, g, i: (b, g, 0, 0)),
                  k_spec(1), vt_spec(1), k_spec(2), vt_spec(2)],
        out_specs=pl.BlockSpec((1, tq, hpg * dk), lambda b, g, i: (b, i, g)),
        out_shape=jax.ShapeDtypeStruct((B, S, NSA_HEADS * NSA_DV), BF16),
        scratch_shapes=[pltpu.VMEM((1, cols), F32), pltpu.VMEM((1, cols), F32),
                        pltpu.VMEM((dk, cols), F32), pltpu.VMEM((dk, cols), F32)],
        compiler_params=_params(("parallel", "parallel", "arbitrary")),
        name="nsa_attn",
    )(qt, gates_t, cmp_kv, vct, qkv, vt, qkv, vt)


MOE_TM = 256
MOE_FFN_TM = 256


def _hidden_chunks(f):
    chunks, c0 = [], 0
    while c0 < f:
        cs = min(MXU_WIDTH_V7X, f - c0)
        chunks.append((c0, cs))
        c0 += cs
    return tuple(chunks)


def _swiglu(x, wgu_ref, wd_ref, h_s, chunks):
    for c0, cs in chunks:
        gu = jnp.dot(x, wgu_ref[0, :, 2 * c0:2 * c0 + 2 * cs], preferred_element_type=F32)
        g, u = gu[:, :cs], gu[:, cs:]
        h_s[:, c0:c0 + cs] = (g * jax.nn.sigmoid(g) * u).astype(BF16)
    return jnp.dot(h_s[...], wd_ref[0], preferred_element_type=F32)


def _moe_ffn_kernel(blk_e_ref, n_used_ref, x_ref, wgu_ref, wd_ref, o_ref, h_s, *, chunks):
    del blk_e_ref

    @pl.when(pl.program_id(0) < n_used_ref[0])
    def _():
        o_ref[...] = _swiglu(x_ref[...], wgu_ref, wd_ref, h_s, chunks).astype(o_ref.dtype)


def _cast_kernel(w_ref, o_ref):
    o_ref[...] = w_ref[...].astype(o_ref.dtype)


def _cast_bf16(w):
    E, R, C = w.shape
    return pl.pallas_call(
        _cast_kernel,
        grid=(E,),
        in_specs=[pl.BlockSpec((1, R, C), lambda e: (e, 0, 0))],
        out_specs=pl.BlockSpec((1, R, C), lambda e: (e, 0, 0)),
        out_shape=jax.ShapeDtypeStruct(w.shape, BF16),
        compiler_params=_params(("parallel",)),
        name="cast_bf16",
    )(w)


def _pack_gate_up_kernel(wg_ref, wu_ref, o_ref, *, chunks):
    for c0, cs in chunks:
        o_ref[0, :, 2 * c0:2 * c0 + cs] = wg_ref[0, :, c0:c0 + cs].astype(o_ref.dtype)
        o_ref[0, :, 2 * c0 + cs:2 * c0 + 2 * cs] = wu_ref[0, :, c0:c0 + cs].astype(o_ref.dtype)


PACK_ROWS = 512


def _pack_gate_up(w_gate, w_up):
    E, D, F = w_gate.shape
    rows = min(PACK_ROWS, D)
    spec = pl.BlockSpec((1, rows, F), lambda e, r: (e, r, 0))
    return pl.pallas_call(
        functools.partial(_pack_gate_up_kernel, chunks=_hidden_chunks(F)),
        grid=(E, D // rows),
        in_specs=[spec, spec],
        out_specs=pl.BlockSpec((1, rows, 2 * F), lambda e, r: (e, r, 0)),
        out_shape=jax.ShapeDtypeStruct((E, D, 2 * F), BF16),
        compiler_params=_params(("parallel", "parallel")),
        name="pack_gate_up",
    )(w_gate, w_up)


def _moe_routed(xb, eid_t, rank_t, counts, w_gate_up, w_down):
    T, D = xb.shape
    K = eid_t.shape[0]
    E, F, _ = w_down.shape
    tm = MOE_FFN_TM
    n = T * K
    n_blk = -(-n // tm) + E
    i32 = jnp.int32

    padded = (counts + tm - 1) // tm * tm
    pad_end = jnp.cumsum(padded)
    pad_start = pad_end - padded
    start_t = jnp.zeros_like(eid_t)
    for e in range(E):
        start_t = jnp.where(eid_t == e, pad_start[e], start_t)
    dest = (start_t + rank_t).reshape(n)
    tok = jnp.zeros((n_blk * tm,), i32).at[dest].set(jnp.tile(jnp.arange(T, dtype=i32), K),
                                                     unique_indices=True, mode="promise_in_bounds")
    n_used = (pad_end[-1] // tm).astype(i32).reshape(1)
    blk_start = jnp.arange(n_blk, dtype=i32) * tm
    blk_e = jnp.minimum(jnp.sum((pad_end[None, :] <= blk_start[:, None]).astype(i32), axis=1), E - 1)

    xs = xb.at[tok].get(mode="promise_in_bounds")

    def row_map(i, be, nu):
        return (jnp.minimum(i, nu[0] - 1), 0)

    y = pl.pallas_call(
        functools.partial(_moe_ffn_kernel, chunks=_hidden_chunks(F)),
        grid_spec=pltpu.PrefetchScalarGridSpec(
            num_scalar_prefetch=2,
            grid=(n_blk,),
            in_specs=[pl.BlockSpec((tm, D), row_map),
                      pl.BlockSpec((1, D, 2 * F), lambda i, be, nu: (be[i], 0, 0)),
                      pl.BlockSpec((1, F, D), lambda i, be, nu: (be[i], 0, 0))],
            out_specs=pl.BlockSpec((tm, D), row_map),
            scratch_shapes=[pltpu.VMEM((tm, F), BF16)]),
        out_shape=jax.ShapeDtypeStruct((n_blk * tm, D), BF16),
        compiler_params=_params(("arbitrary",)),
        name="moe_ffn",
    )(blk_e, n_used, xs, w_gate_up, w_down)

    return y.at[dest].get(mode="promise_in_bounds").reshape(K, T, D)


def _layernorm_rows(z, g, b):
    mu = jnp.mean(z, -1, keepdims=True)
    zc = z - mu
    var = jnp.mean(jnp.square(zc), -1, keepdims=True)
    return zc * lax.rsqrt(var + LN_EPS) * g + b


def _merge_kernel(x_ref, yn_ref, ym_ref, wg_ref, wbn_ref, wbm_ref, o_ref):
    x, yn, ym = x_ref[...], yn_ref[...], ym_ref[...]
    d = o_ref.shape[1]
    for c0 in range(0, d, MXU_WIDTH_V7X):
        cs = slice(c0, c0 + MXU_WIDTH_V7X)
        gn = jax.nn.sigmoid(jnp.dot(x, wg_ref[:, cs], preferred_element_type=F32))
        gm = jax.nn.sigmoid(jnp.dot(x, wg_ref[:, d + c0:d + c0 + MXU_WIDTH_V7X], preferred_element_type=F32))
        bn = jnp.dot(yn, wbn_ref[:, cs], preferred_element_type=F32)
        bm = jnp.dot(ym, wbm_ref[:, cs], preferred_element_type=F32)
        o_ref[:, cs] = (gn * bn + gm * bm).astype(o_ref.dtype)


def _merge(xb, y_nsa, y_mla, w_g, wb_nsa, wb_mla):
    T, D = xb.shape
    tm = PROJ_TM
    row = lambda a: pl.BlockSpec((tm, a.shape[1]), lambda i: (i, 0))
    return pl.pallas_call(
        _merge_kernel,
        grid=(T // tm,),
        in_specs=[row(xb), row(y_nsa), row(y_mla), _resident(w_g.shape), _resident(wb_nsa.shape),
                  _resident(wb_mla.shape)],
        out_specs=pl.BlockSpec((tm, D), lambda i: (i, 0)),
        out_shape=jax.ShapeDtypeStruct((T, D), BF16),
        compiler_params=_params(("parallel",)),
        name="merge",
    )(xb, y_nsa, y_mla, w_g, wb_nsa, wb_mla)


def _rank_before(v):
    n = v.shape[0]
    j_io = lax.broadcasted_iota(jnp.int32, v.shape, 0)
    cnt = jnp.zeros(v.shape, jnp.int32)
    for i in range(n):
        row = v[i:i + 1, :]
        cnt = cnt + ((row > v) | ((row == v) & (j_io > i))).astype(jnp.int32)
    return cnt


def _route_tile(logits_t, rb, cnt_s):
    E, tm = logits_t.shape
    per = E // N_EXPERT_GROUPS
    scores = jax.nn.sigmoid(logits_t)
    biased = scores + rb
    grp = []
    for g in range(N_EXPERT_GROUPS):
        blk = biased[g * per:(g + 1) * per, :]
        m1 = jnp.max(blk, axis=0, keepdims=True)
        eq = blk == m1
        n_eq = jnp.sum(eq.astype(F32), axis=0, keepdims=True)
        m2 = jnp.max(jnp.where(eq, NEG_INF, blk), axis=0, keepdims=True)
        grp.append(m1 + jnp.where(n_eq >= 2.0, m1, m2))
    gsel = _rank_before(jnp.concatenate(grp, axis=0)) < TOPK_GROUPS
    masked = jnp.concatenate(
        [jnp.where(gsel[g:g + 1, :], biased[g * per:(g + 1) * per, :], NEG_INF) for g in range(N_EXPERT_GROUPS)],
        axis=0)
    sel = _rank_before(masked) < TOP_K
    w_sel = jnp.where(sel, scores, 0.0)
    w_sel = w_sel / jnp.sum(w_sel, axis=0, keepdims=True) * ROUTED_SCALE
    tri = jnp.where(lax.broadcasted_iota(jnp.int32, (E, E), 1) <= lax.broadcasted_iota(jnp.int32, (E, E), 0),
                    1.0, 0.0).astype(BF16)
    sel_b = jnp.where(sel, 1.0, 0.0).astype(BF16)
    slot = jnp.dot(tri, sel_b, preferred_element_type=F32)
    upper = jnp.where(lax.broadcasted_iota(jnp.int32, (tm, tm), 0) <= lax.broadcasted_iota(jnp.int32, (tm, tm), 1),
                      1.0, 0.0).astype(BF16)
    incl = jnp.dot(sel_b, upper, preferred_element_type=F32)
    pos = cnt_s[...] + incl - 1.0
    cnt_s[...] = cnt_s[...] + incl[:, tm - 1:tm]
    e_io = lax.broadcasted_iota(jnp.int32, (E, tm), 0)
    ids, wts, ranks = [], [], []
    for k in range(TOP_K):
        mk = sel & (slot == float(k + 1))
        ids.append(jnp.sum(jnp.where(mk, e_io, 0), axis=0, keepdims=True))
        wts.append(jnp.sum(jnp.where(mk, w_sel, 0.0), axis=0, keepdims=True))
        ranks.append(jnp.sum(jnp.where(mk, pos, 0.0), axis=0, keepdims=True).astype(jnp.int32))
    pad = 8 - TOP_K
    ids.append(jnp.zeros((pad, tm), jnp.int32))
    wts.append(jnp.zeros((pad, tm), F32))
    ranks.append(jnp.zeros((pad, tm), jnp.int32))
    return jnp.concatenate(ids, axis=0), jnp.concatenate(wts, axis=0), jnp.concatenate(ranks, axis=0)


def _out_ln1_kernel(m_ref, x_ref, wo_ref, g_ref, b_ref, wrt_ref, rb_ref,
                    x1_ref, x1b_ref, eid_ref, wts_ref, rank_ref, cnt_ref, cnt_s):
    @pl.when(pl.program_id(0) == 0)
    def _():
        cnt_s[...] = jnp.zeros(cnt_s.shape, F32)

    m = m_ref[...]
    d = x1_ref.shape[1]
    for c0 in range(0, d, MXU_WIDTH_V7X):
        cs = slice(c0, c0 + MXU_WIDTH_V7X)
        x1_ref[:, cs] = DN_ALPHA * x_ref[:, cs] + jnp.dot(m, wo_ref[:, cs], preferred_element_type=F32)
    x1 = _layernorm_rows(x1_ref[...], g_ref[...], b_ref[...])
    x1_ref[...] = x1
    x1b = x1.astype(BF16)
    x1b_ref[...] = x1b
    logits_t = _dot_nt(wrt_ref[...], x1b)
    eid_ref[...], wts_ref[...], rank_ref[...] = _route_tile(logits_t, rb_ref[...], cnt_s)
    cnt_ref[...] = cnt_s[...]


def _out_ln1(merged, x2, w_out, g, b, w_router_t, router_b):
    T, D = x2.shape
    E = w_router_t.shape[0]
    tm = MOE_TM
    row = lambda w: pl.BlockSpec((tm, w), lambda i: (i, 0))
    col = pl.BlockSpec((8, tm), lambda i: (0, i))
    return pl.pallas_call(
        _out_ln1_kernel,
        grid=(T // tm,),
        in_specs=[row(D), row(D), _resident(w_out.shape), _resident(g.shape), _resident(b.shape),
                  _resident(w_router_t.shape), _resident(router_b.shape)],
        out_specs=[row(D), row(D), col, col, col, pl.BlockSpec((E, 1), lambda i: (0, 0))],
        out_shape=[jax.ShapeDtypeStruct((T, D), F32), jax.ShapeDtypeStruct((T, D), BF16),
                   jax.ShapeDtypeStruct((8, T), jnp.int32), jax.ShapeDtypeStruct((8, T), F32),
                   jax.ShapeDtypeStruct((8, T), jnp.int32), jax.ShapeDtypeStruct((E, 1), F32)],
        scratch_shapes=[pltpu.VMEM((E, 1), F32)],
        compiler_params=_params(("arbitrary",)),
        name="out_ln1",
    )(merged, x2, w_out, g, b, w_router_t, router_b)


def _ffn_ln2_kernel(x1_ref, yk_ref, w_ref, sgu_ref, sd_ref, g_ref, b_ref, x2_ref, x2b_ref, h_s, *, chunks):
    x1 = x1_ref[...]
    acc = _swiglu(x1.astype(BF16), sgu_ref, sd_ref, h_s, chunks)
    w = w_ref[...]
    for k in range(yk_ref.shape[0]):
        acc = acc + yk_ref[k].astype(F32) * w[:, k:k + 1]
    x2 = _layernorm_rows(DN_ALPHA * x1 + acc, g_ref[...], b_ref[...])
    x2_ref[...] = x2
    x2b_ref[...] = x2.astype(BF16)


FFN_LN2_VMEM_LIMIT_BYTES = 56 * 1024 * 1024


def _ffn_ln2(x1, yk, wts, sh_gate_up, sh_down, g, b):
    T, D = x1.shape
    K = yk.shape[0]
    tm = MOE_TM
    F = sh_down.shape[1]
    row = lambda w: pl.BlockSpec((tm, w), lambda i: (i, 0))
    return pl.pallas_call(
        functools.partial(_ffn_ln2_kernel, chunks=_hidden_chunks(F)),
        grid=(T // tm,),
        in_specs=[row(D), pl.BlockSpec((K, tm, D), lambda i: (0, i, 0)), row(wts.shape[1]),
                  _resident(sh_gate_up.shape), _resident(sh_down.shape), _resident(g.shape), _resident(b.shape)],
        out_specs=[row(D), row(D)],
        out_shape=[jax.ShapeDtypeStruct((T, D), F32), jax.ShapeDtypeStruct((T, D), BF16)],
        scratch_shapes=[pltpu.VMEM((tm, F), BF16)],
        compiler_params=pltpu.CompilerParams(dimension_semantics=("parallel",),
                                             vmem_limit_bytes=FFN_LN2_VMEM_LIMIT_BYTES),
        name="ffn_ln2",
    )(x1, yk, wts, sh_gate_up, sh_down, g, b)


def _ple_kernel(x2_ref, x2b_ref, p_ref, wpg_ref, wpp_ref, o_ref):
    x2b = x2b_ref[...]
    pb = p_ref[...].astype(BF16)
    d = o_ref.shape[1]
    for c0 in range(0, d, MXU_WIDTH_V7X):
        cs = slice(c0, c0 + MXU_WIDTH_V7X)
        gate = jax.nn.sigmoid(jnp.dot(x2b, wpg_ref[:, cs], preferred_element_type=F32))
        pp = jnp.dot(pb, wpp_ref[:, cs], preferred_element_type=F32)
        o_ref[:, cs] = x2_ref[:, cs] + gate * pp


def _ple(x2, x2b, p2, w_pg, w_pp):
    T, D = x2.shape
    tm = PROJ_TM
    row = lambda w: pl.BlockSpec((tm, w), lambda i: (i, 0))
    return pl.pallas_call(
        _ple_kernel,
        grid=(T // tm,),
        in_specs=[row(D), row(D), row(p2.shape[1]), _resident(w_pg.shape), _resident(w_pp.shape)],
        out_specs=row(D),
        out_shape=jax.ShapeDtypeStruct((T, D), F32),
        compiler_params=_params(("parallel",)),
        name="ple",
    )(x2, x2b, p2, w_pg, w_pp)


def _attention_branches(xb, S, w_in, pe_k, pe_v, wk1, wk2, wv1, wv2, q_norm, w_uq, kv_norm, w_uk, w_uv):
    T, D = xb.shape
    B = T // S
    c0 = W_NSA_Q
    c1 = c0 + W_NSA_KV
    c2 = c1 + W_NSA_GATE
    c3 = c2 + W_MLA_CQ
    c4 = c3 + W_MLA_CKV
    c5 = c4 + W_MLA_KR
    G, dk = NSA_GROUPS, NSA_DK
    pos = jnp.arange(S)
    cos128, sin128 = _rope_tables_128(pos)
    cos64, slo64, shi64 = _rope_tables_64(pos)

    qkv = _nsa_proj(xb, w_in[:, :c1].astype(BF16), cos128, sin128, S)

    pad_kr = jnp.zeros((D, LANES - W_MLA_KR), w_in.dtype)
    pad_g = jnp.zeros((D, LANES - W_NSA_GATE), w_in.dtype)
    w_m = jnp.concatenate([w_in[:, c2:c4], w_in[:, c4:c5], pad_kr, w_in[:, c1:c2], pad_g], axis=1).astype(BF16)
    H = MLA_HEADS
    wq = w_uq.reshape(MLA_Q_RANK, H, MLA_NOPE + MLA_ROPE)
    wq = jnp.concatenate([wq, jnp.zeros((MLA_Q_RANK, H, 2 * LANES - MLA_NOPE - MLA_ROPE), wq.dtype)], axis=-1)
    wq = wq.reshape(MLA_Q_RANK, H * 2 * LANES).astype(BF16)
    wukv = jnp.concatenate([w_uk, w_uv], axis=1).astype(BF16)
    q_m, kv_m, kr_m, gates = _mla_in(xb, w_m, q_norm.reshape(1, -1), kv_norm.reshape(1, -1), wq, wukv,
                                     cos64, slo64, shi64, S)

    n_chunks = S // CMP_STRIDE
    kvc = qkv[:, NSA_HEADS * dk:(NSA_HEADS + 2 * G) * dk]
    zc = kvc.reshape(B, n_chunks, CMP_STRIDE, 2, G, dk).transpose(0, 3, 4, 1, 2, 5)
    zc = zc.reshape(B, 2, G, n_chunks, CMP_STRIDE * dk)
    pe = jnp.stack([pe_k.reshape(-1), pe_v.reshape(-1)])
    pe = jnp.broadcast_to(pe[:, None, :], (2, 8, pe.shape[-1])).astype(BF16)
    half = CMP_STRIDE * dk
    w1 = jnp.stack([jnp.concatenate([wk1[:half], wk1[half:]], axis=1),
                    jnp.concatenate([wv1[:half], wv1[half:]], axis=1)]).astype(BF16)
    w2 = jnp.stack([wk2, wv2]).astype(BF16)
    pos_c = jnp.arange(n_chunks) * CMP_STRIDE + (CMP_BLK - 1)
    cos_c, sin_c = _rope_tables_128(pos_c)
    cmp_kv = _nsa_compress(zc, pe, w1, w2, cos_c, sin_c)

    g3 = gates[:, :W_NSA_GATE].reshape(B, S, G, NSA_HPG * 3).transpose(0, 2, 3, 1)
    g3 = jnp.pad(g3, ((0, 0), (0, 0), (0, LANES - NSA_HPG * 3), (0, 0)))

    qkv3 = qkv.reshape(B, S, -1)
    qt = jnp.swapaxes(qkv3[:, :, :NSA_HEADS * dk], 1, 2)
    v0 = NSA_HEADS + 2 * G
    v_sw = jnp.concatenate([qkv3[:, :, (v0 + G) * dk:(v0 + 2 * G) * dk],
                            qkv3[:, :, (v0 + 3 * G) * dk:(v0 + 4 * G) * dk]], axis=-1)
    vt = jnp.swapaxes(v_sw, 1, 2)
    vct = jnp.swapaxes(cmp_kv[:, 1], 2, 3)
    y_nsa = _nsa_attention(qkv3, qt, g3, cmp_kv, vct, vt)
    kv_m = kv_m.reshape(B, S, -1)
    qt_m = jnp.swapaxes(q_m.reshape(B, S, -1), 1, 2)
    vt_m = jnp.swapaxes(kv_m[:, :, MLA_HEADS * MLA_NOPE:], 1, 2)
    y_mla = _mla_attention(qt_m, kv_m, kr_m.reshape(B, S, -1), vt_m)
    return y_nsa, y_mla


def kernel(x, p, w_in, nsa_pe_k, nsa_pe_v, nsa_cmp_k_w1, nsa_cmp_k_w2, nsa_cmp_v_w1, nsa_cmp_v_w2, mla_q_norm, mla_w_uq, mla_kv_norm, mla_w_uk, mla_w_uv, w_branch_nsa, w_branch_mla, w_out, ln1_g, ln1_b, router_w, router_b, exp_w_gate, exp_w_up, exp_w_down, sh_w_gate, sh_w_up, sh_w_down, ln2_g, ln2_b, ple_w_proj, ple_w_gate):
    B, S, D = x.shape
    T = B * S
    c5 = D_IN - W_MERGE
    for i in range(DEPTH):
        x2 = x.reshape(T, D)
        xb = x2.astype(BF16)
        y_nsa, y_mla = _attention_branches(
            xb, S, w_in[i], nsa_pe_k[i], nsa_pe_v[i], nsa_cmp_k_w1[i], nsa_cmp_k_w2[i],
            nsa_cmp_v_w1[i], nsa_cmp_v_w2[i], mla_q_norm[i], mla_w_uq[i], mla_kv_norm[i],
            mla_w_uk[i], mla_w_uv[i])

        merged = _merge(xb, y_nsa.reshape(T, -1), y_mla.reshape(T, -1), w_in[i][:, c5:].astype(BF16),
                        w_branch_nsa[i].astype(BF16), w_branch_mla[i].astype(BF16))
        x1, x1b, eid_t, wts_t, rank_t, counts = _out_ln1(
            merged, x2, w_out[i].astype(BF16), ln1_g[i].reshape(1, D), ln1_b[i].reshape(1, D),
            router_w[i].T.astype(BF16), router_b[i].reshape(N_EXPERTS, 1).astype(F32))

        yk = _moe_routed(x1b, eid_t[:TOP_K], rank_t[:TOP_K], counts.reshape(N_EXPERTS).astype(jnp.int32),
                         _pack_gate_up(exp_w_gate[i], exp_w_up[i]), _cast_bf16(exp_w_down[i]))
        x2n, x2b = _ffn_ln2(x1, yk, wts_t.T, _pack_gate_up(sh_w_gate[i][None], sh_w_up[i][None]),
                            sh_w_down[i][None].astype(BF16), ln2_g[i].reshape(1, D), ln2_b[i].reshape(1, D))
        out = _ple(x2n, x2b, p[i].reshape(T, PLE_DIM), ple_w_gate[i].astype(BF16), ple_w_proj[i].astype(BF16))
        x = out.reshape(B, S, D)
    return x
```

```python
import functools
import math

import jax
import jax.numpy as jnp
from jax import lax
from jax.experimental import pallas as pl
from jax.experimental.pallas import tpu as pltpu

D_MODEL = 2048
BATCH = 8
SEQ = 4096
DEPTH = 1

PLE_DIM = 256
ROPE_THETA = 10000.0
LN_EPS = 1e-5
RMS_EPS = 1e-6
NEG_INF = -1e30

NSA_HEADS = 8
NSA_GROUPS = 2
NSA_HPG = NSA_HEADS // NSA_GROUPS
NSA_DK = 128
NSA_DV = 128
CMP_BLK = 32
CMP_STRIDE = 16
CMP_HID = 256
SLC_BLK = 64
N_SEL = 16
WIN = 512
FORCE_SCORE = 1e4

MLA_HEADS = 8
MLA_Q_RANK = 768
MLA_KV_RANK = 512
MLA_NOPE = 128
MLA_ROPE = 64
MLA_DV = 128

N_EXPERTS = 64
TOP_K = 6
N_EXPERT_GROUPS = 8
TOPK_GROUPS = 4
D_EXPERT = 1408
ROUTED_SCALE = 2.5

W_NSA_Q = NSA_HEADS * NSA_DK
W_NSA_KV = 3 * NSA_GROUPS * (NSA_DK + NSA_DV)
W_NSA_GATE = 3 * NSA_HEADS
W_MLA_CQ = MLA_Q_RANK
W_MLA_CKV = MLA_KV_RANK
W_MLA_KR = MLA_ROPE
W_MERGE = 2 * D_MODEL
D_IN = W_NSA_Q + W_NSA_KV + W_NSA_GATE + W_MLA_CQ + W_MLA_CKV + W_MLA_KR + W_MERGE

DN_ALPHA = (2.0 * DEPTH) ** 0.25
DN_BETA = (8.0 * DEPTH) ** -0.25

V7X_VMEM_LIMIT_BYTES = 48 * 1024 * 1024
LANES = 128
MXU_WIDTH_V7X = 256
PROJ_TM = 512

BF16 = jnp.bfloat16
F32 = jnp.float32


def _params(sem):
    return pltpu.CompilerParams(dimension_semantics=sem, vmem_limit_bytes=V7X_VMEM_LIMIT_BYTES)


def _resident(shape):
    return pl.BlockSpec(shape, lambda *_: (0,) * len(shape), pipeline_mode=pl.Buffered(1))


def _rope_tables_128(pos):
    half = LANES // 2
    inv = jnp.power(ROPE_THETA, -jnp.arange(half, dtype=F32) * 2.0 / LANES)
    ang = pos.astype(F32)[:, None] * inv[None, :]
    cos, sin = jnp.cos(ang), jnp.sin(ang)
    return jnp.concatenate([cos, cos], -1), jnp.concatenate([-sin, sin], -1)


def _rope_tables_64(pos):
    half = MLA_ROPE // 2
    inv = jnp.power(ROPE_THETA, -jnp.arange(half, dtype=F32) * 2.0 / MLA_ROPE)
    ang = pos.astype(F32)[:, None] * inv[None, :]
    cos, sin = jnp.cos(ang), jnp.sin(ang)
    zero = jnp.zeros_like(sin)
    return (jnp.concatenate([cos, cos, cos, cos], -1),
            jnp.concatenate([-sin, zero, -sin, zero], -1),
            jnp.concatenate([zero, sin, zero, sin], -1))


def _rope128(z, cos, sin):
    return z * cos + pltpu.roll(z, LANES // 2, 1) * sin


def _rope64(z, cos, sin_lo, sin_hi):
    return z * cos + pltpu.roll(z, LANES - MLA_ROPE // 2, 1) * sin_lo + pltpu.roll(z, MLA_ROPE // 2, 1) * sin_hi


def _nsa_proj_kernel(x_ref, w_ref, cos_ref, sin_ref, o_ref, *, rope_heads, n_q_heads, scale):
    x = x_ref[...]
    cos, sin = cos_ref[...], sin_ref[...]
    n_heads = w_ref.shape[1] // LANES
    for t in range(n_heads // 2):
        c0 = t * MXU_WIDTH_V7X
        z = jnp.dot(x, w_ref[:, c0:c0 + MXU_WIDTH_V7X], preferred_element_type=F32)
        for hh in range(2):
            head = 2 * t + hh
            zh = z[:, hh * LANES:(hh + 1) * LANES]
            if head in rope_heads:
                zh = _rope128(zh, cos, sin)
            if head < n_q_heads:
                zh = zh * scale
            o_ref[:, head * LANES:(head + 1) * LANES] = zh.astype(o_ref.dtype)


def _nsa_proj(xb, w, cos, sin, S):
    T, D = xb.shape
    n = w.shape[1]
    tm = min(PROJ_TM, S)
    n_pos = S // tm
    kv0 = NSA_HEADS
    per = NSA_GROUPS * 2
    rope_heads = tuple(range(NSA_HEADS)) + tuple(kv0 + per * br + g for br in (1, 2) for g in range(NSA_GROUPS))
    return pl.pallas_call(
        functools.partial(_nsa_proj_kernel, rope_heads=rope_heads, n_q_heads=NSA_HEADS, scale=NSA_DK ** -0.5),
        grid=(T // tm,),
        in_specs=[pl.BlockSpec((tm, D), lambda i: (i, 0)),
                  _resident((D, n)),
                  pl.BlockSpec((tm, LANES), lambda i: (i % n_pos, 0)),
                  pl.BlockSpec((tm, LANES), lambda i: (i % n_pos, 0))],
        out_specs=pl.BlockSpec((tm, n), lambda i: (i, 0)),
        out_shape=jax.ShapeDtypeStruct((T, n), BF16),
        compiler_params=_params(("parallel",)),
        name="nsa_proj",
    )(xb, w, cos, sin)


def _rmsnorm_rows(z, g):
    return z * lax.rsqrt(jnp.mean(jnp.square(z), -1, keepdims=True) + RMS_EPS) * g


def _mla_in_kernel(x_ref, w_ref, gq_ref, gkv_ref, wuq_ref, wukv_ref, cos_ref, slo_ref, shi_ref,
                   q_ref, kv_ref, kr_ref, gate_ref, *, scale):
    x = x_ref[...]
    cos, slo, shi = cos_ref[...], slo_ref[...], shi_ref[...]
    c1 = MLA_Q_RANK
    c2 = c1 + MLA_KV_RANK
    cq = jnp.dot(x, w_ref[:, :c1], preferred_element_type=F32)
    cqn = _rmsnorm_rows(cq, gq_ref[...]).astype(BF16)
    ckv = jnp.dot(x, w_ref[:, c1:c2], preferred_element_type=F32)
    ckvn = _rmsnorm_rows(ckv, gkv_ref[...]).astype(BF16)
    tail = jnp.dot(x, w_ref[:, c2:c2 + 2 * LANES], preferred_element_type=F32)
    kr_ref[...] = _rope64(tail[:, :LANES], cos, slo, shi).astype(kr_ref.dtype)
    gate_ref[...] = jax.nn.sigmoid(tail[:, LANES:])
    for h in range(MLA_HEADS):
        c0 = h * MXU_WIDTH_V7X
        z = jnp.dot(cqn, wuq_ref[:, c0:c0 + MXU_WIDTH_V7X], preferred_element_type=F32)
        q_ref[:, c0:c0 + LANES] = (z[:, :LANES] * scale).astype(q_ref.dtype)
        q_ref[:, c0 + LANES:c0 + 2 * LANES] = (_rope64(z[:, LANES:], cos, slo, shi) * scale).astype(q_ref.dtype)
    n_kv = wukv_ref.shape[1]
    for c0 in range(0, n_kv, MXU_WIDTH_V7X):
        kv_ref[:, c0:c0 + MXU_WIDTH_V7X] = jnp.dot(
            ckvn, wukv_ref[:, c0:c0 + MXU_WIDTH_V7X], preferred_element_type=F32).astype(kv_ref.dtype)


def _mla_in(xb, w, gq, gkv, wuq, wukv, cos, slo, shi, S):
    T, D = xb.shape
    tm = min(PROJ_TM, S)
    n_pos = S // tm
    tbl = pl.BlockSpec((tm, LANES), lambda i: (i % n_pos, 0))
    nq, nkv = wuq.shape[1], wukv.shape[1]
    return pl.pallas_call(
        functools.partial(_mla_in_kernel, scale=(MLA_NOPE + MLA_ROPE) ** -0.5),
        grid=(T // tm,),
        in_specs=[pl.BlockSpec((tm, D), lambda i: (i, 0)),
                  _resident(w.shape), _resident(gq.shape), _resident(gkv.shape),
                  _resident(wuq.shape), _resident(wukv.shape), tbl, tbl, tbl],
        out_specs=[pl.BlockSpec((tm, nq), lambda i: (i, 0)),
                   pl.BlockSpec((tm, nkv), lambda i: (i, 0)),
                   pl.BlockSpec((tm, LANES), lambda i: (i, 0)),
                   pl.BlockSpec((tm, LANES), lambda i: (i, 0))],
        out_shape=[jax.ShapeDtypeStruct((T, nq), BF16), jax.ShapeDtypeStruct((T, nkv), BF16),
                   jax.ShapeDtypeStruct((T, LANES), BF16), jax.ShapeDtypeStruct((T, LANES), F32)],
        compiler_params=_params(("parallel",)),
        name="mla_in",
    )(xb, w, gq, gkv, wuq, wukv, cos, slo, shi)


MASK_BIG = 16384.0


def _softmax_reset(m_s, l_s, acc_s):
    m_s[...] = jnp.full(m_s.shape, NEG_INF, F32)
    l_s[...] = jnp.zeros(l_s.shape, F32)
    acc_s[...] = jnp.zeros(acc_s.shape, F32)


def _online_softmax_step(s, v, m_s, l_s, acc_s):
    m_prev = m_s[...]
    m_new = jnp.maximum(m_prev, jnp.max(s, axis=-1, keepdims=True))
    alpha = jnp.exp(m_prev - m_new)
    p = jnp.exp(s - m_new)
    l_s[...] = alpha * l_s[...] + jnp.sum(p, axis=-1, keepdims=True)
    acc_s[...] = alpha * acc_s[...] + jnp.dot(p.astype(BF16), v, preferred_element_type=F32)
    m_s[...] = m_new


def _online_softmax_step_t(st, vt, m_s, l_s, acc_s):
    m_prev = m_s[...]
    m_new = jnp.maximum(m_prev, jnp.max(st, axis=0, keepdims=True))
    alpha = jnp.exp(m_prev - m_new)
    p = jnp.exp(st - m_new)
    l_s[...] = alpha * l_s[...] + jnp.sum(p, axis=0, keepdims=True)
    acc_s[...] = alpha * acc_s[...] + jnp.dot(vt, p.astype(BF16), preferred_element_type=F32)
    m_s[...] = m_new


def _dot_nt(a, b):
    return lax.dot_general(a, b, (((1,), (1,)), ((), ())), preferred_element_type=F32)


MLA_TQ = 512
MLA_TK = 512


MLA_HPS = 2


def _mla_attn_kernel(qt_ref, kn_ref, kr_ref, vt_ref, o_ref, m_s, l_s, acc_s, *, tq, tk, hps):
    q0 = pl.program_id(2) * tq
    qts = [qt_ref[0, hh * 2 * LANES:(hh + 1) * 2 * LANES, :] for hh in range(hps)]
    _softmax_reset(m_s, l_s, acc_s)

    def tile(j, causal):
        k0 = pl.multiple_of(j * tk, tk)
        kr = kr_ref[0, pl.ds(k0, tk), :]
        for hh in range(hps):
            hs = slice(hh * LANES, (hh + 1) * LANES)
            ka = jnp.concatenate([kn_ref[0, pl.ds(k0, tk), hs], kr], axis=1)
            st = jnp.dot(ka, qts[hh], preferred_element_type=F32)
            if causal:
                kpos = k0 + lax.broadcasted_iota(jnp.int32, (tk, tq), 0)
                qpos = q0 + lax.broadcasted_iota(jnp.int32, (tk, tq), 1)
                st = jnp.where(kpos <= qpos, st, -MASK_BIG)
            _online_softmax_step_t(st, vt_ref[0, hs, pl.ds(k0, tk)], m_s.at[hh], l_s.at[hh], acc_s.at[hh])

    j_diag = q0 // tk

    def body(j, carry):
        tile(j, False)
        return carry

    lax.fori_loop(0, j_diag, body, 0)
    tile(j_diag, True)
    for hh in range(hps):
        o_ref[0, :, hh * LANES:(hh + 1) * LANES] = jnp.transpose(acc_s[hh] / l_s[hh]).astype(o_ref.dtype)


def _mla_attention(qt, kv, kr, vt):
    B, _, S = qt.shape
    H = MLA_HEADS
    hps = MLA_HPS
    tq = tk = min(MLA_TQ, S)
    return pl.pallas_call(
        functools.partial(_mla_attn_kernel, tq=tq, tk=tk, hps=hps),
        grid=(B, H // hps, S // tq),
        in_specs=[pl.BlockSpec((1, hps * 2 * LANES, tq), lambda b, h, i: (b, h, i)),
                  pl.BlockSpec((1, S, hps * LANES), lambda b, h, i: (b, 0, h)),
                  pl.BlockSpec((1, S, LANES), lambda b, h, i: (b, 0, 0)),
                  pl.BlockSpec((1, hps * LANES, S), lambda b, h, i: (b, h, 0))],
        out_specs=pl.BlockSpec((1, tq, hps * LANES), lambda b, h, i: (b, i, h)),
        out_shape=jax.ShapeDtypeStruct((B, S, H * MLA_DV), BF16),
        scratch_shapes=[pltpu.VMEM((hps, 1, tq), F32), pltpu.VMEM((hps, 1, tq), F32),
                        pltpu.VMEM((hps, MLA_DV, tq), F32)],
        compiler_params=_params(("parallel", "parallel", "arbitrary")),
        name="mla_attn",
    )(qt, kv, kr, vt)


def _gelu_tanh(x):
    return 0.5 * x * (1.0 + jnp.tanh(math.sqrt(2.0 / math.pi) * (x + 0.044715 * (x * x * x))))


def _compress_kernel(z_ref, pe_ref, w1_ref, w2_ref, cos_ref, sin_ref, o_ref, *, hid):
    which = pl.program_id(1)
    z = z_ref[0, 0, 0]
    n_chunks = z.shape[0]
    cd = z.shape[1]
    ab = jnp.dot(z, w1_ref[0], preferred_element_type=F32)
    a_part = ab[:, :hid]
    b_next = pltpu.roll(ab[:, hid:], n_chunks - 1, 0)
    pe_a = jnp.dot(pe_ref[0, :, :cd], w1_ref[0, :, :hid], preferred_element_type=F32)
    pe_b = jnp.dot(pe_ref[0, :, cd:], w1_ref[0, :, hid:], preferred_element_type=F32)
    h = _gelu_tanh(a_part + b_next + pe_a[0:1] + pe_b[0:1])
    out = jnp.dot(h.astype(BF16), w2_ref[0], preferred_element_type=F32)
    roped = _rope128(out, cos_ref[...], sin_ref[...])
    o_ref[0, 0, 0] = jnp.where(which == 0, roped, out).astype(o_ref.dtype)


def _nsa_compress(zc, pe, w1, w2, cos_c, sin_c):
    B, _, G, n_chunks, cd = zc.shape
    hid = w2.shape[1]
    d = w2.shape[2]
    return pl.pallas_call(
        functools.partial(_compress_kernel, hid=hid),
        grid=(B, 2, G),
        in_specs=[pl.BlockSpec((1, 1, 1, n_chunks, cd), lambda b, t, g: (b, t, g, 0, 0)),
                  pl.BlockSpec((1, 8, 2 * cd), lambda b, t, g: (t, 0, 0)),
                  pl.BlockSpec((1, cd, 2 * hid), lambda b, t, g: (t, 0, 0)),
                  pl.BlockSpec((1, hid, d), lambda b, t, g: (t, 0, 0)),
                  pl.BlockSpec((n_chunks, LANES), lambda b, t, g: (0, 0)),
                  pl.BlockSpec((n_chunks, LANES), lambda b, t, g: (0, 0))],
        out_specs=pl.BlockSpec((1, 1, 1, n_chunks, d), lambda b, t, g: (b, t, g, 0, 0)),
        out_shape=jax.ShapeDtypeStruct((B, 2, G, n_chunks, d), BF16),
        compiler_params=_params(("parallel", "parallel", "parallel")),
        name="nsa_compress",
    )(zc, pe, w1, w2, cos_c, sin_c)


NSA_TQ = 256
NSA_TK = 512


def _nsa_kernel(qt_ref, g_ref, kc_ref, vct_ref, ks_ref, vst_ref, kw_ref, vwt_ref, o_ref,
                m_s, l_s, acc_s, out_s, *, tq, tk, hpg, dk, n_cmp, n_slc):
    q0 = pl.program_id(2) * tq
    cols = hpg * tq
    i32 = jnp.int32
    qt = jnp.concatenate([qt_ref[0, h * dk:(h + 1) * dk, :] for h in range(hpg)], axis=1)
    gates = g_ref[0, 0]

    def gate_row(branch):
        return jnp.concatenate([gates[3 * h + branch:3 * h + branch + 1, :] for h in range(hpg)], axis=1)

    def qpos_of(n_rows):
        return q0 + lax.rem(lax.broadcasted_iota(i32, (n_rows, cols), 1), tq)

    n_pad = kc_ref.shape[3]
    sc = jnp.dot(kc_ref[0, 0, 0], qt, preferred_element_type=F32)
    n_io = lax.broadcasted_iota(i32, (n_pad, cols), 0)
    vis = (n_io * CMP_STRIDE + (CMP_BLK - 1) <= qpos_of(n_pad)) & (n_io < n_cmp)
    sc = jnp.where(vis, sc, NEG_INF)
    e = jnp.where(vis, jnp.exp(sc - jnp.max(sc, axis=0, keepdims=True)), 0.0)
    p = e / jnp.maximum(jnp.sum(e, axis=0, keepdims=True), 1e-30)
    o_cmp = jnp.dot(vct_ref[0, 0], p.astype(BF16), preferred_element_type=F32)
    out_s[...] = gate_row(0) * o_cmp

    imp = p[:, 0:tq]
    for h in range(1, hpg):
        imp = imp + p[:, h * tq:(h + 1) * tq]
    ratio = SLC_BLK // CMP_STRIDE
    span = CMP_BLK // CMP_STRIDE
    j_p = lax.broadcasted_iota(i32, (n_slc, n_pad), 0)
    c_p = lax.broadcasted_iota(i32, (n_slc, n_pad), 1)
    lo_c = ratio * j_p - (span - 1)
    pool = jnp.where((c_p >= lo_c) & (c_p < lo_c + ratio + span - 1), 1.0, 0.0).astype(BF16)
    hi = imp.astype(BF16)
    r1 = imp - hi.astype(F32)
    mid = r1.astype(BF16)
    lo = (r1 - mid.astype(F32)).astype(BF16)
    p_slc = (jnp.dot(pool, hi, preferred_element_type=F32) + jnp.dot(pool, mid, preferred_element_type=F32)
             + jnp.dot(pool, lo, preferred_element_type=F32))

    j_io = lax.broadcasted_iota(i32, (n_slc, tq), 0)
    qpos_l = q0 + lax.broadcasted_iota(i32, (n_slc, tq), 1)
    cur = qpos_l // SLC_BLK
    forced = (j_io == 0) | (j_io == cur) | (j_io == cur - 1)
    valid = j_io * SLC_BLK <= qpos_l
    score = jnp.where(forced, FORCE_SCORE, jnp.where(valid, p_slc, -FORCE_SCORE))
    bias = jnp.where(_rank_before(score) < N_SEL, 0.0, -MASK_BIG)
    if n_slc < LANES:
        bias = jnp.concatenate([bias, jnp.full((LANES - n_slc, tq), -MASK_BIG, F32)], axis=0)
    bias = bias.astype(BF16)
    qa = jnp.concatenate([qt, jnp.concatenate([bias] * hpg, axis=1)], axis=0)

    qpos_k = qpos_of(tk)
    k_io = lax.broadcasted_iota(i32, (tk, cols), 0)

    _softmax_reset(m_s, l_s, acc_s)

    def slc_tile(j, causal):
        k0 = pl.multiple_of(j * tk, tk)
        kblk = (k0 + lax.broadcasted_iota(i32, (tk, LANES), 0)) // SLC_BLK
        onehot = jnp.where(kblk == lax.broadcasted_iota(i32, (tk, LANES), 1), 1.0, 0.0).astype(BF16)
        ka = jnp.concatenate([ks_ref[0, pl.ds(k0, tk), :], onehot], axis=1)
        s = jnp.dot(ka, qa, preferred_element_type=F32)
        if causal:
            s = jnp.where(k0 + k_io <= qpos_k, s, -MASK_BIG)
        _online_softmax_step_t(s, vst_ref[0, :, pl.ds(k0, tk)], m_s, l_s, acc_s)

    j_diag = q0 // tk

    def body(j, carry):
        slc_tile(j, False)
        return carry

    lax.fori_loop(0, j_diag, body, 0)
    slc_tile(j_diag, True)
    out_s[...] += gate_row(1) * (acc_s[...] / l_s[...])

    _softmax_reset(m_s, l_s, acc_s)

    def win_tile(j):
        k0 = pl.multiple_of(j * tk, tk)
        s = jnp.dot(kw_ref[0, pl.ds(k0, tk), :], qt, preferred_element_type=F32)
        kpos = k0 + k_io
        s = jnp.where((kpos <= qpos_k) & (kpos > qpos_k - WIN), s, -MASK_BIG)
        _online_softmax_step_t(s, vwt_ref[0, :, pl.ds(k0, tk)], m_s, l_s, acc_s)

    @pl.when(j_diag >= 1)
    def _():
        win_tile(j_diag - 1)

    win_tile(j_diag)
    out = out_s[...] + gate_row(2) * (acc_s[...] / l_s[...])
    for h in range(hpg):
        o_ref[0, :, h * dk:(h + 1) * dk] = jnp.transpose(out[:, h * tq:(h + 1) * tq]).astype(o_ref.dtype)


def _nsa_attention(qkv, qt, gates_t, cmp_kv, vct, vt):
    B, S, _ = qkv.shape
    G, hpg, dk = NSA_GROUPS, NSA_HPG, NSA_DK
    n_pad = cmp_kv.shape[3]
    n_cmp = (S - CMP_BLK) // CMP_STRIDE + 1
    tq, tk = min(NSA_TQ, S), min(NSA_TK, S)
    assert tk == WIN and tq <= tk
    cols = hpg * tq
    kv0 = NSA_HEADS

    def k_spec(branch):
        base = kv0 + 2 * G * branch
        return pl.BlockSpec((1, S, dk), lambda b, g, i: (b, 0, base + g))

    def vt_spec(branch):
        return pl.BlockSpec((1, dk, S), lambda b, g, i: (b, (branch - 1) * G + g, 0))

    return pl.pallas_call(
        functools.partial(_nsa_kernel, tq=tq, tk=tk, hpg=hpg, dk=dk, n_cmp=n_cmp, n_slc=S // SLC_BLK),
        grid=(B, G, S // tq),
        in_specs=[pl.BlockSpec((1, hpg * dk, tq), lambda b, g, i: (b, g, i)),
                  pl.BlockSpec((1, 1, LANES, tq), lambda b, g, i: (b, g, 0, i)),
                  pl.BlockSpec((1, 1, 1, n_pad, dk), lambda b, g, i: (b, 0, g, 0, 0)),
                  pl.BlockSpec((1, 1, dk, n_pad), lambda b, g, i: (b, g, 0, 0)),
                  k_spec(1), vt_spec(1), k_spec(2), vt_spec(2)],
        out_specs=pl.BlockSpec((1, tq, hpg * dk), lambda b, g, i: (b, i, g)),
        out_shape=jax.ShapeDtypeStruct((B, S, NSA_HEADS * NSA_DV), BF16),
        scratch_shapes=[pltpu.VMEM((1, cols), F32), pltpu.VMEM((1, cols), F32),
                        pltpu.VMEM((dk, cols), F32), pltpu.VMEM((dk, cols), F32)],
        compiler_params=_params(("parallel", "parallel", "arbitrary")),
        name="nsa_attn",
    )(qt, gates_t, cmp_kv, vct, qkv, vt, qkv, vt)


MOE_TM = 256
MOE_FFN_TM = 256


def _hidden_chunks(f):
    chunks, c0 = [], 0
    while c0 < f:
        cs = min(MXU_WIDTH_V7X, f - c0)
        chunks.append((c0, cs))
        c0 += cs
    return tuple(chunks)


def _swiglu(x, wgu_ref, wd_ref, h_s, chunks):
    for c0, cs in chunks:
        gu = jnp.dot(x, wgu_ref[0, :, 2 * c0:2 * c0 + 2 * cs], preferred_element_type=F32)
        g, u = gu[:, :cs], gu[:, cs:]
        h_s[:, c0:c0 + cs] = (g * jax.nn.sigmoid(g) * u).astype(BF16)
    return jnp.dot(h_s[...], wd_ref[0], preferred_element_type=F32)


W_CHUNK_ROWS = 128
W_STAGE_SLOTS = 4
MOE_FFN_VMEM_LIMIT_BYTES = 56 * 1024 * 1024


def _moe_ffn_kernel(blk_e_ref, n_used_ref, first_ref, nxt_ref, slot_ref, clo_ref, chi_ref,
                    x_ref, wg_hbm, wu_hbm, wd_hbm, o_ref,
                    wgu_buf, wd_buf, stg_gu, stg_d, sem_gu, sem_d, h_s, *, chunks):
    i = pl.program_id(0)
    rows = W_CHUNK_ROWS
    n_g = wg_hbm.shape[1] // rows
    n_gu = 2 * n_g
    n_d = wd_hbm.shape[1] // rows

    def gu_copy(src, e, r0, s):
        return pltpu.make_async_copy(src.at[e, pl.ds(r0, rows), :], stg_gu.at[s], sem_gu.at[s])

    def d_copy(e, r0, s):
        return pltpu.make_async_copy(wd_hbm.at[e, pl.ds(r0, rows), :], stg_d.at[s], sem_d.at[s])

    def gu_start(e, c):
        s = lax.rem(c, W_STAGE_SLOTS)

        @pl.when(c < n_g)
        def _():
            gu_copy(wg_hbm, e, pl.multiple_of(c * rows, rows), s).start()

        @pl.when(c >= n_g)
        def _():
            gu_copy(wu_hbm, e, pl.multiple_of((c - n_g) * rows, rows), s).start()

    def d_start(e, c):
        d_copy(e, pl.multiple_of(c * rows, rows), lax.rem(c, W_STAGE_SLOTS)).start()

    def start_prefetch(e):
        for c in range(W_STAGE_SLOTS):
            gu_start(e, jnp.int32(c))
            d_start(e, jnp.int32(c))

    def convert(e, dst, lo, hi):
        def gu_body(c, carry):
            s = lax.rem(c, W_STAGE_SLOTS)
            gu_copy(wg_hbm, 0, 0, s).wait()
            r0 = pl.multiple_of(lax.rem(c, n_g) * rows, rows)

            @pl.when(c < n_g)
            def _():
                for c0, cs in chunks:
                    wgu_buf[dst, pl.ds(r0, rows), 2 * c0:2 * c0 + cs] = stg_gu[s, :, c0:c0 + cs].astype(BF16)

            @pl.when(c >= n_g)
            def _():
                for c0, cs in chunks:
                    wgu_buf[dst, pl.ds(r0, rows), 2 * c0 + cs:2 * c0 + 2 * cs] = stg_gu[s, :, c0:c0 + cs].astype(BF16)

            @pl.when(c + W_STAGE_SLOTS < n_gu)
            def _():
                gu_start(e, c + W_STAGE_SLOTS)

            return carry

        lax.fori_loop(jnp.clip(lo, 0, n_gu), jnp.clip(hi, 0, n_gu), gu_body, 0)

        def d_body(c, carry):
            s = lax.rem(c, W_STAGE_SLOTS)
            d_copy(0, 0, s).wait()
            wd_buf[dst, pl.ds(pl.multiple_of(c * rows, rows), rows), :] = stg_d[s].astype(BF16)

            @pl.when(c + W_STAGE_SLOTS < n_d)
            def _():
                d_start(e, c + W_STAGE_SLOTS)

            return carry

        lax.fori_loop(jnp.clip(lo - n_gu, 0, n_d), jnp.clip(hi - n_gu, 0, n_d), d_body, 0)

    @pl.when(i == 0)
    def _():
        start_prefetch(blk_e_ref[0])
        convert(blk_e_ref[0], 0, 0, n_gu + n_d)

    used = i < n_used_ref[0]

    @pl.when(used)
    def _():
        nxt = nxt_ref[i]
        cur = slot_ref[i]

        @pl.when((first_ref[i] == 1) & (nxt >= 0))
        def _():
            start_prefetch(nxt)

        o_ref[...] = _swiglu(x_ref[...], wgu_buf.at[pl.ds(cur, 1)], wd_buf.at[pl.ds(cur, 1)], h_s,
                             chunks).astype(o_ref.dtype)

        @pl.when(nxt >= 0)
        def _():
            convert(nxt, 1 - cur, clo_ref[i], chi_ref[i])

    @pl.when(jnp.logical_not(used))
    def _():
        o_ref[...] = jnp.zeros(o_ref.shape, o_ref.dtype)


def _pack_gate_up_kernel(wg_ref, wu_ref, o_ref, *, chunks):
    for c0, cs in chunks:
        o_ref[0, :, 2 * c0:2 * c0 + cs] = wg_ref[0, :, c0:c0 + cs].astype(o_ref.dtype)
        o_ref[0, :, 2 * c0 + cs:2 * c0 + 2 * cs] = wu_ref[0, :, c0:c0 + cs].astype(o_ref.dtype)


PACK_ROWS = 512


def _pack_gate_up(w_gate, w_up):
    E, D, F = w_gate.shape
    rows = min(PACK_ROWS, D)
    spec = pl.BlockSpec((1, rows, F), lambda e, r: (e, r, 0))
    return pl.pallas_call(
        functools.partial(_pack_gate_up_kernel, chunks=_hidden_chunks(F)),
        grid=(E, D // rows),
        in_specs=[spec, spec],
        out_specs=pl.BlockSpec((1, rows, 2 * F), lambda e, r: (e, r, 0)),
        out_shape=jax.ShapeDtypeStruct((E, D, 2 * F), BF16),
        compiler_params=_params(("parallel", "parallel")),
        name="pack_gate_up",
    )(w_gate, w_up)


def _moe_routed(xb, eid_t, rank_t, counts, w_gate, w_up, w_down):
    T, D = xb.shape
    K = eid_t.shape[0]
    E, F, _ = w_down.shape
    tm = MOE_FFN_TM
    n = T * K
    n_blk = -(-n // tm) + E
    i32 = jnp.int32

    padded = (counts + tm - 1) // tm * tm
    pad_end = jnp.cumsum(padded)
    pad_start = pad_end - padded
    start_t = jnp.zeros_like(eid_t)
    for e in range(E):
        start_t = jnp.where(eid_t == e, pad_start[e], start_t)
    dest = (start_t + rank_t).reshape(n)
    tok = jnp.zeros((n_blk * tm,), i32).at[dest].set(jnp.tile(jnp.arange(T, dtype=i32), K),
                                                     unique_indices=True, mode="promise_in_bounds")
    n_used = (pad_end[-1] // tm).astype(i32).reshape(1)
    blk_start = jnp.arange(n_blk, dtype=i32) * tm
    blk_e = jnp.minimum(jnp.sum((pad_end[None, :] <= blk_start[:, None]).astype(i32), axis=1), E - 1)

    blk_i = jnp.arange(n_blk, dtype=i32)
    used = blk_i < n_used[0]
    onehot_e = blk_e[:, None] == jnp.arange(E, dtype=i32)[None, :]
    lookup = lambda tbl: jnp.sum(jnp.where(onehot_e, tbl[None, :], 0), axis=1)
    first_blk = lookup(pad_start) // tm
    n_blk_e = jnp.maximum(lookup(padded) // tm, 1)
    first = (used & (blk_i == first_blk)).astype(i32)
    slot = lax.rem(jnp.cumsum(first) - 1, 2).astype(i32)
    nxt_blk = lookup(pad_end) // tm
    nxt_e = jnp.sum(jnp.where(blk_i[None, :] == nxt_blk[:, None], blk_e[None, :], 0), axis=1)
    nxt_e = jnp.where(used & (nxt_blk < n_used[0]), nxt_e, -1).astype(i32)
    n_chunks = (2 * D + F) // W_CHUNK_ROWS
    step = blk_i - first_blk
    c_lo = (n_chunks * step // n_blk_e).astype(i32)
    c_hi = (n_chunks * (step + 1) // n_blk_e).astype(i32)

    xs = xb.at[tok].get(mode="promise_in_bounds")

    hbm = pl.BlockSpec(memory_space=pl.ANY)
    y = pl.pallas_call(
        functools.partial(_moe_ffn_kernel, chunks=_hidden_chunks(F)),
        grid_spec=pltpu.PrefetchScalarGridSpec(
            num_scalar_prefetch=7,
            grid=(n_blk,),
            in_specs=[pl.BlockSpec((tm, D), lambda i, *_: (i, 0)), hbm, hbm, hbm],
            out_specs=pl.BlockSpec((tm, D), lambda i, *_: (i, 0)),
            scratch_shapes=[pltpu.VMEM((2, D, 2 * F), BF16), pltpu.VMEM((2, F, D), BF16),
                            pltpu.VMEM((W_STAGE_SLOTS, W_CHUNK_ROWS, F), F32),
                            pltpu.VMEM((W_STAGE_SLOTS, W_CHUNK_ROWS, D), F32),
                            pltpu.SemaphoreType.DMA((W_STAGE_SLOTS,)), pltpu.SemaphoreType.DMA((W_STAGE_SLOTS,)),
                            pltpu.VMEM((tm, F), BF16)]),
        out_shape=jax.ShapeDtypeStruct((n_blk * tm, D), BF16),
        compiler_params=pltpu.CompilerParams(dimension_semantics=("arbitrary",),
                                             vmem_limit_bytes=MOE_FFN_VMEM_LIMIT_BYTES),
        name="moe_ffn",
    )(blk_e, n_used, first, nxt_e, slot, c_lo, c_hi, xs, w_gate, w_up, w_down)

    return y.at[dest].get(mode="promise_in_bounds").reshape(K, T, D)


def _layernorm_rows(z, g, b):
    mu = jnp.mean(z, -1, keepdims=True)
    zc = z - mu
    var = jnp.mean(jnp.square(zc), -1, keepdims=True)
    return zc * lax.rsqrt(var + LN_EPS) * g + b


def _merge_kernel(x_ref, yn_ref, ym_ref, wg_ref, wbn_ref, wbm_ref, o_ref):
    x, yn, ym = x_ref[...], yn_ref[...], ym_ref[...]
    d = o_ref.shape[1]
    for c0 in range(0, d, MXU_WIDTH_V7X):
        cs = slice(c0, c0 + MXU_WIDTH_V7X)
        gn = jax.nn.sigmoid(jnp.dot(x, wg_ref[:, cs], preferred_element_type=F32))
        gm = jax.nn.sigmoid(jnp.dot(x, wg_ref[:, d + c0:d + c0 + MXU_WIDTH_V7X], preferred_element_type=F32))
        bn = jnp.dot(yn, wbn_ref[:, cs], preferred_element_type=F32)
        bm = jnp.dot(ym, wbm_ref[:, cs], preferred_element_type=F32)
        o_ref[:, cs] = (gn * bn + gm * bm).astype(o_ref.dtype)


def _merge(xb, y_nsa, y_mla, w_g, wb_nsa, wb_mla):
    T, D = xb.shape
    tm = PROJ_TM
    row = lambda a: pl.BlockSpec((tm, a.shape[1]), lambda i: (i, 0))
    return pl.pallas_call(
        _merge_kernel,
        grid=(T // tm,),
        in_specs=[row(xb), row(y_nsa), row(y_mla), _resident(w_g.shape), _resident(wb_nsa.shape),
                  _resident(wb_mla.shape)],
        out_specs=pl.BlockSpec((tm, D), lambda i: (i, 0)),
        out_shape=jax.ShapeDtypeStruct((T, D), BF16),
        compiler_params=_params(("parallel",)),
        name="merge",
    )(xb, y_nsa, y_mla, w_g, wb_nsa, wb_mla)


def _rank_before(v):
    n = v.shape[0]
    j_io = lax.broadcasted_iota(jnp.int32, v.shape, 0)
    cnt = jnp.zeros(v.shape, jnp.int32)
    for i in range(n):
        row = v[i:i + 1, :]
        cnt = cnt + ((row > v) | ((row == v) & (j_io > i))).astype(jnp.int32)
    return cnt


def _route_tile(logits_t, rb, cnt_s):
    E, tm = logits_t.shape
    per = E // N_EXPERT_GROUPS
    scores = jax.nn.sigmoid(logits_t)
    biased = scores + rb
    grp = []
    for g in range(N_EXPERT_GROUPS):
        blk = biased[g * per:(g + 1) * per, :]
        m1 = jnp.max(blk, axis=0, keepdims=True)
        eq = blk == m1
        n_eq = jnp.sum(eq.astype(F32), axis=0, keepdims=True)
        m2 = jnp.max(jnp.where(eq, NEG_INF, blk), axis=0, keepdims=True)
        grp.append(m1 + jnp.where(n_eq >= 2.0, m1, m2))
    gsel = _rank_before(jnp.concatenate(grp, axis=0)) < TOPK_GROUPS
    masked = jnp.concatenate(
        [jnp.where(gsel[g:g + 1, :], biased[g * per:(g + 1) * per, :], NEG_INF) for g in range(N_EXPERT_GROUPS)],
        axis=0)
    sel = _rank_before(masked) < TOP_K
    w_sel = jnp.where(sel, scores, 0.0)
    w_sel = w_sel / jnp.sum(w_sel, axis=0, keepdims=True) * ROUTED_SCALE
    tri = jnp.where(lax.broadcasted_iota(jnp.int32, (E, E), 1) <= lax.broadcasted_iota(jnp.int32, (E, E), 0),
                    1.0, 0.0).astype(BF16)
    sel_b = jnp.where(sel, 1.0, 0.0).astype(BF16)
    slot = jnp.dot(tri, sel_b, preferred_element_type=F32)
    upper = jnp.where(lax.broadcasted_iota(jnp.int32, (tm, tm), 0) <= lax.broadcasted_iota(jnp.int32, (tm, tm), 1),
                      1.0, 0.0).astype(BF16)
    incl = jnp.dot(sel_b, upper, preferred_element_type=F32)
    pos = cnt_s[...] + incl - 1.0
    cnt_s[...] = cnt_s[...] + incl[:, tm - 1:tm]
    e_io = lax.broadcasted_iota(jnp.int32, (E, tm), 0)
    ids, wts, ranks = [], [], []
    for k in range(TOP_K):
        mk = sel & (slot == float(k + 1))
        ids.append(jnp.sum(jnp.where(mk, e_io, 0), axis=0, keepdims=True))
        wts.append(jnp.sum(jnp.where(mk, w_sel, 0.0), axis=0, keepdims=True))
        ranks.append(jnp.sum(jnp.where(mk, pos, 0.0), axis=0, keepdims=True).astype(jnp.int32))
    pad = 8 - TOP_K
    ids.append(jnp.zeros((pad, tm), jnp.int32))
    wts.append(jnp.zeros((pad, tm), F32))
    ranks.append(jnp.zeros((pad, tm), jnp.int32))
    return jnp.concatenate(ids, axis=0), jnp.concatenate(wts, axis=0), jnp.concatenate(ranks, axis=0)


def _out_ln1_kernel(m_ref, x_ref, wo_ref, g_ref, b_ref, wrt_ref, rb_ref,
                    x1_ref, x1b_ref, eid_ref, wts_ref, rank_ref, cnt_ref, cnt_s):
    @pl.when(pl.program_id(0) == 0)
    def _():
        cnt_s[...] = jnp.zeros(cnt_s.shape, F32)

    m = m_ref[...]
    d = x1_ref.shape[1]
    for c0 in range(0, d, MXU_WIDTH_V7X):
        cs = slice(c0, c0 + MXU_WIDTH_V7X)
        x1_ref[:, cs] = DN_ALPHA * x_ref[:, cs] + jnp.dot(m, wo_ref[:, cs], preferred_element_type=F32)
    x1 = _layernorm_rows(x1_ref[...], g_ref[...], b_ref[...])
    x1_ref[...] = x1
    x1b = x1.astype(BF16)
    x1b_ref[...] = x1b
    logits_t = _dot_nt(wrt_ref[...], x1b)
    eid_ref[...], wts_ref[...], rank_ref[...] = _route_tile(logits_t, rb_ref[...], cnt_s)
    cnt_ref[...] = cnt_s[...]


def _out_ln1(merged, x2, w_out, g, b, w_router_t, router_b):
    T, D = x2.shape
    E = w_router_t.shape[0]
    tm = MOE_TM
    row = lambda w: pl.BlockSpec((tm, w), lambda i: (i, 0))
    col = pl.BlockSpec((8, tm), lambda i: (0, i))
    return pl.pallas_call(
        _out_ln1_kernel,
        grid=(T // tm,),
        in_specs=[row(D), row(D), _resident(w_out.shape), _resident(g.shape), _resident(b.shape),
                  _resident(w_router_t.shape), _resident(router_b.shape)],
        out_specs=[row(D), row(D), col, col, col, pl.BlockSpec((E, 1), lambda i: (0, 0))],
        out_shape=[jax.ShapeDtypeStruct((T, D), F32), jax.ShapeDtypeStruct((T, D), BF16),
                   jax.ShapeDtypeStruct((8, T), jnp.int32), jax.ShapeDtypeStruct((8, T), F32),
                   jax.ShapeDtypeStruct((8, T), jnp.int32), jax.ShapeDtypeStruct((E, 1), F32)],
        scratch_shapes=[pltpu.VMEM((E, 1), F32)],
        compiler_params=_params(("arbitrary",)),
        name="out_ln1",
    )(merged, x2, w_out, g, b, w_router_t, router_b)


def _ffn_ln2_kernel(x1_ref, yk_ref, w_ref, sgu_ref, sd_ref, g_ref, b_ref, x2_ref, x2b_ref, h_s, *, chunks):
    x1 = x1_ref[...]
    acc = _swiglu(x1.astype(BF16), sgu_ref, sd_ref, h_s, chunks)
    w = w_ref[...]
    for k in range(yk_ref.shape[0]):
        acc = acc + yk_ref[k].astype(F32) * w[:, k:k + 1]
    x2 = _layernorm_rows(DN_ALPHA * x1 + acc, g_ref[...], b_ref[...])
    x2_ref[...] = x2
    x2b_ref[...] = x2.astype(BF16)


FFN_LN2_VMEM_LIMIT_BYTES = 56 * 1024 * 1024


def _ffn_ln2(x1, yk, wts, sh_gate_up, sh_down, g, b):
    T, D = x1.shape
    K = yk.shape[0]
    tm = MOE_TM
    F = sh_down.shape[1]
    row = lambda w: pl.BlockSpec((tm, w), lambda i: (i, 0))
    return pl.pallas_call(
        functools.partial(_ffn_ln2_kernel, chunks=_hidden_chunks(F)),
        grid=(T // tm,),
        in_specs=[row(D), pl.BlockSpec((K, tm, D), lambda i: (0, i, 0)), row(wts.shape[1]),
                  _resident(sh_gate_up.shape), _resident(sh_down.shape), _resident(g.shape), _resident(b.shape)],
        out_specs=[row(D), row(D)],
        out_shape=[jax.ShapeDtypeStruct((T, D), F32), jax.ShapeDtypeStruct((T, D), BF16)],
        scratch_shapes=[pltpu.VMEM((tm, F), BF16)],
        compiler_params=pltpu.CompilerParams(dimension_semantics=("parallel",),
                                             vmem_limit_bytes=FFN_LN2_VMEM_LIMIT_BYTES),
        name="ffn_ln2",
    )(x1, yk, wts, sh_gate_up, sh_down, g, b)


def _ple_kernel(x2_ref, x2b_ref, p_ref, wpg_ref, wpp_ref, o_ref):
    x2b = x2b_ref[...]
    pb = p_ref[...].astype(BF16)
    d = o_ref.shape[1]
    for c0 in range(0, d, MXU_WIDTH_V7X):
        cs = slice(c0, c0 + MXU_WIDTH_V7X)
        gate = jax.nn.sigmoid(jnp.dot(x2b, wpg_ref[:, cs], preferred_element_type=F32))
        pp = jnp.dot(pb, wpp_ref[:, cs], preferred_element_type=F32)
        o_ref[:, cs] = x2_ref[:, cs] + gate * pp


def _ple(x2, x2b, p2, w_pg, w_pp):
    T, D = x2.shape
    tm = PROJ_TM
    row = lambda w: pl.BlockSpec((tm, w), lambda i: (i, 0))
    return pl.pallas_call(
        _ple_kernel,
        grid=(T // tm,),
        in_specs=[row(D), row(D), row(p2.shape[1]), _resident(w_pg.shape), _resident(w_pp.shape)],
        out_specs=row(D),
        out_shape=jax.ShapeDtypeStruct((T, D), F32),
        compiler_params=_params(("parallel",)),
        name="ple",
    )(x2, x2b, p2, w_pg, w_pp)


def _attention_branches(xb, S, w_in, pe_k, pe_v, wk1, wk2, wv1, wv2, q_norm, w_uq, kv_norm, w_uk, w_uv):
    T, D = xb.shape
    B = T // S
    c0 = W_NSA_Q
    c1 = c0 + W_NSA_KV
    c2 = c1 + W_NSA_GATE
    c3 = c2 + W_MLA_CQ
    c4 = c3 + W_MLA_CKV
    c5 = c4 + W_MLA_KR
    G, dk = NSA_GROUPS, NSA_DK
    pos = jnp.arange(S)
    cos128, sin128 = _rope_tables_128(pos)
    cos64, slo64, shi64 = _rope_tables_64(pos)

    qkv = _nsa_proj(xb, w_in[:, :c1].astype(BF16), cos128, sin128, S)

    pad_kr = jnp.zeros((D, LANES - W_MLA_KR), w_in.dtype)
    pad_g = jnp.zeros((D, LANES - W_NSA_GATE), w_in.dtype)
    w_m = jnp.concatenate([w_in[:, c2:c4], w_in[:, c4:c5], pad_kr, w_in[:, c1:c2], pad_g], axis=1).astype(BF16)
    H = MLA_HEADS
    wq = w_uq.reshape(MLA_Q_RANK, H, MLA_NOPE + MLA_ROPE)
    wq = jnp.concatenate([wq, jnp.zeros((MLA_Q_RANK, H, 2 * LANES - MLA_NOPE - MLA_ROPE), wq.dtype)], axis=-1)
    wq = wq.reshape(MLA_Q_RANK, H * 2 * LANES).astype(BF16)
    wukv = jnp.concatenate([w_uk, w_uv], axis=1).astype(BF16)
    q_m, kv_m, kr_m, gates = _mla_in(xb, w_m, q_norm.reshape(1, -1), kv_norm.reshape(1, -1), wq, wukv,
                                     cos64, slo64, shi64, S)

    n_chunks = S // CMP_STRIDE
    kvc = qkv[:, NSA_HEADS * dk:(NSA_HEADS + 2 * G) * dk]
    zc = kvc.reshape(B, n_chunks, CMP_STRIDE, 2, G, dk).transpose(0, 3, 4, 1, 2, 5)
    zc = zc.reshape(B, 2, G, n_chunks, CMP_STRIDE * dk)
    pe = jnp.stack([pe_k.reshape(-1), pe_v.reshape(-1)])
    pe = jnp.broadcast_to(pe[:, None, :], (2, 8, pe.shape[-1])).astype(BF16)
    half = CMP_STRIDE * dk
    w1 = jnp.stack([jnp.concatenate([wk1[:half], wk1[half:]], axis=1),
                    jnp.concatenate([wv1[:half], wv1[half:]], axis=1)]).astype(BF16)
    w2 = jnp.stack([wk2, wv2]).astype(BF16)
    pos_c = jnp.arange(n_chunks) * CMP_STRIDE + (CMP_BLK - 1)
    cos_c, sin_c = _rope_tables_128(pos_c)
    cmp_kv = _nsa_compress(zc, pe, w1, w2, cos_c, sin_c)

    g3 = gates[:, :W_NSA_GATE].reshape(B, S, G, NSA_HPG * 3).transpose(0, 2, 3, 1)
    g3 = jnp.pad(g3, ((0, 0), (0, 0), (0, LANES - NSA_HPG * 3), (0, 0)))

    qkv3 = qkv.reshape(B, S, -1)
    qt = jnp.swapaxes(qkv3[:, :, :NSA_HEADS * dk], 1, 2)
    v0 = NSA_HEADS + 2 * G
    v_sw = jnp.concatenate([qkv3[:, :, (v0 + G) * dk:(v0 + 2 * G) * dk],
                            qkv3[:, :, (v0 + 3 * G) * dk:(v0 + 4 * G) * dk]], axis=-1)
    vt = jnp.swapaxes(v_sw, 1, 2)
    vct = jnp.swapaxes(cmp_kv[:, 1], 2, 3)
    y_nsa = _nsa_attention(qkv3, qt, g3, cmp_kv, vct, vt)
    kv_m = kv_m.reshape(B, S, -1)
    qt_m = jnp.swapaxes(q_m.reshape(B, S, -1), 1, 2)
    vt_m = jnp.swapaxes(kv_m[:, :, MLA_HEADS * MLA_NOPE:], 1, 2)
    y_mla = _mla_attention(qt_m, kv_m, kr_m.reshape(B, S, -1), vt_m)
    return y_nsa, y_mla


def kernel(x, p, w_in, nsa_pe_k, nsa_pe_v, nsa_cmp_k_w1, nsa_cmp_k_w2, nsa_cmp_v_w1, nsa_cmp_v_w2, mla_q_norm, mla_w_uq, mla_kv_norm, mla_w_uk, mla_w_uv, w_branch_nsa, w_branch_mla, w_out, ln1_g, ln1_b, router_w, router_b, exp_w_gate, exp_w_up, exp_w_down, sh_w_gate, sh_w_up, sh_w_down, ln2_g, ln2_b, ple_w_proj, ple_w_gate):
    B, S, D = x.shape
    T = B * S
    c5 = D_IN - W_MERGE
    for i in range(DEPTH):
        x2 = x.reshape(T, D)
        xb = x2.astype(BF16)
        y_nsa, y_mla = _attention_branches(
            xb, S, w_in[i], nsa_pe_k[i], nsa_pe_v[i], nsa_cmp_k_w1[i], nsa_cmp_k_w2[i],
            nsa_cmp_v_w1[i], nsa_cmp_v_w2[i], mla_q_norm[i], mla_w_uq[i], mla_kv_norm[i],
            mla_w_uk[i], mla_w_uv[i])

        merged = _merge(xb, y_nsa.reshape(T, -1), y_mla.reshape(T, -1), w_in[i][:, c5:].astype(BF16),
                        w_branch_nsa[i].astype(BF16), w_branch_mla[i].astype(BF16))
        x1, x1b, eid_t, wts_t, rank_t, counts = _out_ln1(
            merged, x2, w_out[i].astype(BF16), ln1_g[i].reshape(1, D), ln1_b[i].reshape(1, D),
            router_w[i].T.astype(BF16), router_b[i].reshape(N_EXPERTS, 1).astype(F32))

        yk = _moe_routed(x1b, eid_t[:TOP_K], rank_t[:TOP_K], counts.reshape(N_EXPERTS).astype(jnp.int32),
                         exp_w_gate[i], exp_w_up[i], exp_w_down[i])
        x2n, x2b = _ffn_ln2(x1, yk, wts_t.T, _pack_gate_up(sh_w_gate[i][None], sh_w_up[i][None]),
                            sh_w_down[i][None].astype(BF16), ln2_g[i].reshape(1, D), ln2_b[i].reshape(1, D))
        out = _ple(x2n, x2b, p[i].reshape(T, PLE_DIM), ple_w_gate[i].astype(BF16), ple_w_proj[i].astype(BF16))
        x = out.reshape(B, S, D)
    return x
```

```python
import functools
import math

import jax
import jax.numpy as jnp
from jax import lax
from jax.experimental import pallas as pl
from jax.experimental.pallas import tpu as pltpu

D_MODEL = 2048
BATCH = 8
SEQ = 4096
DEPTH = 1

PLE_DIM = 256
ROPE_THETA = 10000.0
LN_EPS = 1e-5
RMS_EPS = 1e-6
NEG_INF = -1e30

NSA_HEADS = 8
NSA_GROUPS = 2
NSA_HPG = NSA_HEADS // NSA_GROUPS
NSA_DK = 128
NSA_DV = 128
CMP_BLK = 32
CMP_STRIDE = 16
CMP_HID = 256
SLC_BLK = 64
N_SEL = 16
WIN = 512
FORCE_SCORE = 1e4

MLA_HEADS = 8
MLA_Q_RANK = 768
MLA_KV_RANK = 512
MLA_NOPE = 128
MLA_ROPE = 64
MLA_DV = 128

N_EXPERTS = 64
TOP_K = 6
N_EXPERT_GROUPS = 8
TOPK_GROUPS = 4
D_EXPERT = 1408
ROUTED_SCALE = 2.5

W_NSA_Q = NSA_HEADS * NSA_DK
W_NSA_KV = 3 * NSA_GROUPS * (NSA_DK + NSA_DV)
W_NSA_GATE = 3 * NSA_HEADS
W_MLA_CQ = MLA_Q_RANK
W_MLA_CKV = MLA_KV_RANK
W_MLA_KR = MLA_ROPE
W_MERGE = 2 * D_MODEL
D_IN = W_NSA_Q + W_NSA_KV + W_NSA_GATE + W_MLA_CQ + W_MLA_CKV + W_MLA_KR + W_MERGE

DN_ALPHA = (2.0 * DEPTH) ** 0.25
DN_BETA = (8.0 * DEPTH) ** -0.25

V7X_VMEM_LIMIT_BYTES = 48 * 1024 * 1024
LANES = 128
MXU_WIDTH_V7X = 256
PROJ_TM = 512
BATCH_SPLITS = 2

BF16 = jnp.bfloat16
F32 = jnp.float32


def _params(sem):
    return pltpu.CompilerParams(dimension_semantics=sem, vmem_limit_bytes=V7X_VMEM_LIMIT_BYTES)


def _resident(shape):
    return pl.BlockSpec(shape, lambda *_: (0,) * len(shape), pipeline_mode=pl.Buffered(1))


def _rope_tables_128(pos):
    half = LANES // 2
    inv = jnp.power(ROPE_THETA, -jnp.arange(half, dtype=F32) * 2.0 / LANES)
    ang = pos.astype(F32)[:, None] * inv[None, :]
    cos, sin = jnp.cos(ang), jnp.sin(ang)
    return jnp.concatenate([cos, cos], -1), jnp.concatenate([-sin, sin], -1)


def _rope_tables_64(pos):
    half = MLA_ROPE // 2
    inv = jnp.power(ROPE_THETA, -jnp.arange(half, dtype=F32) * 2.0 / MLA_ROPE)
    ang = pos.astype(F32)[:, None] * inv[None, :]
    cos, sin = jnp.cos(ang), jnp.sin(ang)
    zero = jnp.zeros_like(sin)
    return (jnp.concatenate([cos, cos, cos, cos], -1),
            jnp.concatenate([-sin, zero, -sin, zero], -1),
            jnp.concatenate([zero, sin, zero, sin], -1))


def _rope128(z, cos, sin):
    return z * cos + pltpu.roll(z, LANES // 2, 1) * sin


def _rope64(z, cos, sin_lo, sin_hi):
    return z * cos + pltpu.roll(z, LANES - MLA_ROPE // 2, 1) * sin_lo + pltpu.roll(z, MLA_ROPE // 2, 1) * sin_hi


def _nsa_proj_kernel(x_ref, w_ref, cos_ref, sin_ref, o_ref, *, rope_heads, n_q_heads, scale):
    x = x_ref[...]
    cos, sin = cos_ref[...], sin_ref[...]
    n_heads = w_ref.shape[1] // LANES
    for t in range(n_heads // 2):
        c0 = t * MXU_WIDTH_V7X
        z = jnp.dot(x, w_ref[:, c0:c0 + MXU_WIDTH_V7X], preferred_element_type=F32)
        for hh in range(2):
            head = 2 * t + hh
            zh = z[:, hh * LANES:(hh + 1) * LANES]
            if head in rope_heads:
                zh = _rope128(zh, cos, sin)
            if head < n_q_heads:
                zh = zh * scale
            o_ref[:, head * LANES:(head + 1) * LANES] = zh.astype(o_ref.dtype)


def _nsa_proj(xb, w, cos, sin, S):
    T, D = xb.shape
    n = w.shape[1]
    tm = min(PROJ_TM, S)
    n_pos = S // tm
    kv0 = NSA_HEADS
    per = NSA_GROUPS * 2
    rope_heads = tuple(range(NSA_HEADS)) + tuple(kv0 + per * br + g for br in (1, 2) for g in range(NSA_GROUPS))
    return pl.pallas_call(
        functools.partial(_nsa_proj_kernel, rope_heads=rope_heads, n_q_heads=NSA_HEADS, scale=NSA_DK ** -0.5),
        grid=(T // tm,),
        in_specs=[pl.BlockSpec((tm, D), lambda i: (i, 0)),
                  _resident((D, n)),
                  pl.BlockSpec((tm, LANES), lambda i: (i % n_pos, 0)),
                  pl.BlockSpec((tm, LANES), lambda i: (i % n_pos, 0))],
        out_specs=pl.BlockSpec((tm, n), lambda i: (i, 0)),
        out_shape=jax.ShapeDtypeStruct((T, n), BF16),
        compiler_params=_params(("parallel",)),
        name="nsa_proj",
    )(xb, w, cos, sin)


def _rmsnorm_rows(z, g):
    return z * lax.rsqrt(jnp.mean(jnp.square(z), -1, keepdims=True) + RMS_EPS) * g


def _mla_in_kernel(x_ref, w_ref, gq_ref, gkv_ref, wuq_ref, wukv_ref, cos_ref, slo_ref, shi_ref,
                   q_ref, kv_ref, kr_ref, gate_ref, *, scale):
    x = x_ref[...]
    cos, slo, shi = cos_ref[...], slo_ref[...], shi_ref[...]
    c1 = MLA_Q_RANK
    c2 = c1 + MLA_KV_RANK
    cq = jnp.dot(x, w_ref[:, :c1], preferred_element_type=F32)
    cqn = _rmsnorm_rows(cq, gq_ref[...]).astype(BF16)
    ckv = jnp.dot(x, w_ref[:, c1:c2], preferred_element_type=F32)
    ckvn = _rmsnorm_rows(ckv, gkv_ref[...]).astype(BF16)
    tail = jnp.dot(x, w_ref[:, c2:c2 + 2 * LANES], preferred_element_type=F32)
    kr_ref[...] = _rope64(tail[:, :LANES], cos, slo, shi).astype(kr_ref.dtype)
    gate_ref[...] = jax.nn.sigmoid(tail[:, LANES:])
    for h in range(MLA_HEADS):
        c0 = h * MXU_WIDTH_V7X
        z = jnp.dot(cqn, wuq_ref[:, c0:c0 + MXU_WIDTH_V7X], preferred_element_type=F32)
        q_ref[:, c0:c0 + LANES] = (z[:, :LANES] * scale).astype(q_ref.dtype)
        q_ref[:, c0 + LANES:c0 + 2 * LANES] = (_rope64(z[:, LANES:], cos, slo, shi) * scale).astype(q_ref.dtype)
    n_kv = wukv_ref.shape[1]
    for c0 in range(0, n_kv, MXU_WIDTH_V7X):
        kv_ref[:, c0:c0 + MXU_WIDTH_V7X] = jnp.dot(
            ckvn, wukv_ref[:, c0:c0 + MXU_WIDTH_V7X], preferred_element_type=F32).astype(kv_ref.dtype)


def _mla_in(xb, w, gq, gkv, wuq, wukv, cos, slo, shi, S):
    T, D = xb.shape
    tm = min(PROJ_TM, S)
    n_pos = S // tm
    tbl = pl.BlockSpec((tm, LANES), lambda i: (i % n_pos, 0))
    nq, nkv = wuq.shape[1], wukv.shape[1]
    return pl.pallas_call(
        functools.partial(_mla_in_kernel, scale=(MLA_NOPE + MLA_ROPE) ** -0.5),
        grid=(T // tm,),
        in_specs=[pl.BlockSpec((tm, D), lambda i: (i, 0)),
                  _resident(w.shape), _resident(gq.shape), _resident(gkv.shape),
                  _resident(wuq.shape), _resident(wukv.shape), tbl, tbl, tbl],
        out_specs=[pl.BlockSpec((tm, nq), lambda i: (i, 0)),
                   pl.BlockSpec((tm, nkv), lambda i: (i, 0)),
                   pl.BlockSpec((tm, LANES), lambda i: (i, 0)),
                   pl.BlockSpec((tm, LANES), lambda i: (i, 0))],
        out_shape=[jax.ShapeDtypeStruct((T, nq), BF16), jax.ShapeDtypeStruct((T, nkv), BF16),
                   jax.ShapeDtypeStruct((T, LANES), BF16), jax.ShapeDtypeStruct((T, LANES), F32)],
        compiler_params=_params(("parallel",)),
        name="mla_in",
    )(xb, w, gq, gkv, wuq, wukv, cos, slo, shi)


MASK_BIG = 16384.0


def _softmax_reset(m_s, l_s, acc_s):
    m_s[...] = jnp.full(m_s.shape, NEG_INF, F32)
    l_s[...] = jnp.zeros(l_s.shape, F32)
    acc_s[...] = jnp.zeros(acc_s.shape, F32)


def _online_softmax_step(s, v, m_s, l_s, acc_s):
    m_prev = m_s[...]
    m_new = jnp.maximum(m_prev, jnp.max(s, axis=-1, keepdims=True))
    alpha = jnp.exp(m_prev - m_new)
    p = jnp.exp(s - m_new)
    l_s[...] = alpha * l_s[...] + jnp.sum(p, axis=-1, keepdims=True)
    acc_s[...] = alpha * acc_s[...] + jnp.dot(p.astype(BF16), v, preferred_element_type=F32)
    m_s[...] = m_new


def _online_softmax_step_t(st, vt, m_s, l_s, acc_s):
    m_prev = m_s[...]
    m_new = jnp.maximum(m_prev, jnp.max(st, axis=0, keepdims=True))
    alpha = jnp.exp(m_prev - m_new)
    p = jnp.exp(st - m_new)
    l_s[...] = alpha * l_s[...] + jnp.sum(p, axis=0, keepdims=True)
    acc_s[...] = alpha * acc_s[...] + jnp.dot(vt, p.astype(BF16), preferred_element_type=F32)
    m_s[...] = m_new


def _dot_nt(a, b):
    return lax.dot_general(a, b, (((1,), (1,)), ((), ())), preferred_element_type=F32)


MLA_TQ = 512
MLA_TK = 512


MLA_HPS = 2


def _mla_attn_kernel(qt_ref, kn_ref, kr_ref, vt_ref, o_ref, m_s, l_s, acc_s, *, tq, tk, hps):
    q0 = pl.program_id(2) * tq
    qts = [qt_ref[0, hh * 2 * LANES:(hh + 1) * 2 * LANES, :] for hh in range(hps)]
    _softmax_reset(m_s, l_s, acc_s)

    def tile(j, causal):
        k0 = pl.multiple_of(j * tk, tk)
        kr = kr_ref[0, pl.ds(k0, tk), :]
        for hh in range(hps):
            hs = slice(hh * LANES, (hh + 1) * LANES)
            ka = jnp.concatenate([kn_ref[0, pl.ds(k0, tk), hs], kr], axis=1)
            st = jnp.dot(ka, qts[hh], preferred_element_type=F32)
            if causal:
                kpos = k0 + lax.broadcasted_iota(jnp.int32, (tk, tq), 0)
                qpos = q0 + lax.broadcasted_iota(jnp.int32, (tk, tq), 1)
                st = jnp.where(kpos <= qpos, st, -MASK_BIG)
            _online_softmax_step_t(st, vt_ref[0, hs, pl.ds(k0, tk)], m_s.at[hh], l_s.at[hh], acc_s.at[hh])

    j_diag = q0 // tk

    def body(j, carry):
        tile(j, False)
        return carry

    lax.fori_loop(0, j_diag, body, 0)
    tile(j_diag, True)
    for hh in range(hps):
        o_ref[0, :, hh * LANES:(hh + 1) * LANES] = jnp.transpose(acc_s[hh] / l_s[hh]).astype(o_ref.dtype)


def _mla_attention(qt, kv, kr, vt):
    B, _, S = qt.shape
    H = MLA_HEADS
    hps = MLA_HPS
    tq = tk = min(MLA_TQ, S)
    return pl.pallas_call(
        functools.partial(_mla_attn_kernel, tq=tq, tk=tk, hps=hps),
        grid=(B, H // hps, S // tq),
        in_specs=[pl.BlockSpec((1, hps * 2 * LANES, tq), lambda b, h, i: (b, h, i)),
                  pl.BlockSpec((1, S, hps * LANES), lambda b, h, i: (b, 0, h)),
                  pl.BlockSpec((1, S, LANES), lambda b, h, i: (b, 0, 0)),
                  pl.BlockSpec((1, hps * LANES, S), lambda b, h, i: (b, h, 0))],
        out_specs=pl.BlockSpec((1, tq, hps * LANES), lambda b, h, i: (b, i, h)),
        out_shape=jax.ShapeDtypeStruct((B, S, H * MLA_DV), BF16),
        scratch_shapes=[pltpu.VMEM((hps, 1, tq), F32), pltpu.VMEM((hps, 1, tq), F32),
                        pltpu.VMEM((hps, MLA_DV, tq), F32)],
        compiler_params=_params(("parallel", "parallel", "arbitrary")),
        name="mla_attn",
    )(qt, kv, kr, vt)


def _gelu_tanh(x):
    return 0.5 * x * (1.0 + jnp.tanh(math.sqrt(2.0 / math.pi) * (x + 0.044715 * (x * x * x))))


def _compress_kernel(z_ref, pe_ref, w1_ref, w2_ref, cos_ref, sin_ref, o_ref, *, hid):
    which = pl.program_id(1)
    z = z_ref[0, 0, 0]
    n_chunks = z.shape[0]
    cd = z.shape[1]
    ab = jnp.dot(z, w1_ref[0], preferred_element_type=F32)
    a_part = ab[:, :hid]
    b_next = pltpu.roll(ab[:, hid:], n_chunks - 1, 0)
    pe_a = jnp.dot(pe_ref[0, :, :cd], w1_ref[0, :, :hid], preferred_element_type=F32)
    pe_b = jnp.dot(pe_ref[0, :, cd:], w1_ref[0, :, hid:], preferred_element_type=F32)
    h = _gelu_tanh(a_part + b_next + pe_a[0:1] + pe_b[0:1])
    out = jnp.dot(h.astype(BF16), w2_ref[0], preferred_element_type=F32)
    roped = _rope128(out, cos_ref[...], sin_ref[...])
    o_ref[0, 0, 0] = jnp.where(which == 0, roped, out).astype(o_ref.dtype)


def _nsa_compress(zc, pe, w1, w2, cos_c, sin_c):
    B, _, G, n_chunks, cd = zc.shape
    hid = w2.shape[1]
    d = w2.shape[2]
    return pl.pallas_call(
        functools.partial(_compress_kernel, hid=hid),
        grid=(B, 2, G),
        in_specs=[pl.BlockSpec((1, 1, 1, n_chunks, cd), lambda b, t, g: (b, t, g, 0, 0)),
                  pl.BlockSpec((1, 8, 2 * cd), lambda b, t, g: (t, 0, 0)),
                  pl.BlockSpec((1, cd, 2 * hid), lambda b, t, g: (t, 0, 0)),
                  pl.BlockSpec((1, hid, d), lambda b, t, g: (t, 0, 0)),
                  pl.BlockSpec((n_chunks, LANES), lambda b, t, g: (0, 0)),
                  pl.BlockSpec((n_chunks, LANES), lambda b, t, g: (0, 0))],
        out_specs=pl.BlockSpec((1, 1, 1, n_chunks, d), lambda b, t, g: (b, t, g, 0, 0)),
        out_shape=jax.ShapeDtypeStruct((B, 2, G, n_chunks, d), BF16),
        compiler_params=_params(("parallel", "parallel", "parallel")),
        name="nsa_compress",
    )(zc, pe, w1, w2, cos_c, sin_c)


NSA_TQ = 256
NSA_TK = 512


def _nsa_kernel(qt_ref, g_ref, kc_ref, vct_ref, ks_ref, vst_ref, kw_ref, vwt_ref, o_ref,
                m_s, l_s, acc_s, out_s, *, tq, tk, hpg, dk, n_cmp, n_slc):
    q0 = pl.program_id(2) * tq
    cols = hpg * tq
    i32 = jnp.int32
    qt = jnp.concatenate([qt_ref[0, h * dk:(h + 1) * dk, :] for h in range(hpg)], axis=1)
    gates = g_ref[0, 0]

    def gate_row(branch):
        return jnp.concatenate([gates[3 * h + branch:3 * h + branch + 1, :] for h in range(hpg)], axis=1)

    def qpos_of(n_rows):
        return q0 + lax.rem(lax.broadcasted_iota(i32, (n_rows, cols), 1), tq)

    n_pad = kc_ref.shape[3]
    sc = jnp.dot(kc_ref[0, 0, 0], qt, preferred_element_type=F32)
    n_io = lax.broadcasted_iota(i32, (n_pad, cols), 0)
    vis = (n_io * CMP_STRIDE + (CMP_BLK - 1) <= qpos_of(n_pad)) & (n_io < n_cmp)
    sc = jnp.where(vis, sc, NEG_INF)
    e = jnp.where(vis, jnp.exp(sc - jnp.max(sc, axis=0, keepdims=True)), 0.0)
    p = e / jnp.maximum(jnp.sum(e, axis=0, keepdims=True), 1e-30)
    o_cmp = jnp.dot(vct_ref[0, 0], p.astype(BF16), preferred_element_type=F32)
    out_s[...] = gate_row(0) * o_cmp

    imp = p[:, 0:tq]
    for h in range(1, hpg):
        imp = imp + p[:, h * tq:(h + 1) * tq]
    ratio = SLC_BLK // CMP_STRIDE
    span = CMP_BLK // CMP_STRIDE
    j_p = lax.broadcasted_iota(i32, (n_slc, n_pad), 0)
    c_p = lax.broadcasted_iota(i32, (n_slc, n_pad), 1)
    lo_c = ratio * j_p - (span - 1)
    pool = jnp.where((c_p >= lo_c) & (c_p < lo_c + ratio + span - 1), 1.0, 0.0).astype(BF16)
    hi = imp.astype(BF16)
    r1 = imp - hi.astype(F32)
    mid = r1.astype(BF16)
    lo = (r1 - mid.astype(F32)).astype(BF16)
    p_slc = (jnp.dot(pool, hi, preferred_element_type=F32) + jnp.dot(pool, mid, preferred_element_type=F32)
             + jnp.dot(pool, lo, preferred_element_type=F32))

    j_io = lax.broadcasted_iota(i32, (n_slc, tq), 0)
    qpos_l = q0 + lax.broadcasted_iota(i32, (n_slc, tq), 1)
    cur = qpos_l // SLC_BLK
    forced = (j_io == 0) | (j_io == cur) | (j_io == cur - 1)
    valid = j_io * SLC_BLK <= qpos_l
    score = jnp.where(forced, FORCE_SCORE, jnp.where(valid, p_slc, -FORCE_SCORE))
    bias = jnp.where(_rank_before(score) < N_SEL, 0.0, -MASK_BIG)
    if n_slc < LANES:
        bias = jnp.concatenate([bias, jnp.full((LANES - n_slc, tq), -MASK_BIG, F32)], axis=0)
    bias = bias.astype(BF16)
    qa = jnp.concatenate([qt, jnp.concatenate([bias] * hpg, axis=1)], axis=0)

    qpos_k = qpos_of(tk)
    k_io = lax.broadcasted_iota(i32, (tk, cols), 0)

    _softmax_reset(m_s, l_s, acc_s)

    def slc_tile(j, causal):
        k0 = pl.multiple_of(j * tk, tk)
        kblk = (k0 + lax.broadcasted_iota(i32, (tk, LANES), 0)) // SLC_BLK
        onehot = jnp.where(kblk == lax.broadcasted_iota(i32, (tk, LANES), 1), 1.0, 0.0).astype(BF16)
        ka = jnp.concatenate([ks_ref[0, pl.ds(k0, tk), :], onehot], axis=1)
        s = jnp.dot(ka, qa, preferred_element_type=F32)
        if causal:
            s = jnp.where(k0 + k_io <= qpos_k, s, -MASK_BIG)
        _online_softmax_step_t(s, vst_ref[0, :, pl.ds(k0, tk)], m_s, l_s, acc_s)

    j_diag = q0 // tk

    def body(j, carry):
        slc_tile(j, False)
        return carry

    lax.fori_loop(0, j_diag, body, 0)
    slc_tile(j_diag, True)
    out_s[...] += gate_row(1) * (acc_s[...] / l_s[...])

    _softmax_reset(m_s, l_s, acc_s)

    def win_tile(j):
        k0 = pl.multiple_of(j * tk, tk)
        s = jnp.dot(kw_ref[0, pl.ds(k0, tk), :], qt, preferred_element_type=F32)
        kpos = k0 + k_io
        s = jnp.where((kpos <= qpos_k) & (kpos > qpos_k - WIN), s, -MASK_BIG)
        _online_softmax_step_t(s, vwt_ref[0, :, pl.ds(k0, tk)], m_s, l_s, acc_s)

    @pl.when(j_diag >= 1)
    def _():
        win_tile(j_diag - 1)

    win_tile(j_diag)
    out = out_s[...] + gate_row(2) * (acc_s[...] / l_s[...])
    for h in range(hpg):
        o_ref[0, :, h * dk:(h + 1) * dk] = jnp.transpose(out[:, h * tq:(h + 1) * tq]).astype(o_ref.dtype)


def _nsa_attention(qkv, qt, gates_t, cmp_kv, vct, vt):
    B, S, _ = qkv.shape
    G, hpg, dk = NSA_GROUPS, NSA_HPG, NSA_DK
    n_pad = cmp_kv.shape[3]
    n_cmp = (S - CMP_BLK) // CMP_STRIDE + 1
    tq, tk = min(NSA_TQ, S), min(NSA_TK, S)
    assert tk == WIN and tq <= tk
    cols = hpg * tq
    kv0 = NSA_HEADS

    def k_spec(branch):
        base = kv0 + 2 * G * branch
        return pl.BlockSpec((1, S, dk), lambda b, g, i: (b, 0, base + g))

    def vt_spec(branch):
        return pl.BlockSpec((1, dk, S), lambda b, g, i: (b, (branch - 1) * G + g, 0))

    return pl.pallas_call(
        functools.partial(_nsa_kernel, tq=tq, tk=tk, hpg=hpg, dk=dk, n_cmp=n_cmp, n_slc=S // SLC_BLK),
        grid=(B, G, S // tq),
        in_specs=[pl.BlockSpec((1, hpg * dk, tq), lambda b, g, i: (b, g, i)),
                  pl.BlockSpec((1, 1, LANES, tq), lambda b, g, i: (b, g, 0, i)),
                  pl.BlockSpec((1, 1, 1, n_pad, dk), lambda b, g, i: (b, 0, g, 0, 0)),
                  pl.BlockSpec((1, 1, dk, n_pad), lambda b, g, i: (b, g, 0, 0)),
                  k_spec(1), vt_spec(1), k_spec(2), vt_spec(2)],
        out_specs=pl.BlockSpec((1, tq, hpg * dk), lambda b, g, i: (b, i, g)),
        out_shape=jax.ShapeDtypeStruct((B, S, NSA_HEADS * NSA_DV), BF16),
        scratch_shapes=[pltpu.VMEM((1, cols), F32), pltpu.VMEM((1, cols), F32),
                        pltpu.VMEM((dk, cols), F32), pltpu.VMEM((dk, cols), F32)],
        compiler_params=_params(("parallel", "parallel", "arbitrary")),
        name="nsa_attn",
    )(qt, gates_t, cmp_kv, vct, qkv, vt, qkv, vt)


MOE_TM = 256
MOE_FFN_TM = 256


def _hidden_chunks(f):
    chunks, c0 = [], 0
    while c0 < f:
        cs = min(MXU_WIDTH_V7X, f - c0)
        chunks.append((c0, cs))
        c0 += cs
    return tuple(chunks)


def _swiglu(x, wgu_ref, wd_ref, h_s, chunks):
    for c0, cs in chunks:
        gu = jnp.dot(x, wgu_ref[0, :, 2 * c0:2 * c0 + 2 * cs], preferred_element_type=F32)
        g, u = gu[:, :cs], gu[:, cs:]
        h_s[:, c0:c0 + cs] = (g * jax.nn.sigmoid(g) * u).astype(BF16)
    return jnp.dot(h_s[...], wd_ref[0], preferred_element_type=F32)


W_CHUNK_ROWS = 128
W_STAGE_SLOTS = 4
MOE_FFN_VMEM_LIMIT_BYTES = 56 * 1024 * 1024


def _moe_ffn_kernel(blk_e_ref, n_used_ref, first_ref, nxt_ref, slot_ref, clo_ref, chi_ref,
                    x_ref, wg_hbm, wu_hbm, wd_hbm, o_ref,
                    wgu_buf, wd_buf, stg_gu, stg_d, sem_gu, sem_d, h_s, *, chunks):
    i = pl.program_id(0)
    rows = W_CHUNK_ROWS
    n_g = wg_hbm.shape[1] // rows
    n_gu = 2 * n_g
    n_d = wd_hbm.shape[1] // rows

    def gu_copy(src, e, r0, s):
        return pltpu.make_async_copy(src.at[e, pl.ds(r0, rows), :], stg_gu.at[s], sem_gu.at[s])

    def d_copy(e, r0, s):
        return pltpu.make_async_copy(wd_hbm.at[e, pl.ds(r0, rows), :], stg_d.at[s], sem_d.at[s])

    def gu_start(e, c):
        s = lax.rem(c, W_STAGE_SLOTS)

        @pl.when(c < n_g)
        def _():
            gu_copy(wg_hbm, e, pl.multiple_of(c * rows, rows), s).start()

        @pl.when(c >= n_g)
        def _():
            gu_copy(wu_hbm, e, pl.multiple_of((c - n_g) * rows, rows), s).start()

    def d_start(e, c):
        d_copy(e, pl.multiple_of(c * rows, rows), lax.rem(c, W_STAGE_SLOTS)).start()

    def start_prefetch(e):
        for c in range(W_STAGE_SLOTS):
            gu_start(e, jnp.int32(c))
            d_start(e, jnp.int32(c))

    def convert(e, dst, lo, hi):
        def gu_body(c, carry):
            s = lax.rem(c, W_STAGE_SLOTS)
            gu_copy(wg_hbm, 0, 0, s).wait()
            r0 = pl.multiple_of(lax.rem(c, n_g) * rows, rows)

            @pl.when(c < n_g)
            def _():
                for c0, cs in chunks:
                    wgu_buf[dst, pl.ds(r0, rows), 2 * c0:2 * c0 + cs] = stg_gu[s, :, c0:c0 + cs].astype(BF16)

            @pl.when(c >= n_g)
            def _():
                for c0, cs in chunks:
                    wgu_buf[dst, pl.ds(r0, rows), 2 * c0 + cs:2 * c0 + 2 * cs] = stg_gu[s, :, c0:c0 + cs].astype(BF16)

            @pl.when(c + W_STAGE_SLOTS < n_gu)
            def _():
                gu_start(e, c + W_STAGE_SLOTS)

            return carry

        lax.fori_loop(jnp.clip(lo, 0, n_gu), jnp.clip(hi, 0, n_gu), gu_body, 0)

        def d_body(c, carry):
            s = lax.rem(c, W_STAGE_SLOTS)
            d_copy(0, 0, s).wait()
            wd_buf[dst, pl.ds(pl.multiple_of(c * rows, rows), rows), :] = stg_d[s].astype(BF16)

            @pl.when(c + W_STAGE_SLOTS < n_d)
            def _():
                d_start(e, c + W_STAGE_SLOTS)

            return carry

        lax.fori_loop(jnp.clip(lo - n_gu, 0, n_d), jnp.clip(hi - n_gu, 0, n_d), d_body, 0)

    @pl.when(i == 0)
    def _():
        start_prefetch(blk_e_ref[0])
        convert(blk_e_ref[0], 0, 0, n_gu + n_d)

    used = i < n_used_ref[0]

    @pl.when(used)
    def _():
        nxt = nxt_ref[i]
        cur = slot_ref[i]

        @pl.when((first_ref[i] == 1) & (nxt >= 0))
        def _():
            start_prefetch(nxt)

        o_ref[...] = _swiglu(x_ref[...], wgu_buf.at[pl.ds(cur, 1)], wd_buf.at[pl.ds(cur, 1)], h_s,
                             chunks).astype(o_ref.dtype)

        @pl.when(nxt >= 0)
        def _():
            convert(nxt, 1 - cur, clo_ref[i], chi_ref[i])

    @pl.when(jnp.logical_not(used))
    def _():
        o_ref[...] = jnp.zeros(o_ref.shape, o_ref.dtype)


def _pack_gate_up_kernel(wg_ref, wu_ref, o_ref, *, chunks):
    for c0, cs in chunks:
        o_ref[0, :, 2 * c0:2 * c0 + cs] = wg_ref[0, :, c0:c0 + cs].astype(o_ref.dtype)
        o_ref[0, :, 2 * c0 + cs:2 * c0 + 2 * cs] = wu_ref[0, :, c0:c0 + cs].astype(o_ref.dtype)


PACK_ROWS = 512


def _pack_gate_up(w_gate, w_up):
    E, D, F = w_gate.shape
    rows = min(PACK_ROWS, D)
    spec = pl.BlockSpec((1, rows, F), lambda e, r: (e, r, 0))
    return pl.pallas_call(
        functools.partial(_pack_gate_up_kernel, chunks=_hidden_chunks(F)),
        grid=(E, D // rows),
        in_specs=[spec, spec],
        out_specs=pl.BlockSpec((1, rows, 2 * F), lambda e, r: (e, r, 0)),
        out_shape=jax.ShapeDtypeStruct((E, D, 2 * F), BF16),
        compiler_params=_params(("parallel", "parallel")),
        name="pack_gate_up",
    )(w_gate, w_up)


def _moe_routed(xb, eid_t, rank_t, counts, w_gate, w_up, w_down):
    T, D = xb.shape
    K = eid_t.shape[0]
    E, F, _ = w_down.shape
    tm = MOE_FFN_TM
    n = T * K
    n_blk = -(-n // tm) + E
    i32 = jnp.int32

    padded = (counts + tm - 1) // tm * tm
    pad_end = jnp.cumsum(padded)
    pad_start = pad_end - padded
    start_t = jnp.zeros_like(eid_t)
    for e in range(E):
        start_t = jnp.where(eid_t == e, pad_start[e], start_t)
    dest = (start_t + rank_t).reshape(n)
    tok = jnp.zeros((n_blk * tm,), i32).at[dest].set(jnp.tile(jnp.arange(T, dtype=i32), K),
                                                     unique_indices=True, mode="promise_in_bounds")
    n_used = (pad_end[-1] // tm).astype(i32).reshape(1)
    blk_start = jnp.arange(n_blk, dtype=i32) * tm
    blk_e = jnp.minimum(jnp.sum((pad_end[None, :] <= blk_start[:, None]).astype(i32), axis=1), E - 1)

    blk_i = jnp.arange(n_blk, dtype=i32)
    used = blk_i < n_used[0]
    onehot_e = blk_e[:, None] == jnp.arange(E, dtype=i32)[None, :]
    lookup = lambda tbl: jnp.sum(jnp.where(onehot_e, tbl[None, :], 0), axis=1)
    first_blk = lookup(pad_start) // tm
    n_blk_e = jnp.maximum(lookup(padded) // tm, 1)
    first = (used & (blk_i == first_blk)).astype(i32)
    slot = lax.rem(jnp.cumsum(first) - 1, 2).astype(i32)
    nxt_blk = lookup(pad_end) // tm
    nxt_e = jnp.sum(jnp.where(blk_i[None, :] == nxt_blk[:, None], blk_e[None, :], 0), axis=1)
    nxt_e = jnp.where(used & (nxt_blk < n_used[0]), nxt_e, -1).astype(i32)
    n_chunks = (2 * D + F) // W_CHUNK_ROWS
    step = blk_i - first_blk
    c_lo = (n_chunks * step // n_blk_e).astype(i32)
    c_hi = (n_chunks * (step + 1) // n_blk_e).astype(i32)

    xs = xb.at[tok].get(mode="promise_in_bounds")

    hbm = pl.BlockSpec(memory_space=pl.ANY)
    y = pl.pallas_call(
        functools.partial(_moe_ffn_kernel, chunks=_hidden_chunks(F)),
        grid_spec=pltpu.PrefetchScalarGridSpec(
            num_scalar_prefetch=7,
            grid=(n_blk,),
            in_specs=[pl.BlockSpec((tm, D), lambda i, *_: (i, 0)), hbm, hbm, hbm],
            out_specs=pl.BlockSpec((tm, D), lambda i, *_: (i, 0)),
            scratch_shapes=[pltpu.VMEM((2, D, 2 * F), BF16), pltpu.VMEM((2, F, D), BF16),
                            pltpu.VMEM((W_STAGE_SLOTS, W_CHUNK_ROWS, F), F32),
                            pltpu.VMEM((W_STAGE_SLOTS, W_CHUNK_ROWS, D), F32),
                            pltpu.SemaphoreType.DMA((W_STAGE_SLOTS,)), pltpu.SemaphoreType.DMA((W_STAGE_SLOTS,)),
                            pltpu.VMEM((tm, F), BF16)]),
        out_shape=jax.ShapeDtypeStruct((n_blk * tm, D), BF16),
        compiler_params=pltpu.CompilerParams(dimension_semantics=("arbitrary",),
                                             vmem_limit_bytes=MOE_FFN_VMEM_LIMIT_BYTES),
        name="moe_ffn",
    )(blk_e, n_used, first, nxt_e, slot, c_lo, c_hi, xs, w_gate, w_up, w_down)

    return y.at[dest].get(mode="promise_in_bounds").reshape(K, T, D)


def _layernorm_rows(z, g, b):
    mu = jnp.mean(z, -1, keepdims=True)
    zc = z - mu
    var = jnp.mean(jnp.square(zc), -1, keepdims=True)
    return zc * lax.rsqrt(var + LN_EPS) * g + b


def _merge_kernel(x_ref, yn_ref, ym_ref, wg_ref, wbn_ref, wbm_ref, o_ref):
    x, yn, ym = x_ref[...], yn_ref[...], ym_ref[...]
    d = o_ref.shape[1]
    for c0 in range(0, d, MXU_WIDTH_V7X):
        cs = slice(c0, c0 + MXU_WIDTH_V7X)
        gn = jax.nn.sigmoid(jnp.dot(x, wg_ref[:, cs], preferred_element_type=F32))
        gm = jax.nn.sigmoid(jnp.dot(x, wg_ref[:, d + c0:d + c0 + MXU_WIDTH_V7X], preferred_element_type=F32))
        bn = jnp.dot(yn, wbn_ref[:, cs], preferred_element_type=F32)
        bm = jnp.dot(ym, wbm_ref[:, cs], preferred_element_type=F32)
        o_ref[:, cs] = (gn * bn + gm * bm).astype(o_ref.dtype)


def _merge(xb, y_nsa, y_mla, w_g, wb_nsa, wb_mla):
    T, D = xb.shape
    tm = PROJ_TM
    row = lambda a: pl.BlockSpec((tm, a.shape[1]), lambda i: (i, 0))
    return pl.pallas_call(
        _merge_kernel,
        grid=(T // tm,),
        in_specs=[row(xb), row(y_nsa), row(y_mla), _resident(w_g.shape), _resident(wb_nsa.shape),
                  _resident(wb_mla.shape)],
        out_specs=pl.BlockSpec((tm, D), lambda i: (i, 0)),
        out_shape=jax.ShapeDtypeStruct((T, D), BF16),
        compiler_params=_params(("parallel",)),
        name="merge",
    )(xb, y_nsa, y_mla, w_g, wb_nsa, wb_mla)


def _rank_before(v):
    n = v.shape[0]
    j_io = lax.broadcasted_iota(jnp.int32, v.shape, 0)
    cnt = jnp.zeros(v.shape, jnp.int32)
    for i in range(n):
        row = v[i:i + 1, :]
        cnt = cnt + ((row > v) | ((row == v) & (j_io > i))).astype(jnp.int32)
    return cnt


def _route_tile(logits_t, rb, cnt_s):
    E, tm = logits_t.shape
    per = E // N_EXPERT_GROUPS
    scores = jax.nn.sigmoid(logits_t)
    biased = scores + rb
    grp = []
    for g in range(N_EXPERT_GROUPS):
        blk = biased[g * per:(g + 1) * per, :]
        m1 = jnp.max(blk, axis=0, keepdims=True)
        eq = blk == m1
        n_eq = jnp.sum(eq.astype(F32), axis=0, keepdims=True)
        m2 = jnp.max(jnp.where(eq, NEG_INF, blk), axis=0, keepdims=True)
        grp.append(m1 + jnp.where(n_eq >= 2.0, m1, m2))
    gsel = _rank_before(jnp.concatenate(grp, axis=0)) < TOPK_GROUPS
    masked = jnp.concatenate(
        [jnp.where(gsel[g:g + 1, :], biased[g * per:(g + 1) * per, :], NEG_INF) for g in range(N_EXPERT_GROUPS)],
        axis=0)
    sel = _rank_before(masked) < TOP_K
    w_sel = jnp.where(sel, scores, 0.0)
    w_sel = w_sel / jnp.sum(w_sel, axis=0, keepdims=True) * ROUTED_SCALE
    tri = jnp.where(lax.broadcasted_iota(jnp.int32, (E, E), 1) <= lax.broadcasted_iota(jnp.int32, (E, E), 0),
                    1.0, 0.0).astype(BF16)
    sel_b = jnp.where(sel, 1.0, 0.0).astype(BF16)
    slot = jnp.dot(tri, sel_b, preferred_element_type=F32)
    upper = jnp.where(lax.broadcasted_iota(jnp.int32, (tm, tm), 0) <= lax.broadcasted_iota(jnp.int32, (tm, tm), 1),
                      1.0, 0.0).astype(BF16)
    incl = jnp.dot(sel_b, upper, preferred_element_type=F32)
    pos = cnt_s[...] + incl - 1.0
    cnt_s[...] = cnt_s[...] + incl[:, tm - 1:tm]
    e_io = lax.broadcasted_iota(jnp.int32, (E, tm), 0)
    ids, wts, ranks = [], [], []
    for k in range(TOP_K):
        mk = sel & (slot == float(k + 1))
        ids.append(jnp.sum(jnp.where(mk, e_io, 0), axis=0, keepdims=True))
        wts.append(jnp.sum(jnp.where(mk, w_sel, 0.0), axis=0, keepdims=True))
        ranks.append(jnp.sum(jnp.where(mk, pos, 0.0), axis=0, keepdims=True).astype(jnp.int32))
    pad = 8 - TOP_K
    ids.append(jnp.zeros((pad, tm), jnp.int32))
    wts.append(jnp.zeros((pad, tm), F32))
    ranks.append(jnp.zeros((pad, tm), jnp.int32))
    return jnp.concatenate(ids, axis=0), jnp.concatenate(wts, axis=0), jnp.concatenate(ranks, axis=0)


def _out_ln1_kernel(m_ref, x_ref, wo_ref, g_ref, b_ref, wrt_ref, rb_ref,
                    x1_ref, x1b_ref, eid_ref, wts_ref, rank_ref, cnt_ref, cnt_s):
    @pl.when(pl.program_id(0) == 0)
    def _():
        cnt_s[...] = jnp.zeros(cnt_s.shape, F32)

    m = m_ref[...]
    d = x1_ref.shape[1]
    for c0 in range(0, d, MXU_WIDTH_V7X):
        cs = slice(c0, c0 + MXU_WIDTH_V7X)
        x1_ref[:, cs] = DN_ALPHA * x_ref[:, cs] + jnp.dot(m, wo_ref[:, cs], preferred_element_type=F32)
    x1 = _layernorm_rows(x1_ref[...], g_ref[...], b_ref[...])
    x1_ref[...] = x1
    x1b = x1.astype(BF16)
    x1b_ref[...] = x1b
    logits_t = _dot_nt(wrt_ref[...], x1b)
    eid_ref[...], wts_ref[...], rank_ref[...] = _route_tile(logits_t, rb_ref[...], cnt_s)
    cnt_ref[...] = cnt_s[...]


def _out_ln1(merged, x2, w_out, g, b, w_router_t, router_b):
    T, D = x2.shape
    E = w_router_t.shape[0]
    tm = MOE_TM
    row = lambda w: pl.BlockSpec((tm, w), lambda i: (i, 0))
    col = pl.BlockSpec((8, tm), lambda i: (0, i))
    return pl.pallas_call(
        _out_ln1_kernel,
        grid=(T // tm,),
        in_specs=[row(D), row(D), _resident(w_out.shape), _resident(g.shape), _resident(b.shape),
                  _resident(w_router_t.shape), _resident(router_b.shape)],
        out_specs=[row(D), row(D), col, col, col, pl.BlockSpec((E, 1), lambda i: (0, 0))],
        out_shape=[jax.ShapeDtypeStruct((T, D), F32), jax.ShapeDtypeStruct((T, D), BF16),
                   jax.ShapeDtypeStruct((8, T), jnp.int32), jax.ShapeDtypeStruct((8, T), F32),
                   jax.ShapeDtypeStruct((8, T), jnp.int32), jax.ShapeDtypeStruct((E, 1), F32)],
        scratch_shapes=[pltpu.VMEM((E, 1), F32)],
        compiler_params=_params(("arbitrary",)),
        name="out_ln1",
    )(merged, x2, w_out, g, b, w_router_t, router_b)


def _ffn_ln2_kernel(x1_ref, yk_ref, w_ref, sgu_ref, sd_ref, g_ref, b_ref, x2_ref, x2b_ref, h_s, *, chunks):
    x1 = x1_ref[...]
    acc = _swiglu(x1.astype(BF16), sgu_ref, sd_ref, h_s, chunks)
    w = w_ref[...]
    for k in range(yk_ref.shape[0]):
        acc = acc + yk_ref[k].astype(F32) * w[:, k:k + 1]
    x2 = _layernorm_rows(DN_ALPHA * x1 + acc, g_ref[...], b_ref[...])
    x2_ref[...] = x2
    x2b_ref[...] = x2.astype(BF16)


FFN_LN2_VMEM_LIMIT_BYTES = 56 * 1024 * 1024


def _ffn_ln2(x1, yk, wts, sh_gate_up, sh_down, g, b):
    T, D = x1.shape
    K = yk.shape[0]
    tm = MOE_TM
    F = sh_down.shape[1]
    row = lambda w: pl.BlockSpec((tm, w), lambda i: (i, 0))
    return pl.pallas_call(
        functools.partial(_ffn_ln2_kernel, chunks=_hidden_chunks(F)),
        grid=(T // tm,),
        in_specs=[row(D), pl.BlockSpec((K, tm, D), lambda i: (0, i, 0)), row(wts.shape[1]),
                  _resident(sh_gate_up.shape), _resident(sh_down.shape), _resident(g.shape), _resident(b.shape)],
        out_specs=[row(D), row(D)],
        out_shape=[jax.ShapeDtypeStruct((T, D), F32), jax.ShapeDtypeStruct((T, D), BF16)],
        scratch_shapes=[pltpu.VMEM((tm, F), BF16)],
        compiler_params=pltpu.CompilerParams(dimension_semantics=("parallel",),
                                             vmem_limit_bytes=FFN_LN2_VMEM_LIMIT_BYTES),
        name="ffn_ln2",
    )(x1, yk, wts, sh_gate_up, sh_down, g, b)


def _ple_kernel(x2_ref, x2b_ref, p_ref, wpg_ref, wpp_ref, o_ref):
    x2b = x2b_ref[...]
    pb = p_ref[...].astype(BF16)
    d = o_ref.shape[1]
    for c0 in range(0, d, MXU_WIDTH_V7X):
        cs = slice(c0, c0 + MXU_WIDTH_V7X)
        gate = jax.nn.sigmoid(jnp.dot(x2b, wpg_ref[:, cs], preferred_element_type=F32))
        pp = jnp.dot(pb, wpp_ref[:, cs], preferred_element_type=F32)
        o_ref[:, cs] = x2_ref[:, cs] + gate * pp


def _ple(x2, x2b, p2, w_pg, w_pp):
    T, D = x2.shape
    tm = PROJ_TM
    row = lambda w: pl.BlockSpec((tm, w), lambda i: (i, 0))
    return pl.pallas_call(
        _ple_kernel,
        grid=(T // tm,),
        in_specs=[row(D), row(D), row(p2.shape[1]), _resident(w_pg.shape), _resident(w_pp.shape)],
        out_specs=row(D),
        out_shape=jax.ShapeDtypeStruct((T, D), F32),
        compiler_params=_params(("parallel",)),
        name="ple",
    )(x2, x2b, p2, w_pg, w_pp)


def _attention_branches(xb, S, w_in, pe_k, pe_v, wk1, wk2, wv1, wv2, q_norm, w_uq, kv_norm, w_uk, w_uv):
    T, D = xb.shape
    B = T // S
    c0 = W_NSA_Q
    c1 = c0 + W_NSA_KV
    c2 = c1 + W_NSA_GATE
    c3 = c2 + W_MLA_CQ
    c4 = c3 + W_MLA_CKV
    c5 = c4 + W_MLA_KR
    G, dk = NSA_GROUPS, NSA_DK
    pos = jnp.arange(S)
    cos128, sin128 = _rope_tables_128(pos)
    cos64, slo64, shi64 = _rope_tables_64(pos)

    qkv = _nsa_proj(xb, w_in[:, :c1].astype(BF16), cos128, sin128, S)

    pad_kr = jnp.zeros((D, LANES - W_MLA_KR), w_in.dtype)
    pad_g = jnp.zeros((D, LANES - W_NSA_GATE), w_in.dtype)
    w_m = jnp.concatenate([w_in[:, c2:c4], w_in[:, c4:c5], pad_kr, w_in[:, c1:c2], pad_g], axis=1).astype(BF16)
    H = MLA_HEADS
    wq = w_uq.reshape(MLA_Q_RANK, H, MLA_NOPE + MLA_ROPE)
    wq = jnp.concatenate([wq, jnp.zeros((MLA_Q_RANK, H, 2 * LANES - MLA_NOPE - MLA_ROPE), wq.dtype)], axis=-1)
    wq = wq.reshape(MLA_Q_RANK, H * 2 * LANES).astype(BF16)
    wukv = jnp.concatenate([w_uk, w_uv], axis=1).astype(BF16)
    q_m, kv_m, kr_m, gates = _mla_in(xb, w_m, q_norm.reshape(1, -1), kv_norm.reshape(1, -1), wq, wukv,
                                     cos64, slo64, shi64, S)

    n_chunks = S // CMP_STRIDE
    kvc = qkv[:, NSA_HEADS * dk:(NSA_HEADS + 2 * G) * dk]
    zc = kvc.reshape(B, n_chunks, CMP_STRIDE, 2, G, dk).transpose(0, 3, 4, 1, 2, 5)
    zc = zc.reshape(B, 2, G, n_chunks, CMP_STRIDE * dk)
    pe = jnp.stack([pe_k.reshape(-1), pe_v.reshape(-1)])
    pe = jnp.broadcast_to(pe[:, None, :], (2, 8, pe.shape[-1])).astype(BF16)
    half = CMP_STRIDE * dk
    w1 = jnp.stack([jnp.concatenate([wk1[:half], wk1[half:]], axis=1),
                    jnp.concatenate([wv1[:half], wv1[half:]], axis=1)]).astype(BF16)
    w2 = jnp.stack([wk2, wv2]).astype(BF16)
    pos_c = jnp.arange(n_chunks) * CMP_STRIDE + (CMP_BLK - 1)
    cos_c, sin_c = _rope_tables_128(pos_c)
    cmp_kv = _nsa_compress(zc, pe, w1, w2, cos_c, sin_c)

    g3 = gates[:, :W_NSA_GATE].reshape(B, S, G, NSA_HPG * 3).transpose(0, 2, 3, 1)
    g3 = jnp.pad(g3, ((0, 0), (0, 0), (0, LANES - NSA_HPG * 3), (0, 0)))

    qkv3 = qkv.reshape(B, S, -1)
    qt = jnp.swapaxes(qkv3[:, :, :NSA_HEADS * dk], 1, 2)
    v0 = NSA_HEADS + 2 * G
    v_sw = jnp.concatenate([qkv3[:, :, (v0 + G) * dk:(v0 + 2 * G) * dk],
                            qkv3[:, :, (v0 + 3 * G) * dk:(v0 + 4 * G) * dk]], axis=-1)
    vt = jnp.swapaxes(v_sw, 1, 2)
    vct = jnp.swapaxes(cmp_kv[:, 1], 2, 3)
    y_nsa = _nsa_attention(qkv3, qt, g3, cmp_kv, vct, vt)
    kv_m = kv_m.reshape(B, S, -1)
    qt_m = jnp.swapaxes(q_m.reshape(B, S, -1), 1, 2)
    vt_m = jnp.swapaxes(kv_m[:, :, MLA_HEADS * MLA_NOPE:], 1, 2)
    y_mla = _mla_attention(qt_m, kv_m, kr_m.reshape(B, S, -1), vt_m)
    return y_nsa, y_mla


def kernel(x, p, w_in, nsa_pe_k, nsa_pe_v, nsa_cmp_k_w1, nsa_cmp_k_w2, nsa_cmp_v_w1, nsa_cmp_v_w2, mla_q_norm, mla_w_uq, mla_kv_norm, mla_w_uk, mla_w_uv, w_branch_nsa, w_branch_mla, w_out, ln1_g, ln1_b, router_w, router_b, exp_w_gate, exp_w_up, exp_w_down, sh_w_gate, sh_w_up, sh_w_down, ln2_g, ln2_b, ple_w_proj, ple_w_gate):
    B, S, D = x.shape
    c5 = D_IN - W_MERGE
    bs = B // BATCH_SPLITS
    T = bs * S
    for i in range(DEPTH):
        w_attn = (w_in[i], nsa_pe_k[i], nsa_pe_v[i], nsa_cmp_k_w1[i], nsa_cmp_k_w2[i], nsa_cmp_v_w1[i],
                  nsa_cmp_v_w2[i], mla_q_norm[i], mla_w_uq[i], mla_kv_norm[i], mla_w_uk[i], mla_w_uv[i])
        w_gm = w_in[i][:, c5:].astype(BF16)
        wb_nsa, wb_mla, wo = w_branch_nsa[i].astype(BF16), w_branch_mla[i].astype(BF16), w_out[i].astype(BF16)
        w_rt, b_rt = router_w[i].T.astype(BF16), router_b[i].reshape(N_EXPERTS, 1).astype(F32)
        sh_gu = _pack_gate_up(sh_w_gate[i][None], sh_w_up[i][None])
        sh_d = sh_w_down[i][None].astype(BF16)
        w_pg, w_pp = ple_w_gate[i].astype(BF16), ple_w_proj[i].astype(BF16)
        outs = []
        for h in range(BATCH_SPLITS):
            x2 = x[h * bs:(h + 1) * bs].reshape(T, D)
            p2 = p[i][h * bs:(h + 1) * bs].reshape(T, PLE_DIM)
            xb = x2.astype(BF16)
            y_nsa, y_mla = _attention_branches(xb, S, *w_attn)
            merged = _merge(xb, y_nsa.reshape(T, -1), y_mla.reshape(T, -1), w_gm, wb_nsa, wb_mla)
            x1, x1b, eid_t, wts_t, rank_t, counts = _out_ln1(
                merged, x2, wo, ln1_g[i].reshape(1, D), ln1_b[i].reshape(1, D), w_rt, b_rt)
            yk = _moe_routed(x1b, eid_t[:TOP_K], rank_t[:TOP_K], counts.reshape(N_EXPERTS).astype(jnp.int32),
                             exp_w_gate[i], exp_w_up[i], exp_w_down[i])
            x2n, x2b = _ffn_ln2(x1, yk, wts_t.T, sh_gu, sh_d, ln2_g[i].reshape(1, D), ln2_b[i].reshape(1, D))
            outs.append(_ple(x2n, x2b, p2, w_pg, w_pp).reshape(bs, S, D))
        x = jnp.concatenate(outs, axis=0)
    return x
```

```python
import functools
import math

import jax
import jax.numpy as jnp
from jax import lax
from jax.experimental import pallas as pl
from jax.experimental.pallas import tpu as pltpu

D_MODEL = 2048
BATCH = 8
SEQ = 4096
DEPTH = 1

PLE_DIM = 256
ROPE_THETA = 10000.0
LN_EPS = 1e-5
RMS_EPS = 1e-6
NEG_INF = -1e30

NSA_HEADS = 8
NSA_GROUPS = 2
NSA_HPG = NSA_HEADS // NSA_GROUPS
NSA_DK = 128
NSA_DV = 128
CMP_BLK = 32
CMP_STRIDE = 16
CMP_HID = 256
SLC_BLK = 64
N_SEL = 16
WIN = 512
FORCE_SCORE = 1e4

MLA_HEADS = 8
MLA_Q_RANK = 768
MLA_KV_RANK = 512
MLA_NOPE = 128
MLA_ROPE = 64
MLA_DV = 128

N_EXPERTS = 64
TOP_K = 6
N_EXPERT_GROUPS = 8
TOPK_GROUPS = 4
D_EXPERT = 1408
ROUTED_SCALE = 2.5

W_NSA_Q = NSA_HEADS * NSA_DK
W_NSA_KV = 3 * NSA_GROUPS * (NSA_DK + NSA_DV)
W_NSA_GATE = 3 * NSA_HEADS
W_MLA_CQ = MLA_Q_RANK
W_MLA_CKV = MLA_KV_RANK
W_MLA_KR = MLA_ROPE
W_MERGE = 2 * D_MODEL
D_IN = W_NSA_Q + W_NSA_KV + W_NSA_GATE + W_MLA_CQ + W_MLA_CKV + W_MLA_KR + W_MERGE

DN_ALPHA = (2.0 * DEPTH) ** 0.25
DN_BETA = (8.0 * DEPTH) ** -0.25

V7X_VMEM_LIMIT_BYTES = 48 * 1024 * 1024
LANES = 128
MXU_WIDTH_V7X = 256
PROJ_TM = 512
BATCH_SPLITS = 1

BF16 = jnp.bfloat16
F32 = jnp.float32


def _params(sem):
    return pltpu.CompilerParams(dimension_semantics=sem, vmem_limit_bytes=V7X_VMEM_LIMIT_BYTES)


def _resident(shape):
    return pl.BlockSpec(shape, lambda *_: (0,) * len(shape), pipeline_mode=pl.Buffered(1))


def _rope_tables_128(pos):
    half = LANES // 2
    inv = jnp.power(ROPE_THETA, -jnp.arange(half, dtype=F32) * 2.0 / LANES)
    ang = pos.astype(F32)[:, None] * inv[None, :]
    cos, sin = jnp.cos(ang), jnp.sin(ang)
    return jnp.concatenate([cos, cos], -1), jnp.concatenate([-sin, sin], -1)


def _rope_tables_64(pos):
    half = MLA_ROPE // 2
    inv = jnp.power(ROPE_THETA, -jnp.arange(half, dtype=F32) * 2.0 / MLA_ROPE)
    ang = pos.astype(F32)[:, None] * inv[None, :]
    cos, sin = jnp.cos(ang), jnp.sin(ang)
    zero = jnp.zeros_like(sin)
    return (jnp.concatenate([cos, cos, cos, cos], -1),
            jnp.concatenate([-sin, zero, -sin, zero], -1),
            jnp.concatenate([zero, sin, zero, sin], -1))


def _rope128(z, cos, sin):
    return z * cos + pltpu.roll(z, LANES // 2, 1) * sin


def _rope64(z, cos, sin_lo, sin_hi):
    return z * cos + pltpu.roll(z, LANES - MLA_ROPE // 2, 1) * sin_lo + pltpu.roll(z, MLA_ROPE // 2, 1) * sin_hi


def _nsa_proj_kernel(x_ref, w_ref, cos_ref, sin_ref, o_ref, *, rope_heads, n_q_heads, scale):
    x = x_ref[...]
    cos, sin = cos_ref[...], sin_ref[...]
    n_heads = w_ref.shape[1] // LANES
    for t in range(n_heads // 2):
        c0 = t * MXU_WIDTH_V7X
        z = jnp.dot(x, w_ref[:, c0:c0 + MXU_WIDTH_V7X], preferred_element_type=F32)
        for hh in range(2):
            head = 2 * t + hh
            zh = z[:, hh * LANES:(hh + 1) * LANES]
            if head in rope_heads:
                zh = _rope128(zh, cos, sin)
            if head < n_q_heads:
                zh = zh * scale
            o_ref[:, head * LANES:(head + 1) * LANES] = zh.astype(o_ref.dtype)


def _nsa_proj(xb, w, cos, sin, S):
    T, D = xb.shape
    n = w.shape[1]
    tm = min(PROJ_TM, S)
    n_pos = S // tm
    kv0 = NSA_HEADS
    per = NSA_GROUPS * 2
    rope_heads = tuple(range(NSA_HEADS)) + tuple(kv0 + per * br + g for br in (1, 2) for g in range(NSA_GROUPS))
    return pl.pallas_call(
        functools.partial(_nsa_proj_kernel, rope_heads=rope_heads, n_q_heads=NSA_HEADS, scale=NSA_DK ** -0.5),
        grid=(T // tm,),
        in_specs=[pl.BlockSpec((tm, D), lambda i: (i, 0)),
                  _resident((D, n)),
                  pl.BlockSpec((tm, LANES), lambda i: (i % n_pos, 0)),
                  pl.BlockSpec((tm, LANES), lambda i: (i % n_pos, 0))],
        out_specs=pl.BlockSpec((tm, n), lambda i: (i, 0)),
        out_shape=jax.ShapeDtypeStruct((T, n), BF16),
        compiler_params=_params(("parallel",)),
        name="nsa_proj",
    )(xb, w, cos, sin)


def _rmsnorm_rows(z, g):
    return z * lax.rsqrt(jnp.mean(jnp.square(z), -1, keepdims=True) + RMS_EPS) * g


def _mla_in_kernel(x_ref, w_ref, gq_ref, gkv_ref, wuq_ref, wukv_ref, cos_ref, slo_ref, shi_ref,
                   q_ref, kv_ref, kr_ref, gate_ref, *, scale):
    x = x_ref[...]
    cos, slo, shi = cos_ref[...], slo_ref[...], shi_ref[...]
    c1 = MLA_Q_RANK
    c2 = c1 + MLA_KV_RANK
    cq = jnp.dot(x, w_ref[:, :c1], preferred_element_type=F32)
    cqn = _rmsnorm_rows(cq, gq_ref[...]).astype(BF16)
    ckv = jnp.dot(x, w_ref[:, c1:c2], preferred_element_type=F32)
    ckvn = _rmsnorm_rows(ckv, gkv_ref[...]).astype(BF16)
    tail = jnp.dot(x, w_ref[:, c2:c2 + 2 * LANES], preferred_element_type=F32)
    kr_ref[...] = _rope64(tail[:, :LANES], cos, slo, shi).astype(kr_ref.dtype)
    gate_ref[...] = jax.nn.sigmoid(tail[:, LANES:])
    for h in range(MLA_HEADS):
        c0 = h * MXU_WIDTH_V7X
        z = jnp.dot(cqn, wuq_ref[:, c0:c0 + MXU_WIDTH_V7X], preferred_element_type=F32)
        q_ref[:, c0:c0 + LANES] = (z[:, :LANES] * scale).astype(q_ref.dtype)
        q_ref[:, c0 + LANES:c0 + 2 * LANES] = (_rope64(z[:, LANES:], cos, slo, shi) * scale).astype(q_ref.dtype)
    n_kv = wukv_ref.shape[1]
    for c0 in range(0, n_kv, MXU_WIDTH_V7X):
        kv_ref[:, c0:c0 + MXU_WIDTH_V7X] = jnp.dot(
            ckvn, wukv_ref[:, c0:c0 + MXU_WIDTH_V7X], preferred_element_type=F32).astype(kv_ref.dtype)


def _mla_in(xb, w, gq, gkv, wuq, wukv, cos, slo, shi, S):
    T, D = xb.shape
    tm = min(PROJ_TM, S)
    n_pos = S // tm
    tbl = pl.BlockSpec((tm, LANES), lambda i: (i % n_pos, 0))
    nq, nkv = wuq.shape[1], wukv.shape[1]
    return pl.pallas_call(
        functools.partial(_mla_in_kernel, scale=(MLA_NOPE + MLA_ROPE) ** -0.5),
        grid=(T // tm,),
        in_specs=[pl.BlockSpec((tm, D), lambda i: (i, 0)),
                  _resident(w.shape), _resident(gq.shape), _resident(gkv.shape),
                  _resident(wuq.shape), _resident(wukv.shape), tbl, tbl, tbl],
        out_specs=[pl.BlockSpec((tm, nq), lambda i: (i, 0)),
                   pl.BlockSpec((tm, nkv), lambda i: (i, 0)),
                   pl.BlockSpec((tm, LANES), lambda i: (i, 0)),
                   pl.BlockSpec((tm, LANES), lambda i: (i, 0))],
        out_shape=[jax.ShapeDtypeStruct((T, nq), BF16), jax.ShapeDtypeStruct((T, nkv), BF16),
                   jax.ShapeDtypeStruct((T, LANES), BF16), jax.ShapeDtypeStruct((T, LANES), F32)],
        compiler_params=_params(("parallel",)),
        name="mla_in",
    )(xb, w, gq, gkv, wuq, wukv, cos, slo, shi)


MASK_BIG = 16384.0


def _softmax_reset(m_s, l_s, acc_s):
    m_s[...] = jnp.full(m_s.shape, NEG_INF, F32)
    l_s[...] = jnp.zeros(l_s.shape, F32)
    acc_s[...] = jnp.zeros(acc_s.shape, F32)


def _online_softmax_step(s, v, m_s, l_s, acc_s):
    m_prev = m_s[...]
    m_new = jnp.maximum(m_prev, jnp.max(s, axis=-1, keepdims=True))
    alpha = jnp.exp(m_prev - m_new)
    p = jnp.exp(s - m_new)
    l_s[...] = alpha * l_s[...] + jnp.sum(p, axis=-1, keepdims=True)
    acc_s[...] = alpha * acc_s[...] + jnp.dot(p.astype(BF16), v, preferred_element_type=F32)
    m_s[...] = m_new


def _online_softmax_step_t(st, vt, m_s, l_s, acc_s):
    m_prev = m_s[...]
    m_new = jnp.maximum(m_prev, jnp.max(st, axis=0, keepdims=True))
    alpha = jnp.exp(m_prev - m_new)
    p = jnp.exp(st - m_new)
    l_s[...] = alpha * l_s[...] + jnp.sum(p, axis=0, keepdims=True)
    acc_s[...] = alpha * acc_s[...] + jnp.dot(vt, p.astype(BF16), preferred_element_type=F32)
    m_s[...] = m_new


def _dot_nt(a, b):
    return lax.dot_general(a, b, (((1,), (1,)), ((), ())), preferred_element_type=F32)


MLA_TQ = 512
MLA_TK = 512


MLA_HPS = 2


def _mla_attn_kernel(qt_ref, kn_ref, kr_ref, vt_ref, o_ref, m_s, l_s, acc_s, *, tq, tk, hps):
    q0 = pl.program_id(2) * tq
    qts = [qt_ref[0, hh * 2 * LANES:(hh + 1) * 2 * LANES, :] for hh in range(hps)]
    _softmax_reset(m_s, l_s, acc_s)

    def tile(j, causal):
        k0 = pl.multiple_of(j * tk, tk)
        kr = kr_ref[0, pl.ds(k0, tk), :]
        for hh in range(hps):
            hs = slice(hh * LANES, (hh + 1) * LANES)
            ka = jnp.concatenate([kn_ref[0, pl.ds(k0, tk), hs], kr], axis=1)
            st = jnp.dot(ka, qts[hh], preferred_element_type=F32)
            if causal:
                kpos = k0 + lax.broadcasted_iota(jnp.int32, (tk, tq), 0)
                qpos = q0 + lax.broadcasted_iota(jnp.int32, (tk, tq), 1)
                st = jnp.where(kpos <= qpos, st, -MASK_BIG)
            _online_softmax_step_t(st, vt_ref[0, hs, pl.ds(k0, tk)], m_s.at[hh], l_s.at[hh], acc_s.at[hh])

    j_diag = q0 // tk

    def body(j, carry):
        tile(j, False)
        return carry

    lax.fori_loop(0, j_diag, body, 0)
    tile(j_diag, True)
    for hh in range(hps):
        o_ref[0, :, hh * LANES:(hh + 1) * LANES] = jnp.transpose(acc_s[hh] / l_s[hh]).astype(o_ref.dtype)


def _mla_attention(qt, kv, kr, vt):
    B, _, S = qt.shape
    H = MLA_HEADS
    hps = MLA_HPS
    tq = tk = min(MLA_TQ, S)
    return pl.pallas_call(
        functools.partial(_mla_attn_kernel, tq=tq, tk=tk, hps=hps),
        grid=(B, H // hps, S // tq),
        in_specs=[pl.BlockSpec((1, hps * 2 * LANES, tq), lambda b, h, i: (b, h, i)),
                  pl.BlockSpec((1, S, hps * LANES), lambda b, h, i: (b, 0, h)),
                  pl.BlockSpec((1, S, LANES), lambda b, h, i: (b, 0, 0)),
                  pl.BlockSpec((1, hps * LANES, S), lambda b, h, i: (b, h, 0))],
        out_specs=pl.BlockSpec((1, tq, hps * LANES), lambda b, h, i: (b, i, h)),
        out_shape=jax.ShapeDtypeStruct((B, S, H * MLA_DV), BF16),
        scratch_shapes=[pltpu.VMEM((hps, 1, tq), F32), pltpu.VMEM((hps, 1, tq), F32),
                        pltpu.VMEM((hps, MLA_DV, tq), F32)],
        compiler_params=_params(("parallel", "parallel", "arbitrary")),
        name="mla_attn",
    )(qt, kv, kr, vt)


def _gelu_tanh(x):
    return 0.5 * x * (1.0 + jnp.tanh(math.sqrt(2.0 / math.pi) * (x + 0.044715 * (x * x * x))))


def _compress_kernel(z_ref, pe_ref, w1_ref, w2_ref, cos_ref, sin_ref, o_ref, *, hid):
    which = pl.program_id(1)
    z = z_ref[0, 0, 0]
    n_chunks = z.shape[0]
    cd = z.shape[1]
    ab = jnp.dot(z, w1_ref[0], preferred_element_type=F32)
    a_part = ab[:, :hid]
    b_next = pltpu.roll(ab[:, hid:], n_chunks - 1, 0)
    pe_a = jnp.dot(pe_ref[0, :, :cd], w1_ref[0, :, :hid], preferred_element_type=F32)
    pe_b = jnp.dot(pe_ref[0, :, cd:], w1_ref[0, :, hid:], preferred_element_type=F32)
    h = _gelu_tanh(a_part + b_next + pe_a[0:1] + pe_b[0:1])
    out = jnp.dot(h.astype(BF16), w2_ref[0], preferred_element_type=F32)
    roped = _rope128(out, cos_ref[...], sin_ref[...])
    o_ref[0, 0, 0] = jnp.where(which == 0, roped, out).astype(o_ref.dtype)


def _nsa_compress(zc, pe, w1, w2, cos_c, sin_c):
    B, _, G, n_chunks, cd = zc.shape
    hid = w2.shape[1]
    d = w2.shape[2]
    return pl.pallas_call(
        functools.partial(_compress_kernel, hid=hid),
        grid=(B, 2, G),
        in_specs=[pl.BlockSpec((1, 1, 1, n_chunks, cd), lambda b, t, g: (b, t, g, 0, 0)),
                  pl.BlockSpec((1, 8, 2 * cd), lambda b, t, g: (t, 0, 0)),
                  pl.BlockSpec((1, cd, 2 * hid), lambda b, t, g: (t, 0, 0)),
                  pl.BlockSpec((1, hid, d), lambda b, t, g: (t, 0, 0)),
                  pl.BlockSpec((n_chunks, LANES), lambda b, t, g: (0, 0)),
                  pl.BlockSpec((n_chunks, LANES), lambda b, t, g: (0, 0))],
        out_specs=pl.BlockSpec((1, 1, 1, n_chunks, d), lambda b, t, g: (b, t, g, 0, 0)),
        out_shape=jax.ShapeDtypeStruct((B, 2, G, n_chunks, d), BF16),
        compiler_params=_params(("parallel", "parallel", "parallel")),
        name="nsa_compress",
    )(zc, pe, w1, w2, cos_c, sin_c)


NSA_TQ = 256
NSA_TK = 512


def _nsa_kernel(qt_ref, g_ref, kc_ref, vct_ref, ks_ref, vst_ref, kw_ref, vwt_ref, o_ref,
                m_s, l_s, acc_s, out_s, *, tq, tk, hpg, dk, n_cmp, n_slc):
    q0 = pl.program_id(2) * tq
    cols = hpg * tq
    i32 = jnp.int32
    qt = jnp.concatenate([qt_ref[0, h * dk:(h + 1) * dk, :] for h in range(hpg)], axis=1)
    gates = g_ref[0, 0]

    def gate_row(branch):
        return jnp.concatenate([gates[3 * h + branch:3 * h + branch + 1, :] for h in range(hpg)], axis=1)

    def qpos_of(n_rows):
        return q0 + lax.rem(lax.broadcasted_iota(i32, (n_rows, cols), 1), tq)

    n_pad = kc_ref.shape[3]
    sc = jnp.dot(kc_ref[0, 0, 0], qt, preferred_element_type=F32)
    n_io = lax.broadcasted_iota(i32, (n_pad, cols), 0)
    vis = (n_io * CMP_STRIDE + (CMP_BLK - 1) <= qpos_of(n_pad)) & (n_io < n_cmp)
    sc = jnp.where(vis, sc, NEG_INF)
    e = jnp.where(vis, jnp.exp(sc - jnp.max(sc, axis=0, keepdims=True)), 0.0)
    p = e / jnp.maximum(jnp.sum(e, axis=0, keepdims=True), 1e-30)
    o_cmp = jnp.dot(vct_ref[0, 0], p.astype(BF16), preferred_element_type=F32)
    out_s[...] = gate_row(0) * o_cmp

    imp = p[:, 0:tq]
    for h in range(1, hpg):
        imp = imp + p[:, h * tq:(h + 1) * tq]
    ratio = SLC_BLK // CMP_STRIDE
    span = CMP_BLK // CMP_STRIDE
    j_p = lax.broadcasted_iota(i32, (n_slc, n_pad), 0)
    c_p = lax.broadcasted_iota(i32, (n_slc, n_pad), 1)
    lo_c = ratio * j_p - (span - 1)
    pool = jnp.where((c_p >= lo_c) & (c_p < lo_c + ratio + span - 1), 1.0, 0.0).astype(BF16)
    hi = imp.astype(BF16)
    r1 = imp - hi.astype(F32)
    mid = r1.astype(BF16)
    lo = (r1 - mid.astype(F32)).astype(BF16)
    p_slc = (jnp.dot(pool, hi, preferred_element_type=F32) + jnp.dot(pool, mid, preferred_element_type=F32)
             + jnp.dot(pool, lo, preferred_element_type=F32))

    j_io = lax.broadcasted_iota(i32, (n_slc, tq), 0)
    qpos_l = q0 + lax.broadcasted_iota(i32, (n_slc, tq), 1)
    cur = qpos_l // SLC_BLK
    forced = (j_io == 0) | (j_io == cur) | (j_io == cur - 1)
    valid = j_io * SLC_BLK <= qpos_l
    score = jnp.where(forced, FORCE_SCORE, jnp.where(valid, p_slc, -FORCE_SCORE))
    bias = jnp.where(_rank_before(score) < N_SEL, 0.0, -MASK_BIG)
    if n_slc < LANES:
        bias = jnp.concatenate([bias, jnp.full((LANES - n_slc, tq), -MASK_BIG, F32)], axis=0)
    bias = bias.astype(BF16)
    qa = jnp.concatenate([qt, jnp.concatenate([bias] * hpg, axis=1)], axis=0)

    qpos_k = qpos_of(tk)
    k_io = lax.broadcasted_iota(i32, (tk, cols), 0)

    _softmax_reset(m_s, l_s, acc_s)

    def slc_tile(j, causal):
        k0 = pl.multiple_of(j * tk, tk)
        kblk = (k0 + lax.broadcasted_iota(i32, (tk, LANES), 0)) // SLC_BLK
        onehot = jnp.where(kblk == lax.broadcasted_iota(i32, (tk, LANES), 1), 1.0, 0.0).astype(BF16)
        ka = jnp.concatenate([ks_ref[0, pl.ds(k0, tk), :], onehot], axis=1)
        s = jnp.dot(ka, qa, preferred_element_type=F32)
        if causal:
            s = jnp.where(k0 + k_io <= qpos_k, s, -MASK_BIG)
        _online_softmax_step_t(s, vst_ref[0, :, pl.ds(k0, tk)], m_s, l_s, acc_s)

    j_diag = q0 // tk

    def body(j, carry):
        slc_tile(j, False)
        return carry

    lax.fori_loop(0, j_diag, body, 0)
    slc_tile(j_diag, True)
    out_s[...] += gate_row(1) * (acc_s[...] / l_s[...])

    _softmax_reset(m_s, l_s, acc_s)

    def win_tile(j):
        k0 = pl.multiple_of(j * tk, tk)
        s = jnp.dot(kw_ref[0, pl.ds(k0, tk), :], qt, preferred_element_type=F32)
        kpos = k0 + k_io
        s = jnp.where((kpos <= qpos_k) & (kpos > qpos_k - WIN), s, -MASK_BIG)
        _online_softmax_step_t(s, vwt_ref[0, :, pl.ds(k0, tk)], m_s, l_s, acc_s)

    @pl.when(j_diag >= 1)
    def _():
        win_tile(j_diag - 1)

    win_tile(j_diag)
    out = out_s[...] + gate_row(2) * (acc_s[...] / l_s[...])
    for h in range(hpg):
        o_ref[0, :, h * dk:(h + 1) * dk] = jnp.transpose(out[:, h * tq:(h + 1) * tq]).astype(o_ref.dtype)


def _nsa_attention(qkv, qt, gates_t, cmp_kv, vct, vt):
    B, S, _ = qkv.shape
    G, hpg, dk = NSA_GROUPS, NSA_HPG, NSA_DK
    n_pad = cmp_kv.shape[3]
    n_cmp = (S - CMP_BLK) // CMP_STRIDE + 1
    tq, tk = min(NSA_TQ, S), min(NSA_TK, S)
    assert tk == WIN and tq <= tk
    cols = hpg * tq
    kv0 = NSA_HEADS

    def k_spec(branch):
        base = kv0 + 2 * G * branch
        return pl.BlockSpec((1, S, dk), lambda b, g, i: (b, 0, base + g))

    def vt_spec(branch):
        return pl.BlockSpec((1, dk, S), lambda b, g, i: (b, (branch - 1) * G + g, 0))

    return pl.pallas_call(
        functools.partial(_nsa_kernel, tq=tq, tk=tk, hpg=hpg, dk=dk, n_cmp=n_cmp, n_slc=S // SLC_BLK),
        grid=(B, G, S // tq),
        in_specs=[pl.BlockSpec((1, hpg * dk, tq), lambda b, g, i: (b, g, i)),
                  pl.BlockSpec((1, 1, LANES, tq), lambda b, g, i: (b, g, 0, i)),
                  pl.BlockSpec((1, 1, 1, n_pad, dk), lambda b, g, i: (b, 0, g, 0, 0)),
                  pl.BlockSpec((1, 1, dk, n_pad), lambda b, g, i: (b, g, 0, 0)),
                  k_spec(1), vt_spec(1), k_spec(2), vt_spec(2)],
        out_specs=pl.BlockSpec((1, tq, hpg * dk), lambda b, g, i: (b, i, g)),
        out_shape=jax.ShapeDtypeStruct((B, S, NSA_HEADS * NSA_DV), BF16),
        scratch_shapes=[pltpu.VMEM((1, cols), F32), pltpu.VMEM((1, cols), F32),
                        pltpu.VMEM((dk, cols), F32), pltpu.VMEM((dk, cols), F32)],
        compiler_params=_params(("parallel", "parallel", "arbitrary")),
        name="nsa_attn",
    )(qt, gates_t, cmp_kv, vct, qkv, vt, qkv, vt)


MOE_TM = 256
MOE_FFN_TM = 256


def _hidden_chunks(f):
    chunks, c0 = [], 0
    while c0 < f:
        cs = min(MXU_WIDTH_V7X, f - c0)
        chunks.append((c0, cs))
        c0 += cs
    return tuple(chunks)


def _swiglu(x, wgu_ref, wd_ref, h_s, chunks):
    for c0, cs in chunks:
        gu = jnp.dot(x, wgu_ref[0, :, 2 * c0:2 * c0 + 2 * cs], preferred_element_type=F32)
        g, u = gu[:, :cs], gu[:, cs:]
        h_s[:, c0:c0 + cs] = (g * jax.nn.sigmoid(g) * u).astype(BF16)
    return jnp.dot(h_s[...], wd_ref[0], preferred_element_type=F32)


W_CHUNK_ROWS = 128
W_STAGE_SLOTS = 4
MOE_FFN_VMEM_LIMIT_BYTES = 56 * 1024 * 1024


def _moe_ffn_kernel(blk_e_ref, n_used_ref, first_ref, nxt_ref, slot_ref, clo_ref, chi_ref,
                    x_ref, wg_hbm, wu_hbm, wd_hbm, *rest, chunks, has_prev):
    o_ref, wgu_buf, wd_buf, stg_gu, stg_d, sem_gu, sem_d, h_s = rest[1:] if has_prev else rest
    i = pl.program_id(0)
    rows = W_CHUNK_ROWS
    n_g = wg_hbm.shape[1] // rows
    n_gu = 2 * n_g
    n_d = wd_hbm.shape[1] // rows

    def gu_copy(src, e, r0, s):
        return pltpu.make_async_copy(src.at[e, pl.ds(r0, rows), :], stg_gu.at[s], sem_gu.at[s])

    def d_copy(e, r0, s):
        return pltpu.make_async_copy(wd_hbm.at[e, pl.ds(r0, rows), :], stg_d.at[s], sem_d.at[s])

    def gu_start(e, c):
        s = lax.rem(c, W_STAGE_SLOTS)

        @pl.when(c < n_g)
        def _():
            gu_copy(wg_hbm, e, pl.multiple_of(c * rows, rows), s).start()

        @pl.when(c >= n_g)
        def _():
            gu_copy(wu_hbm, e, pl.multiple_of((c - n_g) * rows, rows), s).start()

    def d_start(e, c):
        d_copy(e, pl.multiple_of(c * rows, rows), lax.rem(c, W_STAGE_SLOTS)).start()

    def start_prefetch(e):
        for c in range(W_STAGE_SLOTS):
            gu_start(e, jnp.int32(c))
            d_start(e, jnp.int32(c))

    def convert(e, dst, lo, hi):
        def gu_body(c, carry):
            s = lax.rem(c, W_STAGE_SLOTS)
            gu_copy(wg_hbm, 0, 0, s).wait()
            r0 = pl.multiple_of(lax.rem(c, n_g) * rows, rows)

            @pl.when(c < n_g)
            def _():
                for c0, cs in chunks:
                    wgu_buf[dst, pl.ds(r0, rows), 2 * c0:2 * c0 + cs] = stg_gu[s, :, c0:c0 + cs].astype(BF16)

            @pl.when(c >= n_g)
            def _():
                for c0, cs in chunks:
                    wgu_buf[dst, pl.ds(r0, rows), 2 * c0 + cs:2 * c0 + 2 * cs] = stg_gu[s, :, c0:c0 + cs].astype(BF16)

            @pl.when(c + W_STAGE_SLOTS < n_gu)
            def _():
                gu_start(e, c + W_STAGE_SLOTS)

            return carry

        lax.fori_loop(jnp.clip(lo, 0, n_gu), jnp.clip(hi, 0, n_gu), gu_body, 0)

        def d_body(c, carry):
            s = lax.rem(c, W_STAGE_SLOTS)
            d_copy(0, 0, s).wait()
            wd_buf[dst, pl.ds(pl.multiple_of(c * rows, rows), rows), :] = stg_d[s].astype(BF16)

            @pl.when(c + W_STAGE_SLOTS < n_d)
            def _():
                d_start(e, c + W_STAGE_SLOTS)

            return carry

        lax.fori_loop(jnp.clip(lo - n_gu, 0, n_d), jnp.clip(hi - n_gu, 0, n_d), d_body, 0)

    @pl.when(i == 0)
    def _():
        start_prefetch(blk_e_ref[0])
        convert(blk_e_ref[0], 0, 0, n_gu + n_d)

    used = i < n_used_ref[0]

    @pl.when(used)
    def _():
        nxt = nxt_ref[i]
        cur = slot_ref[i]

        @pl.when((first_ref[i] == 1) & (nxt >= 0))
        def _():
            start_prefetch(nxt)

        o_ref[...] = _swiglu(x_ref[...], wgu_buf.at[pl.ds(cur, 1)], wd_buf.at[pl.ds(cur, 1)], h_s,
                             chunks).astype(o_ref.dtype)

        @pl.when(nxt >= 0)
        def _():
            convert(nxt, 1 - cur, clo_ref[i], chi_ref[i])

    @pl.when(jnp.logical_not(used))
    def _():
        o_ref[...] = jnp.zeros(o_ref.shape, o_ref.dtype)


def _pack_gate_up_kernel(wg_ref, wu_ref, o_ref, *, chunks):
    for c0, cs in chunks:
        o_ref[0, :, 2 * c0:2 * c0 + cs] = wg_ref[0, :, c0:c0 + cs].astype(o_ref.dtype)
        o_ref[0, :, 2 * c0 + cs:2 * c0 + 2 * cs] = wu_ref[0, :, c0:c0 + cs].astype(o_ref.dtype)


PACK_ROWS = 512


def _pack_gate_up(w_gate, w_up):
    E, D, F = w_gate.shape
    rows = min(PACK_ROWS, D)
    spec = pl.BlockSpec((1, rows, F), lambda e, r: (e, r, 0))
    return pl.pallas_call(
        functools.partial(_pack_gate_up_kernel, chunks=_hidden_chunks(F)),
        grid=(E, D // rows),
        in_specs=[spec, spec],
        out_specs=pl.BlockSpec((1, rows, 2 * F), lambda e, r: (e, r, 0)),
        out_shape=jax.ShapeDtypeStruct((E, D, 2 * F), BF16),
        compiler_params=_params(("parallel", "parallel")),
        name="pack_gate_up",
    )(w_gate, w_up)


def _stream_schedule(blk_e, n_used, n_chunks):
    i32 = jnp.int32
    n = blk_e.shape[0]
    idx = jnp.arange(n, dtype=i32)
    used = idx < n_used
    prev_e = jnp.concatenate([jnp.full((1,), -1, i32), blk_e[:-1]])
    first_b = used & (blk_e != prev_e)
    run_start = lax.cummax(jnp.where(first_b, idx, 0), axis=0)
    nxt_incl = lax.cummin(jnp.where(first_b, idx, n), axis=0, reverse=True)
    nxt_start = jnp.concatenate([nxt_incl[1:], jnp.full((1,), n, i32)])
    has_next = used & (nxt_start < n_used)
    nxt_e = jnp.sum(jnp.where(idx[None, :] == nxt_start[:, None], blk_e[None, :], 0), axis=1)
    nxt_e = jnp.where(has_next, nxt_e, -1).astype(i32)
    n_run = jnp.maximum(jnp.minimum(nxt_start, n_used) - run_start, 1)
    step = idx - run_start
    c_lo = (n_chunks * step // n_run).astype(i32)
    c_hi = (n_chunks * (step + 1) // n_run).astype(i32)
    slot = lax.rem(jnp.cumsum(first_b.astype(i32)) - 1, 2).astype(i32)
    return first_b.astype(i32), nxt_e, slot, c_lo, c_hi


MOE_CALLS = 2


def _moe_routed(xb, eid_t, rank_t, counts, w_gate, w_up, w_down):
    T, D = xb.shape
    K = eid_t.shape[0]
    E, F, _ = w_down.shape
    tm = MOE_FFN_TM
    n = T * K
    n_blk = -(-n // tm) + E
    i32 = jnp.int32

    padded = (counts + tm - 1) // tm * tm
    pad_end = jnp.cumsum(padded)
    pad_start = pad_end - padded
    start_t = jnp.zeros_like(eid_t)
    for e in range(E):
        start_t = jnp.where(eid_t == e, pad_start[e], start_t)
    dest = (start_t + rank_t).reshape(n)
    tok = jnp.zeros((n_blk * tm,), i32).at[dest].set(jnp.tile(jnp.arange(T, dtype=i32), K),
                                                     unique_indices=True, mode="promise_in_bounds")
    n_used = (pad_end[-1] // tm).astype(i32).reshape(1)
    blk_start = jnp.arange(n_blk, dtype=i32) * tm
    blk_e = jnp.minimum(jnp.sum((pad_end[None, :] <= blk_start[:, None]).astype(i32), axis=1), E - 1)

    n_chunks = (2 * D + F) // W_CHUNK_ROWS
    n_call = n_blk // MOE_CALLS
    hbm = pl.BlockSpec(memory_space=pl.ANY)
    scratch = [pltpu.VMEM((2, D, 2 * F), BF16), pltpu.VMEM((2, F, D), BF16),
               pltpu.VMEM((W_STAGE_SLOTS, W_CHUNK_ROWS, F), F32), pltpu.VMEM((W_STAGE_SLOTS, W_CHUNK_ROWS, D), F32),
               pltpu.SemaphoreType.DMA((W_STAGE_SLOTS,)), pltpu.SemaphoreType.DMA((W_STAGE_SLOTS,)),
               pltpu.VMEM((tm, F), BF16)]
    y = None
    for c in range(MOE_CALLS):
        b0 = c * n_call
        blk_e_c = blk_e[b0:b0 + n_call]
        n_used_c = jnp.clip(n_used - b0, 0, n_call)
        sched = _stream_schedule(blk_e_c, n_used_c[0], n_chunks)
        xs = xb.at[tok[b0 * tm:(b0 + n_call) * tm]].get(mode="promise_in_bounds")
        prev = () if y is None else (y,)
        y = pl.pallas_call(
            functools.partial(_moe_ffn_kernel, chunks=_hidden_chunks(F), has_prev=y is not None),
            grid_spec=pltpu.PrefetchScalarGridSpec(
                num_scalar_prefetch=7,
                grid=(n_call,),
                in_specs=[pl.BlockSpec((tm, D), lambda i, *_: (i, 0)), hbm, hbm, hbm] + [hbm] * len(prev),
                out_specs=pl.BlockSpec((tm, D), lambda i, *_, b0=b0: (i + b0, 0)),
                scratch_shapes=scratch),
            out_shape=jax.ShapeDtypeStruct((n_blk * tm, D), BF16),
            input_output_aliases={11: 0} if prev else {},
            compiler_params=pltpu.CompilerParams(dimension_semantics=("arbitrary",),
                                                 vmem_limit_bytes=MOE_FFN_VMEM_LIMIT_BYTES),
            name="moe_ffn",
        )(blk_e_c, n_used_c, *sched, xs, w_gate, w_up, w_down, *prev)

    return y.at[dest].get(mode="promise_in_bounds").reshape(K, T, D)


def _layernorm_rows(z, g, b):
    mu = jnp.mean(z, -1, keepdims=True)
    zc = z - mu
    var = jnp.mean(jnp.square(zc), -1, keepdims=True)
    return zc * lax.rsqrt(var + LN_EPS) * g + b


def _merge_kernel(x_ref, yn_ref, ym_ref, wg_ref, wbn_ref, wbm_ref, o_ref):
    x, yn, ym = x_ref[...], yn_ref[...], ym_ref[...]
    d = o_ref.shape[1]
    for c0 in range(0, d, MXU_WIDTH_V7X):
        cs = slice(c0, c0 + MXU_WIDTH_V7X)
        gn = jax.nn.sigmoid(jnp.dot(x, wg_ref[:, cs], preferred_element_type=F32))
        gm = jax.nn.sigmoid(jnp.dot(x, wg_ref[:, d + c0:d + c0 + MXU_WIDTH_V7X], preferred_element_type=F32))
        bn = jnp.dot(yn, wbn_ref[:, cs], preferred_element_type=F32)
        bm = jnp.dot(ym, wbm_ref[:, cs], preferred_element_type=F32)
        o_ref[:, cs] = (gn * bn + gm * bm).astype(o_ref.dtype)


def _merge(xb, y_nsa, y_mla, w_g, wb_nsa, wb_mla):
    T, D = xb.shape
    tm = PROJ_TM
    row = lambda a: pl.BlockSpec((tm, a.shape[1]), lambda i: (i, 0))
    return pl.pallas_call(
        _merge_kernel,
        grid=(T // tm,),
        in_specs=[row(xb), row(y_nsa), row(y_mla), _resident(w_g.shape), _resident(wb_nsa.shape),
                  _resident(wb_mla.shape)],
        out_specs=pl.BlockSpec((tm, D), lambda i: (i, 0)),
        out_shape=jax.ShapeDtypeStruct((T, D), BF16),
        compiler_params=_params(("parallel",)),
        name="merge",
    )(xb, y_nsa, y_mla, w_g, wb_nsa, wb_mla)


def _rank_before(v):
    n = v.shape[0]
    j_io = lax.broadcasted_iota(jnp.int32, v.shape, 0)
    cnt = jnp.zeros(v.shape, jnp.int32)
    for i in range(n):
        row = v[i:i + 1, :]
        cnt = cnt + ((row > v) | ((row == v) & (j_io > i))).astype(jnp.int32)
    return cnt


def _route_tile(logits_t, rb, cnt_s):
    E, tm = logits_t.shape
    per = E // N_EXPERT_GROUPS
    scores = jax.nn.sigmoid(logits_t)
    biased = scores + rb
    grp = []
    for g in range(N_EXPERT_GROUPS):
        blk = biased[g * per:(g + 1) * per, :]
        m1 = jnp.max(blk, axis=0, keepdims=True)
        eq = blk == m1
        n_eq = jnp.sum(eq.astype(F32), axis=0, keepdims=True)
        m2 = jnp.max(jnp.where(eq, NEG_INF, blk), axis=0, keepdims=True)
        grp.append(m1 + jnp.where(n_eq >= 2.0, m1, m2))
    gsel = _rank_before(jnp.concatenate(grp, axis=0)) < TOPK_GROUPS
    masked = jnp.concatenate(
        [jnp.where(gsel[g:g + 1, :], biased[g * per:(g + 1) * per, :], NEG_INF) for g in range(N_EXPERT_GROUPS)],
        axis=0)
    sel = _rank_before(masked) < TOP_K
    w_sel = jnp.where(sel, scores, 0.0)
    w_sel = w_sel / jnp.sum(w_sel, axis=0, keepdims=True) * ROUTED_SCALE
    tri = jnp.where(lax.broadcasted_iota(jnp.int32, (E, E), 1) <= lax.broadcasted_iota(jnp.int32, (E, E), 0),
                    1.0, 0.0).astype(BF16)
    sel_b = jnp.where(sel, 1.0, 0.0).astype(BF16)
    slot = jnp.dot(tri, sel_b, preferred_element_type=F32)
    upper = jnp.where(lax.broadcasted_iota(jnp.int32, (tm, tm), 0) <= lax.broadcasted_iota(jnp.int32, (tm, tm), 1),
                      1.0, 0.0).astype(BF16)
    incl = jnp.dot(sel_b, upper, preferred_element_type=F32)
    pos = cnt_s[...] + incl - 1.0
    cnt_s[...] = cnt_s[...] + incl[:, tm - 1:tm]
    e_io = lax.broadcasted_iota(jnp.int32, (E, tm), 0)
    ids, wts, ranks = [], [], []
    for k in range(TOP_K):
        mk = sel & (slot == float(k + 1))
        ids.append(jnp.sum(jnp.where(mk, e_io, 0), axis=0, keepdims=True))
        wts.append(jnp.sum(jnp.where(mk, w_sel, 0.0), axis=0, keepdims=True))
        ranks.append(jnp.sum(jnp.where(mk, pos, 0.0), axis=0, keepdims=True).astype(jnp.int32))
    pad = 8 - TOP_K
    ids.append(jnp.zeros((pad, tm), jnp.int32))
    wts.append(jnp.zeros((pad, tm), F32))
    ranks.append(jnp.zeros((pad, tm), jnp.int32))
    return jnp.concatenate(ids, axis=0), jnp.concatenate(wts, axis=0), jnp.concatenate(ranks, axis=0)


def _out_ln1_kernel(m_ref, x_ref, wo_ref, g_ref, b_ref, wrt_ref, rb_ref,
                    x1_ref, x1b_ref, eid_ref, wts_ref, rank_ref, cnt_ref, cnt_s):
    @pl.when(pl.program_id(0) == 0)
    def _():
        cnt_s[...] = jnp.zeros(cnt_s.shape, F32)

    m = m_ref[...]
    d = x1_ref.shape[1]
    for c0 in range(0, d, MXU_WIDTH_V7X):
        cs = slice(c0, c0 + MXU_WIDTH_V7X)
        x1_ref[:, cs] = DN_ALPHA * x_ref[:, cs] + jnp.dot(m, wo_ref[:, cs], preferred_element_type=F32)
    x1 = _layernorm_rows(x1_ref[...], g_ref[...], b_ref[...])
    x1_ref[...] = x1
    x1b = x1.astype(BF16)
    x1b_ref[...] = x1b
    logits_t = _dot_nt(wrt_ref[...], x1b)
    eid_ref[...], wts_ref[...], rank_ref[...] = _route_tile(logits_t, rb_ref[...], cnt_s)
    cnt_ref[...] = cnt_s[...]


def _out_ln1(merged, x2, w_out, g, b, w_router_t, router_b):
    T, D = x2.shape
    E = w_router_t.shape[0]
    tm = MOE_TM
    row = lambda w: pl.BlockSpec((tm, w), lambda i: (i, 0))
    col = pl.BlockSpec((8, tm), lambda i: (0, i))
    return pl.pallas_call(
        _out_ln1_kernel,
        grid=(T // tm,),
        in_specs=[row(D), row(D), _resident(w_out.shape), _resident(g.shape), _resident(b.shape),
                  _resident(w_router_t.shape), _resident(router_b.shape)],
        out_specs=[row(D), row(D), col, col, col, pl.BlockSpec((E, 1), lambda i: (0, 0))],
        out_shape=[jax.ShapeDtypeStruct((T, D), F32), jax.ShapeDtypeStruct((T, D), BF16),
                   jax.ShapeDtypeStruct((8, T), jnp.int32), jax.ShapeDtypeStruct((8, T), F32),
                   jax.ShapeDtypeStruct((8, T), jnp.int32), jax.ShapeDtypeStruct((E, 1), F32)],
        scratch_shapes=[pltpu.VMEM((E, 1), F32)],
        compiler_params=_params(("arbitrary",)),
        name="out_ln1",
    )(merged, x2, w_out, g, b, w_router_t, router_b)


def _shared_ffn_kernel(x_ref, sgu_ref, sd_ref, o_ref, h_s, *, chunks):
    o_ref[...] = _swiglu(x_ref[...], sgu_ref, sd_ref, h_s, chunks)


def _shared_ffn(x1b, sh_gate_up, sh_down):
    T, D = x1b.shape
    tm = PROJ_TM
    F = sh_down.shape[1]
    return pl.pallas_call(
        functools.partial(_shared_ffn_kernel, chunks=_hidden_chunks(F)),
        grid=(T // tm,),
        in_specs=[pl.BlockSpec((tm, D), lambda i: (i, 0)), _resident(sh_gate_up.shape), _resident(sh_down.shape)],
        out_specs=pl.BlockSpec((tm, D), lambda i: (i, 0)),
        out_shape=jax.ShapeDtypeStruct((T, D), F32),
        scratch_shapes=[pltpu.VMEM((tm, F), BF16)],
        compiler_params=_params(("parallel",)),
        name="shared_ffn",
    )(x1b, sh_gate_up, sh_down)


def _ffn_ln2_kernel(x1_ref, sh_ref, yk_ref, w_ref, g_ref, b_ref, x2_ref, x2b_ref):
    acc = sh_ref[...]
    w = w_ref[...]
    for k in range(yk_ref.shape[0]):
        acc = acc + yk_ref[k].astype(F32) * w[:, k:k + 1]
    x2 = _layernorm_rows(DN_ALPHA * x1_ref[...] + acc, g_ref[...], b_ref[...])
    x2_ref[...] = x2
    x2b_ref[...] = x2.astype(BF16)


def _ffn_ln2(x1, shared, yk, wts, g, b):
    T, D = x1.shape
    K = yk.shape[0]
    tm = MOE_TM
    row = lambda w: pl.BlockSpec((tm, w), lambda i: (i, 0))
    return pl.pallas_call(
        _ffn_ln2_kernel,
        grid=(T // tm,),
        in_specs=[row(D), row(D), pl.BlockSpec((K, tm, D), lambda i: (0, i, 0)), row(wts.shape[1]),
                  _resident(g.shape), _resident(b.shape)],
        out_specs=[row(D), row(D)],
        out_shape=[jax.ShapeDtypeStruct((T, D), F32), jax.ShapeDtypeStruct((T, D), BF16)],
        compiler_params=_params(("parallel",)),
        name="ffn_ln2",
    )(x1, shared, yk, wts, g, b)


def _ple_kernel(x2_ref, x2b_ref, p_ref, wpg_ref, wpp_ref, o_ref):
    x2b = x2b_ref[...]
    pb = p_ref[...].astype(BF16)
    d = o_ref.shape[1]
    for c0 in range(0, d, MXU_WIDTH_V7X):
        cs = slice(c0, c0 + MXU_WIDTH_V7X)
        gate = jax.nn.sigmoid(jnp.dot(x2b, wpg_ref[:, cs], preferred_element_type=F32))
        pp = jnp.dot(pb, wpp_ref[:, cs], preferred_element_type=F32)
        o_ref[:, cs] = x2_ref[:, cs] + gate * pp


def _ple(x2, x2b, p2, w_pg, w_pp):
    T, D = x2.shape
    tm = PROJ_TM
    row = lambda w: pl.BlockSpec((tm, w), lambda i: (i, 0))
    return pl.pallas_call(
        _ple_kernel,
        grid=(T // tm,),
        in_specs=[row(D), row(D), row(p2.shape[1]), _resident(w_pg.shape), _resident(w_pp.shape)],
        out_specs=row(D),
        out_shape=jax.ShapeDtypeStruct((T, D), F32),
        compiler_params=_params(("parallel",)),
        name="ple",
    )(x2, x2b, p2, w_pg, w_pp)


def _attention_branches(xb, S, w_in, pe_k, pe_v, wk1, wk2, wv1, wv2, q_norm, w_uq, kv_norm, w_uk, w_uv):
    T, D = xb.shape
    B = T // S
    c0 = W_NSA_Q
    c1 = c0 + W_NSA_KV
    c2 = c1 + W_NSA_GATE
    c3 = c2 + W_MLA_CQ
    c4 = c3 + W_MLA_CKV
    c5 = c4 + W_MLA_KR
    G, dk = NSA_GROUPS, NSA_DK
    pos = jnp.arange(S)
    cos128, sin128 = _rope_tables_128(pos)
    cos64, slo64, shi64 = _rope_tables_64(pos)

    qkv = _nsa_proj(xb, w_in[:, :c1].astype(BF16), cos128, sin128, S)

    pad_kr = jnp.zeros((D, LANES - W_MLA_KR), w_in.dtype)
    pad_g = jnp.zeros((D, LANES - W_NSA_GATE), w_in.dtype)
    w_m = jnp.concatenate([w_in[:, c2:c4], w_in[:, c4:c5], pad_kr, w_in[:, c1:c2], pad_g], axis=1).astype(BF16)
    H = MLA_HEADS
    wq = w_uq.reshape(MLA_Q_RANK, H, MLA_NOPE + MLA_ROPE)
    wq = jnp.concatenate([wq, jnp.zeros((MLA_Q_RANK, H, 2 * LANES - MLA_NOPE - MLA_ROPE), wq.dtype)], axis=-1)
    wq = wq.reshape(MLA_Q_RANK, H * 2 * LANES).astype(BF16)
    wukv = jnp.concatenate([w_uk, w_uv], axis=1).astype(BF16)
    q_m, kv_m, kr_m, gates = _mla_in(xb, w_m, q_norm.reshape(1, -1), kv_norm.reshape(1, -1), wq, wukv,
                                     cos64, slo64, shi64, S)

    n_chunks = S // CMP_STRIDE
    kvc = qkv[:, NSA_HEADS * dk:(NSA_HEADS + 2 * G) * dk]
    zc = kvc.reshape(B, n_chunks, CMP_STRIDE, 2, G, dk).transpose(0, 3, 4, 1, 2, 5)
    zc = zc.reshape(B, 2, G, n_chunks, CMP_STRIDE * dk)
    pe = jnp.stack([pe_k.reshape(-1), pe_v.reshape(-1)])
    pe = jnp.broadcast_to(pe[:, None, :], (2, 8, pe.shape[-1])).astype(BF16)
    half = CMP_STRIDE * dk
    w1 = jnp.stack([jnp.concatenate([wk1[:half], wk1[half:]], axis=1),
                    jnp.concatenate([wv1[:half], wv1[half:]], axis=1)]).astype(BF16)
    w2 = jnp.stack([wk2, wv2]).astype(BF16)
    pos_c = jnp.arange(n_chunks) * CMP_STRIDE + (CMP_BLK - 1)
    cos_c, sin_c = _rope_tables_128(pos_c)
    cmp_kv = _nsa_compress(zc, pe, w1, w2, cos_c, sin_c)

    g3 = gates[:, :W_NSA_GATE].reshape(B, S, G, NSA_HPG * 3).transpose(0, 2, 3, 1)
    g3 = jnp.pad(g3, ((0, 0), (0, 0), (0, LANES - NSA_HPG * 3), (0, 0)))

    qkv3 = qkv.reshape(B, S, -1)
    qt = jnp.swapaxes(qkv3[:, :, :NSA_HEADS * dk], 1, 2)
    v0 = NSA_HEADS + 2 * G
    v_sw = jnp.concatenate([qkv3[:, :, (v0 + G) * dk:(v0 + 2 * G) * dk],
                            qkv3[:, :, (v0 + 3 * G) * dk:(v0 + 4 * G) * dk]], axis=-1)
    vt = jnp.swapaxes(v_sw, 1, 2)
    vct = jnp.swapaxes(cmp_kv[:, 1], 2, 3)
    y_nsa = _nsa_attention(qkv3, qt, g3, cmp_kv, vct, vt)
    kv_m = kv_m.reshape(B, S, -1)
    qt_m = jnp.swapaxes(q_m.reshape(B, S, -1), 1, 2)
    vt_m = jnp.swapaxes(kv_m[:, :, MLA_HEADS * MLA_NOPE:], 1, 2)
    y_mla = _mla_attention(qt_m, kv_m, kr_m.reshape(B, S, -1), vt_m)
    return y_nsa, y_mla


def kernel(x, p, w_in, nsa_pe_k, nsa_pe_v, nsa_cmp_k_w1, nsa_cmp_k_w2, nsa_cmp_v_w1, nsa_cmp_v_w2, mla_q_norm, mla_w_uq, mla_kv_norm, mla_w_uk, mla_w_uv, w_branch_nsa, w_branch_mla, w_out, ln1_g, ln1_b, router_w, router_b, exp_w_gate, exp_w_up, exp_w_down, sh_w_gate, sh_w_up, sh_w_down, ln2_g, ln2_b, ple_w_proj, ple_w_gate):
    B, S, D = x.shape
    c5 = D_IN - W_MERGE
    bs = B // BATCH_SPLITS
    T = bs * S
    for i in range(DEPTH):
        w_attn = (w_in[i], nsa_pe_k[i], nsa_pe_v[i], nsa_cmp_k_w1[i], nsa_cmp_k_w2[i], nsa_cmp_v_w1[i],
                  nsa_cmp_v_w2[i], mla_q_norm[i], mla_w_uq[i], mla_kv_norm[i], mla_w_uk[i], mla_w_uv[i])
        w_gm = w_in[i][:, c5:].astype(BF16)
        wb_nsa, wb_mla, wo = w_branch_nsa[i].astype(BF16), w_branch_mla[i].astype(BF16), w_out[i].astype(BF16)
        w_rt, b_rt = router_w[i].T.astype(BF16), router_b[i].reshape(N_EXPERTS, 1).astype(F32)
        sh_gu = _pack_gate_up(sh_w_gate[i][None], sh_w_up[i][None])
        sh_d = sh_w_down[i][None].astype(BF16)
        w_pg, w_pp = ple_w_gate[i].astype(BF16), ple_w_proj[i].astype(BF16)
        outs = []
        for h in range(BATCH_SPLITS):
            x2 = x[h * bs:(h + 1) * bs].reshape(T, D)
            p2 = p[i][h * bs:(h + 1) * bs].reshape(T, PLE_DIM)
            xb = x2.astype(BF16)
            y_nsa, y_mla = _attention_branches(xb, S, *w_attn)
            merged = _merge(xb, y_nsa.reshape(T, -1), y_mla.reshape(T, -1), w_gm, wb_nsa, wb_mla)
            x1, x1b, eid_t, wts_t, rank_t, counts = _out_ln1(
                merged, x2, wo, ln1_g[i].reshape(1, D), ln1_b[i].reshape(1, D), w_rt, b_rt)
            yk = _moe_routed(x1b, eid_t[:TOP_K], rank_t[:TOP_K], counts.reshape(N_EXPERTS).astype(jnp.int32),
                             exp_w_gate[i], exp_w_up[i], exp_w_down[i])
            x2n, x2b = _ffn_ln2(x1, _shared_ffn(x1b, sh_gu, sh_d), yk, wts_t.T, ln2_g[i].reshape(1, D),
                                ln2_b[i].reshape(1, D))
            outs.append(_ple(x2n, x2b, p2, w_pg, w_pp).reshape(bs, S, D))
        x = jnp.concatenate(outs, axis=0)
    return x
```

```python
import functools
import math

import jax
import jax.numpy as jnp
from jax import lax
from jax.experimental import pallas as pl
from jax.experimental.pallas import tpu as pltpu

D_MODEL = 2048
BATCH = 8
SEQ = 4096
DEPTH = 1

PLE_DIM = 256
ROPE_THETA = 10000.0
LN_EPS = 1e-5
RMS_EPS = 1e-6
NEG_INF = -1e30

NSA_HEADS = 8
NSA_GROUPS = 2
NSA_HPG = NSA_HEADS // NSA_GROUPS
NSA_DK = 128
NSA_DV = 128
CMP_BLK = 32
CMP_STRIDE = 16
CMP_HID = 256
SLC_BLK = 64
N_SEL = 16
WIN = 512
FORCE_SCORE = 1e4

MLA_HEADS = 8
MLA_Q_RANK = 768
MLA_KV_RANK = 512
MLA_NOPE = 128
MLA_ROPE = 64
MLA_DV = 128

N_EXPERTS = 64
TOP_K = 6
N_EXPERT_GROUPS = 8
TOPK_GROUPS = 4
D_EXPERT = 1408
ROUTED_SCALE = 2.5

W_NSA_Q = NSA_HEADS * NSA_DK
W_NSA_KV = 3 * NSA_GROUPS * (NSA_DK + NSA_DV)
W_NSA_GATE = 3 * NSA_HEADS
W_MLA_CQ = MLA_Q_RANK
W_MLA_CKV = MLA_KV_RANK
W_MLA_KR = MLA_ROPE
W_MERGE = 2 * D_MODEL
D_IN = W_NSA_Q + W_NSA_KV + W_NSA_GATE + W_MLA_CQ + W_MLA_CKV + W_MLA_KR + W_MERGE

DN_ALPHA = (2.0 * DEPTH) ** 0.25
DN_BETA = (8.0 * DEPTH) ** -0.25

V7X_VMEM_LIMIT_BYTES = 48 * 1024 * 1024
LANES = 128
MXU_WIDTH_V7X = 256
PROJ_TM = 512
BATCH_SPLITS = 1

BF16 = jnp.bfloat16
F32 = jnp.float32


def _params(sem):
    return pltpu.CompilerParams(dimension_semantics=sem, vmem_limit_bytes=V7X_VMEM_LIMIT_BYTES)


def _resident(shape):
    return pl.BlockSpec(shape, lambda *_: (0,) * len(shape), pipeline_mode=pl.Buffered(1))


def _rope_tables_128(pos):
    half = LANES // 2
    inv = jnp.power(ROPE_THETA, -jnp.arange(half, dtype=F32) * 2.0 / LANES)
    ang = pos.astype(F32)[:, None] * inv[None, :]
    cos, sin = jnp.cos(ang), jnp.sin(ang)
    return jnp.concatenate([cos, cos], -1), jnp.concatenate([-sin, sin], -1)


def _rope_tables_64(pos):
    half = MLA_ROPE // 2
    inv = jnp.power(ROPE_THETA, -jnp.arange(half, dtype=F32) * 2.0 / MLA_ROPE)
    ang = pos.astype(F32)[:, None] * inv[None, :]
    cos, sin = jnp.cos(ang), jnp.sin(ang)
    zero = jnp.zeros_like(sin)
    return (jnp.concatenate([cos, cos, cos, cos], -1),
            jnp.concatenate([-sin, zero, -sin, zero], -1),
            jnp.concatenate([zero, sin, zero, sin], -1))


def _rope128(z, cos, sin):
    return z * cos + pltpu.roll(z, LANES // 2, 1) * sin


def _rope64(z, cos, sin_lo, sin_hi):
    return z * cos + pltpu.roll(z, LANES - MLA_ROPE // 2, 1) * sin_lo + pltpu.roll(z, MLA_ROPE // 2, 1) * sin_hi


def _nsa_proj_kernel(x_ref, w_ref, cos_ref, sin_ref, o_ref, *, rope_heads, n_q_heads, scale):
    x = x_ref[...]
    cos, sin = cos_ref[...], sin_ref[...]
    n_heads = w_ref.shape[1] // LANES
    for t in range(n_heads // 2):
        c0 = t * MXU_WIDTH_V7X
        z = jnp.dot(x, w_ref[:, c0:c0 + MXU_WIDTH_V7X], preferred_element_type=F32)
        for hh in range(2):
            head = 2 * t + hh
            zh = z[:, hh * LANES:(hh + 1) * LANES]
            if head in rope_heads:
                zh = _rope128(zh, cos, sin)
            if head < n_q_heads:
                zh = zh * scale
            o_ref[:, head * LANES:(head + 1) * LANES] = zh.astype(o_ref.dtype)


def _nsa_proj(xb, w, cos, sin, S):
    T, D = xb.shape
    n = w.shape[1]
    tm = min(PROJ_TM, S)
    n_pos = S // tm
    kv0 = NSA_HEADS
    per = NSA_GROUPS * 2
    rope_heads = tuple(range(NSA_HEADS)) + tuple(kv0 + per * br + g for br in (1, 2) for g in range(NSA_GROUPS))
    return pl.pallas_call(
        functools.partial(_nsa_proj_kernel, rope_heads=rope_heads, n_q_heads=NSA_HEADS, scale=NSA_DK ** -0.5),
        grid=(T // tm,),
        in_specs=[pl.BlockSpec((tm, D), lambda i: (i, 0)),
                  _resident((D, n)),
                  pl.BlockSpec((tm, LANES), lambda i: (i % n_pos, 0)),
                  pl.BlockSpec((tm, LANES), lambda i: (i % n_pos, 0))],
        out_specs=pl.BlockSpec((tm, n), lambda i: (i, 0)),
        out_shape=jax.ShapeDtypeStruct((T, n), BF16),
        compiler_params=_params(("parallel",)),
        name="nsa_proj",
    )(xb, w, cos, sin)


def _rmsnorm_rows(z, g):
    return z * lax.rsqrt(jnp.mean(jnp.square(z), -1, keepdims=True) + RMS_EPS) * g


def _mla_in_kernel(x_ref, w_ref, gq_ref, gkv_ref, wuq_ref, wukv_ref, cos_ref, slo_ref, shi_ref,
                   q_ref, kv_ref, kr_ref, gate_ref, *, scale):
    x = x_ref[...]
    cos, slo, shi = cos_ref[...], slo_ref[...], shi_ref[...]
    c1 = MLA_Q_RANK
    c2 = c1 + MLA_KV_RANK
    cq = jnp.dot(x, w_ref[:, :c1], preferred_element_type=F32)
    cqn = _rmsnorm_rows(cq, gq_ref[...]).astype(BF16)
    ckv = jnp.dot(x, w_ref[:, c1:c2], preferred_element_type=F32)
    ckvn = _rmsnorm_rows(ckv, gkv_ref[...]).astype(BF16)
    tail = jnp.dot(x, w_ref[:, c2:c2 + 2 * LANES], preferred_element_type=F32)
    kr_ref[...] = _rope64(tail[:, :LANES], cos, slo, shi).astype(kr_ref.dtype)
    gate_ref[...] = jax.nn.sigmoid(tail[:, LANES:])
    for h in range(MLA_HEADS):
        c0 = h * MXU_WIDTH_V7X
        z = jnp.dot(cqn, wuq_ref[:, c0:c0 + MXU_WIDTH_V7X], preferred_element_type=F32)
        q_ref[:, c0:c0 + LANES] = (z[:, :LANES] * scale).astype(q_ref.dtype)
        q_ref[:, c0 + LANES:c0 + 2 * LANES] = (_rope64(z[:, LANES:], cos, slo, shi) * scale).astype(q_ref.dtype)
    n_kv = wukv_ref.shape[1]
    for c0 in range(0, n_kv, MXU_WIDTH_V7X):
        kv_ref[:, c0:c0 + MXU_WIDTH_V7X] = jnp.dot(
            ckvn, wukv_ref[:, c0:c0 + MXU_WIDTH_V7X], preferred_element_type=F32).astype(kv_ref.dtype)


def _mla_in(xb, w, gq, gkv, wuq, wukv, cos, slo, shi, S):
    T, D = xb.shape
    tm = min(PROJ_TM, S)
    n_pos = S // tm
    tbl = pl.BlockSpec((tm, LANES), lambda i: (i % n_pos, 0))
    nq, nkv = wuq.shape[1], wukv.shape[1]
    return pl.pallas_call(
        functools.partial(_mla_in_kernel, scale=(MLA_NOPE + MLA_ROPE) ** -0.5),
        grid=(T // tm,),
        in_specs=[pl.BlockSpec((tm, D), lambda i: (i, 0)),
                  _resident(w.shape), _resident(gq.shape), _resident(gkv.shape),
                  _resident(wuq.shape), _resident(wukv.shape), tbl, tbl, tbl],
        out_specs=[pl.BlockSpec((tm, nq), lambda i: (i, 0)),
                   pl.BlockSpec((tm, nkv), lambda i: (i, 0)),
                   pl.BlockSpec((tm, LANES), lambda i: (i, 0)),
                   pl.BlockSpec((tm, LANES), lambda i: (i, 0))],
        out_shape=[jax.ShapeDtypeStruct((T, nq), BF16), jax.ShapeDtypeStruct((T, nkv), BF16),
                   jax.ShapeDtypeStruct((T, LANES), BF16), jax.ShapeDtypeStruct((T, LANES), F32)],
        compiler_params=_params(("parallel",)),
        name="mla_in",
    )(xb, w, gq, gkv, wuq, wukv, cos, slo, shi)


MASK_BIG = 16384.0


def _softmax_reset(m_s, l_s, acc_s):
    m_s[...] = jnp.full(m_s.shape, NEG_INF, F32)
    l_s[...] = jnp.zeros(l_s.shape, F32)
    acc_s[...] = jnp.zeros(acc_s.shape, F32)


def _online_softmax_step(s, v, m_s, l_s, acc_s):
    m_prev = m_s[...]
    m_new = jnp.maximum(m_prev, jnp.max(s, axis=-1, keepdims=True))
    alpha = jnp.exp(m_prev - m_new)
    p = jnp.exp(s - m_new)
    l_s[...] = alpha * l_s[...] + jnp.sum(p, axis=-1, keepdims=True)
    acc_s[...] = alpha * acc_s[...] + jnp.dot(p.astype(BF16), v, preferred_element_type=F32)
    m_s[...] = m_new


def _online_softmax_step_t(st, vt, m_s, l_s, acc_s):
    m_prev = m_s[...]
    m_new = jnp.maximum(m_prev, jnp.max(st, axis=0, keepdims=True))
    alpha = jnp.exp(m_prev - m_new)
    p = jnp.exp(st - m_new)
    l_s[...] = alpha * l_s[...] + jnp.sum(p, axis=0, keepdims=True)
    acc_s[...] = alpha * acc_s[...] + jnp.dot(vt, p.astype(BF16), preferred_element_type=F32)
    m_s[...] = m_new


def _dot_nt(a, b):
    return lax.dot_general(a, b, (((1,), (1,)), ((), ())), preferred_element_type=F32)


MLA_TQ = 512
MLA_TK = 512


MLA_HPS = 4


def _mla_attn_kernel(qt_ref, kn_ref, kr_ref, vt_ref, o_ref, m_s, l_s, acc_s, *, tq, tk, hps):
    q0 = pl.program_id(2) * tq
    qts = [qt_ref[0, hh * 2 * LANES:(hh + 1) * 2 * LANES, :] for hh in range(hps)]
    _softmax_reset(m_s, l_s, acc_s)

    def tile(j, causal):
        k0 = pl.multiple_of(j * tk, tk)
        kr = kr_ref[0, pl.ds(k0, tk), :]
        for hh in range(hps):
            hs = slice(hh * LANES, (hh + 1) * LANES)
            ka = jnp.concatenate([kn_ref[0, pl.ds(k0, tk), hs], kr], axis=1)
            st = jnp.dot(ka, qts[hh], preferred_element_type=F32)
            if causal:
                kpos = k0 + lax.broadcasted_iota(jnp.int32, (tk, tq), 0)
                qpos = q0 + lax.broadcasted_iota(jnp.int32, (tk, tq), 1)
                st = jnp.where(kpos <= qpos, st, -MASK_BIG)
            _online_softmax_step_t(st, vt_ref[0, hs, pl.ds(k0, tk)], m_s.at[hh], l_s.at[hh], acc_s.at[hh])

    j_diag = q0 // tk

    def body(j, carry):
        tile(j, False)
        return carry

    lax.fori_loop(0, j_diag, body, 0)
    tile(j_diag, True)
    for hh in range(hps):
        o_ref[0, :, hh * LANES:(hh + 1) * LANES] = jnp.transpose(acc_s[hh] / l_s[hh]).astype(o_ref.dtype)


def _mla_attention(qt, kv, kr, vt):
    B, _, S = qt.shape
    H = MLA_HEADS
    hps = MLA_HPS
    tq = tk = min(MLA_TQ, S)
    return pl.pallas_call(
        functools.partial(_mla_attn_kernel, tq=tq, tk=tk, hps=hps),
        grid=(B, H // hps, S // tq),
        in_specs=[pl.BlockSpec((1, hps * 2 * LANES, tq), lambda b, h, i: (b, h, i)),
                  pl.BlockSpec((1, S, hps * LANES), lambda b, h, i: (b, 0, h)),
                  pl.BlockSpec((1, S, LANES), lambda b, h, i: (b, 0, 0)),
                  pl.BlockSpec((1, hps * LANES, S), lambda b, h, i: (b, h, 0))],
        out_specs=pl.BlockSpec((1, tq, hps * LANES), lambda b, h, i: (b, i, h)),
        out_shape=jax.ShapeDtypeStruct((B, S, H * MLA_DV), BF16),
        scratch_shapes=[pltpu.VMEM((hps, 1, tq), F32), pltpu.VMEM((hps, 1, tq), F32),
                        pltpu.VMEM((hps, MLA_DV, tq), F32)],
        compiler_params=_params(("parallel", "parallel", "arbitrary")),
        name="mla_attn",
    )(qt, kv, kr, vt)


def _gelu_tanh(x):
    return 0.5 * x * (1.0 + jnp.tanh(math.sqrt(2.0 / math.pi) * (x + 0.044715 * (x * x * x))))


def _compress_kernel(z_ref, pe_ref, w1_ref, w2_ref, cos_ref, sin_ref, o_ref, *, hid):
    which = pl.program_id(1)
    z = z_ref[0, 0, 0]
    n_chunks = z.shape[0]
    cd = z.shape[1]
    ab = jnp.dot(z, w1_ref[0], preferred_element_type=F32)
    a_part = ab[:, :hid]
    b_next = pltpu.roll(ab[:, hid:], n_chunks - 1, 0)
    pe_a = jnp.dot(pe_ref[0, :, :cd], w1_ref[0, :, :hid], preferred_element_type=F32)
    pe_b = jnp.dot(pe_ref[0, :, cd:], w1_ref[0, :, hid:], preferred_element_type=F32)
    h = _gelu_tanh(a_part + b_next + pe_a[0:1] + pe_b[0:1])
    out = jnp.dot(h.astype(BF16), w2_ref[0], preferred_element_type=F32)
    roped = _rope128(out, cos_ref[...], sin_ref[...])
    o_ref[0, 0, 0] = jnp.where(which == 0, roped, out).astype(o_ref.dtype)


def _nsa_compress(zc, pe, w1, w2, cos_c, sin_c):
    B, _, G, n_chunks, cd = zc.shape
    hid = w2.shape[1]
    d = w2.shape[2]
    return pl.pallas_call(
        functools.partial(_compress_kernel, hid=hid),
        grid=(B, 2, G),
        in_specs=[pl.BlockSpec((1, 1, 1, n_chunks, cd), lambda b, t, g: (b, t, g, 0, 0)),
                  pl.BlockSpec((1, 8, 2 * cd), lambda b, t, g: (t, 0, 0)),
                  pl.BlockSpec((1, cd, 2 * hid), lambda b, t, g: (t, 0, 0)),
                  pl.BlockSpec((1, hid, d), lambda b, t, g: (t, 0, 0)),
                  pl.BlockSpec((n_chunks, LANES), lambda b, t, g: (0, 0)),
                  pl.BlockSpec((n_chunks, LANES), lambda b, t, g: (0, 0))],
        out_specs=pl.BlockSpec((1, 1, 1, n_chunks, d), lambda b, t, g: (b, t, g, 0, 0)),
        out_shape=jax.ShapeDtypeStruct((B, 2, G, n_chunks, d), BF16),
        compiler_params=_params(("parallel", "parallel", "parallel")),
        name="nsa_compress",
    )(zc, pe, w1, w2, cos_c, sin_c)


NSA_TQ = 256
NSA_TK = 512


def _nsa_kernel(qt_ref, g_ref, kc_ref, vct_ref, ks_ref, vst_ref, kw_ref, vwt_ref, o_ref,
                m_s, l_s, acc_s, out_s, *, tq, tk, hpg, dk, n_cmp, n_slc):
    q0 = pl.program_id(2) * tq
    cols = hpg * tq
    i32 = jnp.int32
    qt = jnp.concatenate([qt_ref[0, h * dk:(h + 1) * dk, :] for h in range(hpg)], axis=1)
    gates = g_ref[0, 0]

    def gate_row(branch):
        return jnp.concatenate([gates[3 * h + branch:3 * h + branch + 1, :] for h in range(hpg)], axis=1)

    def qpos_of(n_rows):
        return q0 + lax.rem(lax.broadcasted_iota(i32, (n_rows, cols), 1), tq)

    n_pad = kc_ref.shape[3]
    sc = jnp.dot(kc_ref[0, 0, 0], qt, preferred_element_type=F32)
    n_io = lax.broadcasted_iota(i32, (n_pad, cols), 0)
    vis = (n_io * CMP_STRIDE + (CMP_BLK - 1) <= qpos_of(n_pad)) & (n_io < n_cmp)
    sc = jnp.where(vis, sc, NEG_INF)
    e = jnp.where(vis, jnp.exp(sc - jnp.max(sc, axis=0, keepdims=True)), 0.0)
    p = e / jnp.maximum(jnp.sum(e, axis=0, keepdims=True), 1e-30)
    o_cmp = jnp.dot(vct_ref[0, 0], p.astype(BF16), preferred_element_type=F32)
    out_s[...] = gate_row(0) * o_cmp

    imp = p[:, 0:tq]
    for h in range(1, hpg):
        imp = imp + p[:, h * tq:(h + 1) * tq]
    ratio = SLC_BLK // CMP_STRIDE
    span = CMP_BLK // CMP_STRIDE
    j_p = lax.broadcasted_iota(i32, (n_slc, n_pad), 0)
    c_p = lax.broadcasted_iota(i32, (n_slc, n_pad), 1)
    lo_c = ratio * j_p - (span - 1)
    pool = jnp.where((c_p >= lo_c) & (c_p < lo_c + ratio + span - 1), 1.0, 0.0).astype(BF16)
    hi = imp.astype(BF16)
    r1 = imp - hi.astype(F32)
    mid = r1.astype(BF16)
    lo = (r1 - mid.astype(F32)).astype(BF16)
    p_slc = (jnp.dot(pool, hi, preferred_element_type=F32) + jnp.dot(pool, mid, preferred_element_type=F32)
             + jnp.dot(pool, lo, preferred_element_type=F32))

    j_io = lax.broadcasted_iota(i32, (n_slc, tq), 0)
    qpos_l = q0 + lax.broadcasted_iota(i32, (n_slc, tq), 1)
    cur = qpos_l // SLC_BLK
    forced = (j_io == 0) | (j_io == cur) | (j_io == cur - 1)
    valid = j_io * SLC_BLK <= qpos_l
    score = jnp.where(forced, FORCE_SCORE, jnp.where(valid, p_slc, -FORCE_SCORE))
    bias = jnp.where(_rank_before(score) < N_SEL, 0.0, -MASK_BIG)
    if n_slc < LANES:
        bias = jnp.concatenate([bias, jnp.full((LANES - n_slc, tq), -MASK_BIG, F32)], axis=0)
    bias = bias.astype(BF16)
    qa = jnp.concatenate([qt, jnp.concatenate([bias] * hpg, axis=1)], axis=0)

    qpos_k = qpos_of(tk)
    k_io = lax.broadcasted_iota(i32, (tk, cols), 0)

    _softmax_reset(m_s, l_s, acc_s)

    def slc_tile(j, causal):
        k0 = pl.multiple_of(j * tk, tk)
        kblk = (k0 + lax.broadcasted_iota(i32, (tk, LANES), 0)) // SLC_BLK
        onehot = jnp.where(kblk == lax.broadcasted_iota(i32, (tk, LANES), 1), 1.0, 0.0).astype(BF16)
        ka = jnp.concatenate([ks_ref[0, pl.ds(k0, tk), :], onehot], axis=1)
        s = jnp.dot(ka, qa, preferred_element_type=F32)
        if causal:
            s = jnp.where(k0 + k_io <= qpos_k, s, -MASK_BIG)
        _online_softmax_step_t(s, vst_ref[0, :, pl.ds(k0, tk)], m_s, l_s, acc_s)

    j_diag = q0 // tk

    def body(j, carry):
        slc_tile(j, False)
        return carry

    lax.fori_loop(0, j_diag, body, 0)
    slc_tile(j_diag, True)
    out_s[...] += gate_row(1) * (acc_s[...] / l_s[...])

    _softmax_reset(m_s, l_s, acc_s)

    def win_tile(j):
        k0 = pl.multiple_of(j * tk, tk)
        s = jnp.dot(kw_ref[0, pl.ds(k0, tk), :], qt, preferred_element_type=F32)
        kpos = k0 + k_io
        s = jnp.where((kpos <= qpos_k) & (kpos > qpos_k - WIN), s, -MASK_BIG)
        _online_softmax_step_t(s, vwt_ref[0, :, pl.ds(k0, tk)], m_s, l_s, acc_s)

    @pl.when(j_diag >= 1)
    def _():
        win_tile(j_diag - 1)

    win_tile(j_diag)
    out = out_s[...] + gate_row(2) * (acc_s[...] / l_s[...])
    for h in range(hpg):
        o_ref[0, :, h * dk:(h + 1) * dk] = jnp.transpose(out[:, h * tq:(h + 1) * tq]).astype(o_ref.dtype)


def _nsa_attention(qkv, qt, gates_t, cmp_kv, vct, vt):
    B, S, _ = qkv.shape
    G, hpg, dk = NSA_GROUPS, NSA_HPG, NSA_DK
    n_pad = cmp_kv.shape[3]
    n_cmp = (S - CMP_BLK) // CMP_STRIDE + 1
    tq, tk = min(NSA_TQ, S), min(NSA_TK, S)
    assert tk == WIN and tq <= tk
    cols = hpg * tq
    kv0 = NSA_HEADS

    def k_spec(branch):
        base = kv0 + 2 * G * branch
        return pl.BlockSpec((1, S, dk), lambda b, g, i: (b, 0, base + g))

    def vt_spec(branch):
        return pl.BlockSpec((1, dk, S), lambda b, g, i: (b, (branch - 1) * G + g, 0))

    return pl.pallas_call(
        functools.partial(_nsa_kernel, tq=tq, tk=tk, hpg=hpg, dk=dk, n_cmp=n_cmp, n_slc=S // SLC_BLK),
        grid=(B, G, S // tq),
        in_specs=[pl.BlockSpec((1, hpg * dk, tq), lambda b, g, i: (b, g, i)),
                  pl.BlockSpec((1, 1, LANES, tq), lambda b, g, i: (b, g, 0, i)),
                  pl.BlockSpec((1, 1, 1, n_pad, dk), lambda b, g, i: (b, 0, g, 0, 0)),
                  pl.BlockSpec((1, 1, dk, n_pad), lambda b, g, i: (b, g, 0, 0)),
                  k_spec(1), vt_spec(1), k_spec(2), vt_spec(2)],
        out_specs=pl.BlockSpec((1, tq, hpg * dk), lambda b, g, i: (b, i, g)),
        out_shape=jax.ShapeDtypeStruct((B, S, NSA_HEADS * NSA_DV), BF16),
        scratch_shapes=[pltpu.VMEM((1, cols), F32), pltpu.VMEM((1, cols), F32),
                        pltpu.VMEM((dk, cols), F32), pltpu.VMEM((dk, cols), F32)],
        compiler_params=_params(("parallel", "parallel", "arbitrary")),
        name="nsa_attn",
    )(qt, gates_t, cmp_kv, vct, qkv, vt, qkv, vt)


MOE_TM = 256
MOE_FFN_TM = 256


def _hidden_chunks(f):
    chunks, c0 = [], 0
    while c0 < f:
        cs = min(MXU_WIDTH_V7X, f - c0)
        chunks.append((c0, cs))
        c0 += cs
    return tuple(chunks)


def _swiglu(x, wgu_ref, wd_ref, h_s, chunks):
    for c0, cs in chunks:
        gu = jnp.dot(x, wgu_ref[0, :, 2 * c0:2 * c0 + 2 * cs], preferred_element_type=F32)
        g, u = gu[:, :cs], gu[:, cs:]
        h_s[:, c0:c0 + cs] = (g * jax.nn.sigmoid(g) * u).astype(BF16)
    return jnp.dot(h_s[...], wd_ref[0], preferred_element_type=F32)


W_CHUNK_ROWS = 128
W_STAGE_SLOTS = 4
MOE_FFN_VMEM_LIMIT_BYTES = 56 * 1024 * 1024


def _moe_ffn_kernel(blk_e_ref, n_used_ref, first_ref, nxt_ref, slot_ref, clo_ref, chi_ref,
                    x_ref, wg_hbm, wu_hbm, wd_hbm, *rest, chunks, has_prev):
    o_ref, wgu_buf, wd_buf, stg_gu, stg_d, sem_gu, sem_d, h_s = rest[1:] if has_prev else rest
    i = pl.program_id(0)
    rows = W_CHUNK_ROWS
    n_g = wg_hbm.shape[1] // rows
    n_gu = 2 * n_g
    n_d = wd_hbm.shape[1] // rows

    def gu_copy(src, e, r0, s):
        return pltpu.make_async_copy(src.at[e, pl.ds(r0, rows), :], stg_gu.at[s], sem_gu.at[s])

    def d_copy(e, r0, s):
        return pltpu.make_async_copy(wd_hbm.at[e, pl.ds(r0, rows), :], stg_d.at[s], sem_d.at[s])

    def gu_start(e, c):
        s = lax.rem(c, W_STAGE_SLOTS)

        @pl.when(c < n_g)
        def _():
            gu_copy(wg_hbm, e, pl.multiple_of(c * rows, rows), s).start()

        @pl.when(c >= n_g)
        def _():
            gu_copy(wu_hbm, e, pl.multiple_of((c - n_g) * rows, rows), s).start()

    def d_start(e, c):
        d_copy(e, pl.multiple_of(c * rows, rows), lax.rem(c, W_STAGE_SLOTS)).start()

    def start_prefetch(e):
        for c in range(W_STAGE_SLOTS):
            gu_start(e, jnp.int32(c))
            d_start(e, jnp.int32(c))

    def convert(e, dst, lo, hi):
        def gu_body(c, carry):
            s = lax.rem(c, W_STAGE_SLOTS)
            gu_copy(wg_hbm, 0, 0, s).wait()
            r0 = pl.multiple_of(lax.rem(c, n_g) * rows, rows)

            @pl.when(c < n_g)
            def _():
                for c0, cs in chunks:
                    wgu_buf[dst, pl.ds(r0, rows), 2 * c0:2 * c0 + cs] = stg_gu[s, :, c0:c0 + cs].astype(BF16)

            @pl.when(c >= n_g)
            def _():
                for c0, cs in chunks:
                    wgu_buf[dst, pl.ds(r0, rows), 2 * c0 + cs:2 * c0 + 2 * cs] = stg_gu[s, :, c0:c0 + cs].astype(BF16)

            @pl.when(c + W_STAGE_SLOTS < n_gu)
            def _():
                gu_start(e, c + W_STAGE_SLOTS)

            return carry

        lax.fori_loop(jnp.clip(lo, 0, n_gu), jnp.clip(hi, 0, n_gu), gu_body, 0)

        def d_body(c, carry):
            s = lax.rem(c, W_STAGE_SLOTS)
            d_copy(0, 0, s).wait()
            wd_buf[dst, pl.ds(pl.multiple_of(c * rows, rows), rows), :] = stg_d[s].astype(BF16)

            @pl.when(c + W_STAGE_SLOTS < n_d)
            def _():
                d_start(e, c + W_STAGE_SLOTS)

            return carry

        lax.fori_loop(jnp.clip(lo - n_gu, 0, n_d), jnp.clip(hi - n_gu, 0, n_d), d_body, 0)

    @pl.when(i == 0)
    def _():
        start_prefetch(blk_e_ref[0])
        convert(blk_e_ref[0], 0, 0, n_gu + n_d)

    used = i < n_used_ref[0]

    @pl.when(used)
    def _():
        nxt = nxt_ref[i]
        cur = slot_ref[i]

        @pl.when((first_ref[i] == 1) & (nxt >= 0))
        def _():
            start_prefetch(nxt)

        o_ref[...] = _swiglu(x_ref[...], wgu_buf.at[pl.ds(cur, 1)], wd_buf.at[pl.ds(cur, 1)], h_s,
                             chunks).astype(o_ref.dtype)

        @pl.when(nxt >= 0)
        def _():
            convert(nxt, 1 - cur, clo_ref[i], chi_ref[i])

    @pl.when(jnp.logical_not(used))
    def _():
        o_ref[...] = jnp.zeros(o_ref.shape, o_ref.dtype)


def _pack_gate_up_kernel(wg_ref, wu_ref, o_ref, *, chunks):
    for c0, cs in chunks:
        o_ref[0, :, 2 * c0:2 * c0 + cs] = wg_ref[0, :, c0:c0 + cs].astype(o_ref.dtype)
        o_ref[0, :, 2 * c0 + cs:2 * c0 + 2 * cs] = wu_ref[0, :, c0:c0 + cs].astype(o_ref.dtype)


PACK_ROWS = 512


def _pack_gate_up(w_gate, w_up):
    E, D, F = w_gate.shape
    rows = min(PACK_ROWS, D)
    spec = pl.BlockSpec((1, rows, F), lambda e, r: (e, r, 0))
    return pl.pallas_call(
        functools.partial(_pack_gate_up_kernel, chunks=_hidden_chunks(F)),
        grid=(E, D // rows),
        in_specs=[spec, spec],
        out_specs=pl.BlockSpec((1, rows, 2 * F), lambda e, r: (e, r, 0)),
        out_shape=jax.ShapeDtypeStruct((E, D, 2 * F), BF16),
        compiler_params=_params(("parallel", "parallel")),
        name="pack_gate_up",
    )(w_gate, w_up)


def _stream_schedule(blk_e, n_used, n_chunks):
    i32 = jnp.int32
    n = blk_e.shape[0]
    idx = jnp.arange(n, dtype=i32)
    used = idx < n_used
    prev_e = jnp.concatenate([jnp.full((1,), -1, i32), blk_e[:-1]])
    first_b = used & (blk_e != prev_e)
    run_start = lax.cummax(jnp.where(first_b, idx, 0), axis=0)
    nxt_incl = lax.cummin(jnp.where(first_b, idx, n), axis=0, reverse=True)
    nxt_start = jnp.concatenate([nxt_incl[1:], jnp.full((1,), n, i32)])
    has_next = used & (nxt_start < n_used)
    nxt_e = jnp.sum(jnp.where(idx[None, :] == nxt_start[:, None], blk_e[None, :], 0), axis=1)
    nxt_e = jnp.where(has_next, nxt_e, -1).astype(i32)
    n_run = jnp.maximum(jnp.minimum(nxt_start, n_used) - run_start, 1)
    step = idx - run_start
    c_lo = (n_chunks * step // n_run).astype(i32)
    c_hi = (n_chunks * (step + 1) // n_run).astype(i32)
    slot = lax.rem(jnp.cumsum(first_b.astype(i32)) - 1, 2).astype(i32)
    return first_b.astype(i32), nxt_e, slot, c_lo, c_hi


MOE_CALLS = 4


def _moe_routed(xb, eid_t, rank_t, counts, w_gate, w_up, w_down):
    T, D = xb.shape
    K = eid_t.shape[0]
    E, F, _ = w_down.shape
    tm = MOE_FFN_TM
    n = T * K
    n_blk = -(-n // tm) + E
    i32 = jnp.int32

    padded = (counts + tm - 1) // tm * tm
    pad_end = jnp.cumsum(padded)
    pad_start = pad_end - padded
    start_t = jnp.zeros_like(eid_t)
    for e in range(E):
        start_t = jnp.where(eid_t == e, pad_start[e], start_t)
    dest = (start_t + rank_t).reshape(n)
    tok = jnp.zeros((n_blk * tm,), i32).at[dest].set(jnp.tile(jnp.arange(T, dtype=i32), K),
                                                     unique_indices=True, mode="promise_in_bounds")
    n_used = (pad_end[-1] // tm).astype(i32).reshape(1)
    blk_start = jnp.arange(n_blk, dtype=i32) * tm
    blk_e = jnp.minimum(jnp.sum((pad_end[None, :] <= blk_start[:, None]).astype(i32), axis=1), E - 1)

    n_chunks = (2 * D + F) // W_CHUNK_ROWS
    n_call = n_blk // MOE_CALLS
    hbm = pl.BlockSpec(memory_space=pl.ANY)
    scratch = [pltpu.VMEM((2, D, 2 * F), BF16), pltpu.VMEM((2, F, D), BF16),
               pltpu.VMEM((W_STAGE_SLOTS, W_CHUNK_ROWS, F), F32), pltpu.VMEM((W_STAGE_SLOTS, W_CHUNK_ROWS, D), F32),
               pltpu.SemaphoreType.DMA((W_STAGE_SLOTS,)), pltpu.SemaphoreType.DMA((W_STAGE_SLOTS,)),
               pltpu.VMEM((tm, F), BF16)]
    y = None
    for c in range(MOE_CALLS):
        b0 = c * n_call
        blk_e_c = blk_e[b0:b0 + n_call]
        n_used_c = jnp.clip(n_used - b0, 0, n_call)
        sched = _stream_schedule(blk_e_c, n_used_c[0], n_chunks)
        xs = xb.at[tok[b0 * tm:(b0 + n_call) * tm]].get(mode="promise_in_bounds")
        prev = () if y is None else (y,)
        y = pl.pallas_call(
            functools.partial(_moe_ffn_kernel, chunks=_hidden_chunks(F), has_prev=y is not None),
            grid_spec=pltpu.PrefetchScalarGridSpec(
                num_scalar_prefetch=7,
                grid=(n_call,),
                in_specs=[pl.BlockSpec((tm, D), lambda i, *_: (i, 0)), hbm, hbm, hbm] + [hbm] * len(prev),
                out_specs=pl.BlockSpec((tm, D), lambda i, *_, b0=b0: (i + b0, 0)),
                scratch_shapes=scratch),
            out_shape=jax.ShapeDtypeStruct((n_blk * tm, D), BF16),
            input_output_aliases={11: 0} if prev else {},
            compiler_params=pltpu.CompilerParams(dimension_semantics=("arbitrary",),
                                                 vmem_limit_bytes=MOE_FFN_VMEM_LIMIT_BYTES),
            name="moe_ffn",
        )(blk_e_c, n_used_c, *sched, xs, w_gate, w_up, w_down, *prev)

    return y.at[dest].get(mode="promise_in_bounds").reshape(K, T, D)


def _layernorm_rows(z, g, b):
    mu = jnp.mean(z, -1, keepdims=True)
    zc = z - mu
    var = jnp.mean(jnp.square(zc), -1, keepdims=True)
    return zc * lax.rsqrt(var + LN_EPS) * g + b


def _merge_kernel(x_ref, yn_ref, ym_ref, wg_ref, wbn_ref, wbm_ref, o_ref):
    x, yn, ym = x_ref[...], yn_ref[...], ym_ref[...]
    d = o_ref.shape[1]
    for c0 in range(0, d, MXU_WIDTH_V7X):
        cs = slice(c0, c0 + MXU_WIDTH_V7X)
        gn = jax.nn.sigmoid(jnp.dot(x, wg_ref[:, cs], preferred_element_type=F32))
        gm = jax.nn.sigmoid(jnp.dot(x, wg_ref[:, d + c0:d + c0 + MXU_WIDTH_V7X], preferred_element_type=F32))
        bn = jnp.dot(yn, wbn_ref[:, cs], preferred_element_type=F32)
        bm = jnp.dot(ym, wbm_ref[:, cs], preferred_element_type=F32)
        o_ref[:, cs] = (gn * bn + gm * bm).astype(o_ref.dtype)


def _merge(xb, y_nsa, y_mla, w_g, wb_nsa, wb_mla):
    T, D = xb.shape
    tm = PROJ_TM
    row = lambda a: pl.BlockSpec((tm, a.shape[1]), lambda i: (i, 0))
    return pl.pallas_call(
        _merge_kernel,
        grid=(T // tm,),
        in_specs=[row(xb), row(y_nsa), row(y_mla), _resident(w_g.shape), _resident(wb_nsa.shape),
                  _resident(wb_mla.shape)],
        out_specs=pl.BlockSpec((tm, D), lambda i: (i, 0)),
        out_shape=jax.ShapeDtypeStruct((T, D), BF16),
        compiler_params=_params(("parallel",)),
        name="merge",
    )(xb, y_nsa, y_mla, w_g, wb_nsa, wb_mla)


def _rank_before(v):
    n = v.shape[0]
    j_io = lax.broadcasted_iota(jnp.int32, v.shape, 0)
    cnt = jnp.zeros(v.shape, jnp.int32)
    for i in range(n):
        row = v[i:i + 1, :]
        cnt = cnt + ((row > v) | ((row == v) & (j_io > i))).astype(jnp.int32)
    return cnt


def _route_tile(logits_t, rb, cnt_s):
    E, tm = logits_t.shape
    per = E // N_EXPERT_GROUPS
    scores = jax.nn.sigmoid(logits_t)
    biased = scores + rb
    grp = []
    for g in range(N_EXPERT_GROUPS):
        blk = biased[g * per:(g + 1) * per, :]
        m1 = jnp.max(blk, axis=0, keepdims=True)
        eq = blk == m1
        n_eq = jnp.sum(eq.astype(F32), axis=0, keepdims=True)
        m2 = jnp.max(jnp.where(eq, NEG_INF, blk), axis=0, keepdims=True)
        grp.append(m1 + jnp.where(n_eq >= 2.0, m1, m2))
    gsel = _rank_before(jnp.concatenate(grp, axis=0)) < TOPK_GROUPS
    masked = jnp.concatenate(
        [jnp.where(gsel[g:g + 1, :], biased[g * per:(g + 1) * per, :], NEG_INF) for g in range(N_EXPERT_GROUPS)],
        axis=0)
    sel = _rank_before(masked) < TOP_K
    w_sel = jnp.where(sel, scores, 0.0)
    w_sel = w_sel / jnp.sum(w_sel, axis=0, keepdims=True) * ROUTED_SCALE
    tri = jnp.where(lax.broadcasted_iota(jnp.int32, (E, E), 1) <= lax.broadcasted_iota(jnp.int32, (E, E), 0),
                    1.0, 0.0).astype(BF16)
    sel_b = jnp.where(sel, 1.0, 0.0).astype(BF16)
    slot = jnp.dot(tri, sel_b, preferred_element_type=F32)
    upper = jnp.where(lax.broadcasted_iota(jnp.int32, (tm, tm), 0) <= lax.broadcasted_iota(jnp.int32, (tm, tm), 1),
                      1.0, 0.0).astype(BF16)
    incl = jnp.dot(sel_b, upper, preferred_element_type=F32)
    pos = cnt_s[...] + incl - 1.0
    cnt_s[...] = cnt_s[...] + incl[:, tm - 1:tm]
    e_io = lax.broadcasted_iota(jnp.int32, (E, tm), 0)
    ids, wts, ranks = [], [], []
    for k in range(TOP_K):
        mk = sel & (slot == float(k + 1))
        ids.append(jnp.sum(jnp.where(mk, e_io, 0), axis=0, keepdims=True))
        wts.append(jnp.sum(jnp.where(mk, w_sel, 0.0), axis=0, keepdims=True))
        ranks.append(jnp.sum(jnp.where(mk, pos, 0.0), axis=0, keepdims=True).astype(jnp.int32))
    pad = 8 - TOP_K
    ids.append(jnp.zeros((pad, tm), jnp.int32))
    wts.append(jnp.zeros((pad, tm), F32))
    ranks.append(jnp.zeros((pad, tm), jnp.int32))
    return jnp.concatenate(ids, axis=0), jnp.concatenate(wts, axis=0), jnp.concatenate(ranks, axis=0)


def _out_ln1_kernel(m_ref, x_ref, wo_ref, g_ref, b_ref, wrt_ref, rb_ref,
                    x1_ref, x1b_ref, eid_ref, wts_ref, rank_ref, cnt_ref, cnt_s):
    @pl.when(pl.program_id(0) == 0)
    def _():
        cnt_s[...] = jnp.zeros(cnt_s.shape, F32)

    m = m_ref[...]
    d = x1_ref.shape[1]
    for c0 in range(0, d, MXU_WIDTH_V7X):
        cs = slice(c0, c0 + MXU_WIDTH_V7X)
        x1_ref[:, cs] = DN_ALPHA * x_ref[:, cs] + jnp.dot(m, wo_ref[:, cs], preferred_element_type=F32)
    x1 = _layernorm_rows(x1_ref[...], g_ref[...], b_ref[...])
    x1_ref[...] = x1
    x1b = x1.astype(BF16)
    x1b_ref[...] = x1b
    logits_t = _dot_nt(wrt_ref[...], x1b)
    eid_ref[...], wts_ref[...], rank_ref[...] = _route_tile(logits_t, rb_ref[...], cnt_s)
    cnt_ref[...] = cnt_s[...]


def _out_ln1(merged, x2, w_out, g, b, w_router_t, router_b):
    T, D = x2.shape
    E = w_router_t.shape[0]
    tm = MOE_TM
    row = lambda w: pl.BlockSpec((tm, w), lambda i: (i, 0))
    col = pl.BlockSpec((8, tm), lambda i: (0, i))
    return pl.pallas_call(
        _out_ln1_kernel,
        grid=(T // tm,),
        in_specs=[row(D), row(D), _resident(w_out.shape), _resident(g.shape), _resident(b.shape),
                  _resident(w_router_t.shape), _resident(router_b.shape)],
        out_specs=[row(D), row(D), col, col, col, pl.BlockSpec((E, 1), lambda i: (0, 0))],
        out_shape=[jax.ShapeDtypeStruct((T, D), F32), jax.ShapeDtypeStruct((T, D), BF16),
                   jax.ShapeDtypeStruct((8, T), jnp.int32), jax.ShapeDtypeStruct((8, T), F32),
                   jax.ShapeDtypeStruct((8, T), jnp.int32), jax.ShapeDtypeStruct((E, 1), F32)],
        scratch_shapes=[pltpu.VMEM((E, 1), F32)],
        compiler_params=_params(("arbitrary",)),
        name="out_ln1",
    )(merged, x2, w_out, g, b, w_router_t, router_b)


def _shared_ffn_kernel(x_ref, sgu_ref, sd_ref, o_ref, h_s, *, chunks):
    o_ref[...] = _swiglu(x_ref[...], sgu_ref, sd_ref, h_s, chunks)


def _shared_ffn(x1b, sh_gate_up, sh_down):
    T, D = x1b.shape
    tm = PROJ_TM
    F = sh_down.shape[1]
    return pl.pallas_call(
        functools.partial(_shared_ffn_kernel, chunks=_hidden_chunks(F)),
        grid=(T // tm,),
        in_specs=[pl.BlockSpec((tm, D), lambda i: (i, 0)), _resident(sh_gate_up.shape), _resident(sh_down.shape)],
        out_specs=pl.BlockSpec((tm, D), lambda i: (i, 0)),
        out_shape=jax.ShapeDtypeStruct((T, D), F32),
        scratch_shapes=[pltpu.VMEM((tm, F), BF16)],
        compiler_params=_params(("parallel",)),
        name="shared_ffn",
    )(x1b, sh_gate_up, sh_down)


def _ffn_ln2_kernel(x1_ref, sh_ref, yk_ref, w_ref, g_ref, b_ref, x2_ref, x2b_ref):
    acc = sh_ref[...]
    w = w_ref[...]
    for k in range(yk_ref.shape[0]):
        acc = acc + yk_ref[k].astype(F32) * w[:, k:k + 1]
    x2 = _layernorm_rows(DN_ALPHA * x1_ref[...] + acc, g_ref[...], b_ref[...])
    x2_ref[...] = x2
    x2b_ref[...] = x2.astype(BF16)


def _ffn_ln2(x1, shared, yk, wts, g, b):
    T, D = x1.shape
    K = yk.shape[0]
    tm = MOE_TM
    row = lambda w: pl.BlockSpec((tm, w), lambda i: (i, 0))
    return pl.pallas_call(
        _ffn_ln2_kernel,
        grid=(T // tm,),
        in_specs=[row(D), row(D), pl.BlockSpec((K, tm, D), lambda i: (0, i, 0)), row(wts.shape[1]),
                  _resident(g.shape), _resident(b.shape)],
        out_specs=[row(D), row(D)],
        out_shape=[jax.ShapeDtypeStruct((T, D), F32), jax.ShapeDtypeStruct((T, D), BF16)],
        compiler_params=_params(("parallel",)),
        name="ffn_ln2",
    )(x1, shared, yk, wts, g, b)


def _ple_kernel(x2_ref, x2b_ref, p_ref, wpg_ref, wpp_ref, o_ref):
    x2b = x2b_ref[...]
    pb = p_ref[...].astype(BF16)
    d = o_ref.shape[1]
    for c0 in range(0, d, MXU_WIDTH_V7X):
        cs = slice(c0, c0 + MXU_WIDTH_V7X)
        gate = jax.nn.sigmoid(jnp.dot(x2b, wpg_ref[:, cs], preferred_element_type=F32))
        pp = jnp.dot(pb, wpp_ref[:, cs], preferred_element_type=F32)
        o_ref[:, cs] = x2_ref[:, cs] + gate * pp


def _ple(x2, x2b, p2, w_pg, w_pp):
    T, D = x2.shape
    tm = PROJ_TM
    row = lambda w: pl.BlockSpec((tm, w), lambda i: (i, 0))
    return pl.pallas_call(
        _ple_kernel,
        grid=(T // tm,),
        in_specs=[row(D), row(D), row(p2.shape[1]), _resident(w_pg.shape), _resident(w_pp.shape)],
        out_specs=row(D),
        out_shape=jax.ShapeDtypeStruct((T, D), F32),
        compiler_params=_params(("parallel",)),
        name="ple",
    )(x2, x2b, p2, w_pg, w_pp)


def _attention_branches(xb, S, w_in, pe_k, pe_v, wk1, wk2, wv1, wv2, q_norm, w_uq, kv_norm, w_uk, w_uv):
    T, D = xb.shape
    B = T // S
    c0 = W_NSA_Q
    c1 = c0 + W_NSA_KV
    c2 = c1 + W_NSA_GATE
    c3 = c2 + W_MLA_CQ
    c4 = c3 + W_MLA_CKV
    c5 = c4 + W_MLA_KR
    G, dk = NSA_GROUPS, NSA_DK
    pos = jnp.arange(S)
    cos128, sin128 = _rope_tables_128(pos)
    cos64, slo64, shi64 = _rope_tables_64(pos)

    qkv = _nsa_proj(xb, w_in[:, :c1].astype(BF16), cos128, sin128, S)

    pad_kr = jnp.zeros((D, LANES - W_MLA_KR), w_in.dtype)
    pad_g = jnp.zeros((D, LANES - W_NSA_GATE), w_in.dtype)
    w_m = jnp.concatenate([w_in[:, c2:c4], w_in[:, c4:c5], pad_kr, w_in[:, c1:c2], pad_g], axis=1).astype(BF16)
    H = MLA_HEADS
    wq = w_uq.reshape(MLA_Q_RANK, H, MLA_NOPE + MLA_ROPE)
    wq = jnp.concatenate([wq, jnp.zeros((MLA_Q_RANK, H, 2 * LANES - MLA_NOPE - MLA_ROPE), wq.dtype)], axis=-1)
    wq = wq.reshape(MLA_Q_RANK, H * 2 * LANES).astype(BF16)
    wukv = jnp.concatenate([w_uk, w_uv], axis=1).astype(BF16)
    q_m, kv_m, kr_m, gates = _mla_in(xb, w_m, q_norm.reshape(1, -1), kv_norm.reshape(1, -1), wq, wukv,
                                     cos64, slo64, shi64, S)

    n_chunks = S // CMP_STRIDE
    kvc = qkv[:, NSA_HEADS * dk:(NSA_HEADS + 2 * G) * dk]
    zc = kvc.reshape(B, n_chunks, CMP_STRIDE, 2, G, dk).transpose(0, 3, 4, 1, 2, 5)
    zc = zc.reshape(B, 2, G, n_chunks, CMP_STRIDE * dk)
    pe = jnp.stack([pe_k.reshape(-1), pe_v.reshape(-1)])
    pe = jnp.broadcast_to(pe[:, None, :], (2, 8, pe.shape[-1])).astype(BF16)
    half = CMP_STRIDE * dk
    w1 = jnp.stack([jnp.concatenate([wk1[:half], wk1[half:]], axis=1),
                    jnp.concatenate([wv1[:half], wv1[half:]], axis=1)]).astype(BF16)
    w2 = jnp.stack([wk2, wv2]).astype(BF16)
    pos_c = jnp.arange(n_chunks) * CMP_STRIDE + (CMP_BLK - 1)
    cos_c, sin_c = _rope_tables_128(pos_c)
    cmp_kv = _nsa_compress(zc, pe, w1, w2, cos_c, sin_c)

    g3 = gates[:, :W_NSA_GATE].reshape(B, S, G, NSA_HPG * 3).transpose(0, 2, 3, 1)
    g3 = jnp.pad(g3, ((0, 0), (0, 0), (0, LANES - NSA_HPG * 3), (0, 0)))

    qkv3 = qkv.reshape(B, S, -1)
    qt = jnp.swapaxes(qkv3[:, :, :NSA_HEADS * dk], 1, 2)
    v0 = NSA_HEADS + 2 * G
    v_sw = jnp.concatenate([qkv3[:, :, (v0 + G) * dk:(v0 + 2 * G) * dk],
                            qkv3[:, :, (v0 + 3 * G) * dk:(v0 + 4 * G) * dk]], axis=-1)
    vt = jnp.swapaxes(v_sw, 1, 2)
    vct = jnp.swapaxes(cmp_kv[:, 1], 2, 3)
    y_nsa = _nsa_attention(qkv3, qt, g3, cmp_kv, vct, vt)
    kv_m = kv_m.reshape(B, S, -1)
    qt_m = jnp.swapaxes(q_m.reshape(B, S, -1), 1, 2)
    vt_m = jnp.swapaxes(kv_m[:, :, MLA_HEADS * MLA_NOPE:], 1, 2)
    y_mla = _mla_attention(qt_m, kv_m, kr_m.reshape(B, S, -1), vt_m)
    return y_nsa, y_mla


def kernel(x, p, w_in, nsa_pe_k, nsa_pe_v, nsa_cmp_k_w1, nsa_cmp_k_w2, nsa_cmp_v_w1, nsa_cmp_v_w2, mla_q_norm, mla_w_uq, mla_kv_norm, mla_w_uk, mla_w_uv, w_branch_nsa, w_branch_mla, w_out, ln1_g, ln1_b, router_w, router_b, exp_w_gate, exp_w_up, exp_w_down, sh_w_gate, sh_w_up, sh_w_down, ln2_g, ln2_b, ple_w_proj, ple_w_gate):
    B, S, D = x.shape
    c5 = D_IN - W_MERGE
    bs = B // BATCH_SPLITS
    T = bs * S
    for i in range(DEPTH):
        w_attn = (w_in[i], nsa_pe_k[i], nsa_pe_v[i], nsa_cmp_k_w1[i], nsa_cmp_k_w2[i], nsa_cmp_v_w1[i],
                  nsa_cmp_v_w2[i], mla_q_norm[i], mla_w_uq[i], mla_kv_norm[i], mla_w_uk[i], mla_w_uv[i])
        w_gm = w_in[i][:, c5:].astype(BF16)
        wb_nsa, wb_mla, wo = w_branch_nsa[i].astype(BF16), w_branch_mla[i].astype(BF16), w_out[i].astype(BF16)
        w_rt, b_rt = router_w[i].T.astype(BF16), router_b[i].reshape(N_EXPERTS, 1).astype(F32)
        sh_gu = _pack_gate_up(sh_w_gate[i][None], sh_w_up[i][None])
        sh_d = sh_w_down[i][None].astype(BF16)
        w_pg, w_pp = ple_w_gate[i].astype(BF16), ple_w_proj[i].astype(BF16)
        outs = []
        for h in range(BATCH_SPLITS):
            x2 = x[h * bs:(h + 1) * bs].reshape(T, D)
            p2 = p[i][h * bs:(h + 1) * bs].reshape(T, PLE_DIM)
            xb = x2.astype(BF16)
            y_nsa, y_mla = _attention_branches(xb, S, *w_attn)
            merged = _merge(xb, y_nsa.reshape(T, -1), y_mla.reshape(T, -1), w_gm, wb_nsa, wb_mla)
            x1, x1b, eid_t, wts_t, rank_t, counts = _out_ln1(
                merged, x2, wo, ln1_g[i].reshape(1, D), ln1_b[i].reshape(1, D), w_rt, b_rt)
            yk = _moe_routed(x1b, eid_t[:TOP_K], rank_t[:TOP_K], counts.reshape(N_EXPERTS).astype(jnp.int32),
                             exp_w_gate[i], exp_w_up[i], exp_w_down[i])
            x2n, x2b = _ffn_ln2(x1, _shared_ffn(x1b, sh_gu, sh_d), yk, wts_t.T, ln2_g[i].reshape(1, D),
                                ln2_b[i].reshape(1, D))
            outs.append(_ple(x2n, x2b, p2, w_pg, w_pp).reshape(bs, S, D))
        x = jnp.concatenate(outs, axis=0)
    return x
```

```python
import functools
import math

import jax
import jax.numpy as jnp
from jax import lax
from jax.experimental import pallas as pl
from jax.experimental.pallas import tpu as pltpu

D_MODEL = 2048
BATCH = 8
SEQ = 4096
DEPTH = 1

PLE_DIM = 256
ROPE_THETA = 10000.0
LN_EPS = 1e-5
RMS_EPS = 1e-6
NEG_INF = -1e30

NSA_HEADS = 8
NSA_GROUPS = 2
NSA_HPG = NSA_HEADS // NSA_GROUPS
NSA_DK = 128
NSA_DV = 128
CMP_BLK = 32
CMP_STRIDE = 16
CMP_HID = 256
SLC_BLK = 64
N_SEL = 16
WIN = 512
FORCE_SCORE = 1e4

MLA_HEADS = 8
MLA_Q_RANK = 768
MLA_KV_RANK = 512
MLA_NOPE = 128
MLA_ROPE = 64
MLA_DV = 128

N_EXPERTS = 64
TOP_K = 6
N_EXPERT_GROUPS = 8
TOPK_GROUPS = 4
D_EXPERT = 1408
ROUTED_SCALE = 2.5

W_NSA_Q = NSA_HEADS * NSA_DK
W_NSA_KV = 3 * NSA_GROUPS * (NSA_DK + NSA_DV)
W_NSA_GATE = 3 * NSA_HEADS
W_MLA_CQ = MLA_Q_RANK
W_MLA_CKV = MLA_KV_RANK
W_MLA_KR = MLA_ROPE
W_MERGE = 2 * D_MODEL
D_IN = W_NSA_Q + W_NSA_KV + W_NSA_GATE + W_MLA_CQ + W_MLA_CKV + W_MLA_KR + W_MERGE

DN_ALPHA = (2.0 * DEPTH) ** 0.25
DN_BETA = (8.0 * DEPTH) ** -0.25

V7X_VMEM_LIMIT_BYTES = 48 * 1024 * 1024
LANES = 128
MXU_WIDTH_V7X = 256
PROJ_TM = 512
BATCH_SPLITS = 1

BF16 = jnp.bfloat16
F32 = jnp.float32


def _params(sem):
    return pltpu.CompilerParams(dimension_semantics=sem, vmem_limit_bytes=V7X_VMEM_LIMIT_BYTES)


def _resident(shape):
    return pl.BlockSpec(shape, lambda *_: (0,) * len(shape), pipeline_mode=pl.Buffered(1))


def _rope_tables_128(pos):
    half = LANES // 2
    inv = jnp.power(ROPE_THETA, -jnp.arange(half, dtype=F32) * 2.0 / LANES)
    ang = pos.astype(F32)[:, None] * inv[None, :]
    cos, sin = jnp.cos(ang), jnp.sin(ang)
    return jnp.concatenate([cos, cos], -1), jnp.concatenate([-sin, sin], -1)


def _rope_tables_64(pos):
    half = MLA_ROPE // 2
    inv = jnp.power(ROPE_THETA, -jnp.arange(half, dtype=F32) * 2.0 / MLA_ROPE)
    ang = pos.astype(F32)[:, None] * inv[None, :]
    cos, sin = jnp.cos(ang), jnp.sin(ang)
    zero = jnp.zeros_like(sin)
    return (jnp.concatenate([cos, cos, cos, cos], -1),
            jnp.concatenate([-sin, zero, -sin, zero], -1),
            jnp.concatenate([zero, sin, zero, sin], -1))


def _rope128(z, cos, sin):
    return z * cos + pltpu.roll(z, LANES // 2, 1) * sin


def _rope64(z, cos, sin_lo, sin_hi):
    return z * cos + pltpu.roll(z, LANES - MLA_ROPE // 2, 1) * sin_lo + pltpu.roll(z, MLA_ROPE // 2, 1) * sin_hi


def _nsa_proj_kernel(x_ref, w_ref, cos_ref, sin_ref, o_ref, t_ref, *, rope_heads, n_q_heads, scale, t_heads):
    x = x_ref[...]
    cos, sin = cos_ref[...], sin_ref[...]
    n_heads = w_ref.shape[1] // LANES
    for t in range(n_heads // 2):
        c0 = t * MXU_WIDTH_V7X
        z = jnp.dot(x, w_ref[:, c0:c0 + MXU_WIDTH_V7X], preferred_element_type=F32)
        for hh in range(2):
            head = 2 * t + hh
            zh = z[:, hh * LANES:(hh + 1) * LANES]
            if head in rope_heads:
                zh = _rope128(zh, cos, sin)
            if head < n_q_heads:
                zh = zh * scale
            o_ref[:, head * LANES:(head + 1) * LANES] = zh.astype(o_ref.dtype)
            if head in t_heads:
                j = t_heads.index(head)
                t_ref[0, j * LANES:(j + 1) * LANES, :] = jnp.transpose(zh).astype(t_ref.dtype)


def _nsa_proj(xb, w, cos, sin, S):
    T, D = xb.shape
    n = w.shape[1]
    tm = min(PROJ_TM, S)
    n_pos = S // tm
    kv0 = NSA_HEADS
    per = NSA_GROUPS * 2
    rope_heads = tuple(range(NSA_HEADS)) + tuple(kv0 + per * br + g for br in (1, 2) for g in range(NSA_GROUPS))
    t_heads = tuple(range(NSA_HEADS)) + tuple(kv0 + per * br + NSA_GROUPS + g for br in (1, 2)
                                              for g in range(NSA_GROUPS))
    n_t = len(t_heads) * LANES
    return pl.pallas_call(
        functools.partial(_nsa_proj_kernel, rope_heads=rope_heads, n_q_heads=NSA_HEADS, scale=NSA_DK ** -0.5,
                          t_heads=t_heads),
        grid=(T // tm,),
        in_specs=[pl.BlockSpec((tm, D), lambda i: (i, 0)),
                  _resident((D, n)),
                  pl.BlockSpec((tm, LANES), lambda i: (i % n_pos, 0)),
                  pl.BlockSpec((tm, LANES), lambda i: (i % n_pos, 0))],
        out_specs=[pl.BlockSpec((tm, n), lambda i: (i, 0)),
                   pl.BlockSpec((1, n_t, tm), lambda i: (i // n_pos, 0, i % n_pos))],
        out_shape=[jax.ShapeDtypeStruct((T, n), BF16), jax.ShapeDtypeStruct((T // S, n_t, S), BF16)],
        compiler_params=_params(("parallel",)),
        name="nsa_proj",
    )(xb, w, cos, sin)


def _rmsnorm_rows(z, g):
    return z * lax.rsqrt(jnp.mean(jnp.square(z), -1, keepdims=True) + RMS_EPS) * g


def _mla_in_kernel(x_ref, w_ref, gq_ref, gkv_ref, wuq_ref, wukv_ref, cos_ref, slo_ref, shi_ref,
                   q_ref, kv_ref, kr_ref, gate_ref, *, scale):
    x = x_ref[...]
    cos, slo, shi = cos_ref[...], slo_ref[...], shi_ref[...]
    c1 = MLA_Q_RANK
    c2 = c1 + MLA_KV_RANK
    cq = jnp.dot(x, w_ref[:, :c1], preferred_element_type=F32)
    cqn = _rmsnorm_rows(cq, gq_ref[...]).astype(BF16)
    ckv = jnp.dot(x, w_ref[:, c1:c2], preferred_element_type=F32)
    ckvn = _rmsnorm_rows(ckv, gkv_ref[...]).astype(BF16)
    tail = jnp.dot(x, w_ref[:, c2:c2 + 2 * LANES], preferred_element_type=F32)
    kr_ref[...] = _rope64(tail[:, :LANES], cos, slo, shi).astype(kr_ref.dtype)
    gate_ref[...] = jax.nn.sigmoid(tail[:, LANES:])
    for h in range(MLA_HEADS):
        c0 = h * MXU_WIDTH_V7X
        z = jnp.dot(cqn, wuq_ref[:, c0:c0 + MXU_WIDTH_V7X], preferred_element_type=F32)
        q_ref[:, c0:c0 + LANES] = (z[:, :LANES] * scale).astype(q_ref.dtype)
        q_ref[:, c0 + LANES:c0 + 2 * LANES] = (_rope64(z[:, LANES:], cos, slo, shi) * scale).astype(q_ref.dtype)
    n_kv = wukv_ref.shape[1]
    for c0 in range(0, n_kv, MXU_WIDTH_V7X):
        kv_ref[:, c0:c0 + MXU_WIDTH_V7X] = jnp.dot(
            ckvn, wukv_ref[:, c0:c0 + MXU_WIDTH_V7X], preferred_element_type=F32).astype(kv_ref.dtype)


def _mla_in(xb, w, gq, gkv, wuq, wukv, cos, slo, shi, S):
    T, D = xb.shape
    tm = min(PROJ_TM, S)
    n_pos = S // tm
    tbl = pl.BlockSpec((tm, LANES), lambda i: (i % n_pos, 0))
    nq, nkv = wuq.shape[1], wukv.shape[1]
    return pl.pallas_call(
        functools.partial(_mla_in_kernel, scale=(MLA_NOPE + MLA_ROPE) ** -0.5),
        grid=(T // tm,),
        in_specs=[pl.BlockSpec((tm, D), lambda i: (i, 0)),
                  _resident(w.shape), _resident(gq.shape), _resident(gkv.shape),
                  _resident(wuq.shape), _resident(wukv.shape), tbl, tbl, tbl],
        out_specs=[pl.BlockSpec((tm, nq), lambda i: (i, 0)),
                   pl.BlockSpec((tm, nkv), lambda i: (i, 0)),
                   pl.BlockSpec((tm, LANES), lambda i: (i, 0)),
                   pl.BlockSpec((tm, LANES), lambda i: (i, 0))],
        out_shape=[jax.ShapeDtypeStruct((T, nq), BF16), jax.ShapeDtypeStruct((T, nkv), BF16),
                   jax.ShapeDtypeStruct((T, LANES), BF16), jax.ShapeDtypeStruct((T, LANES), F32)],
        compiler_params=_params(("parallel",)),
        name="mla_in",
    )(xb, w, gq, gkv, wuq, wukv, cos, slo, shi)


MASK_BIG = 16384.0


def _softmax_reset(m_s, l_s, acc_s):
    m_s[...] = jnp.full(m_s.shape, NEG_INF, F32)
    l_s[...] = jnp.zeros(l_s.shape, F32)
    acc_s[...] = jnp.zeros(acc_s.shape, F32)


def _online_softmax_step(s, v, m_s, l_s, acc_s):
    m_prev = m_s[...]
    m_new = jnp.maximum(m_prev, jnp.max(s, axis=-1, keepdims=True))
    alpha = jnp.exp(m_prev - m_new)
    p = jnp.exp(s - m_new)
    l_s[...] = alpha * l_s[...] + jnp.sum(p, axis=-1, keepdims=True)
    acc_s[...] = alpha * acc_s[...] + jnp.dot(p.astype(BF16), v, preferred_element_type=F32)
    m_s[...] = m_new


def _online_softmax_step_t(st, vt, m_s, l_s, acc_s):
    m_prev = m_s[...]
    m_new = jnp.maximum(m_prev, jnp.max(st, axis=0, keepdims=True))
    alpha = jnp.exp(m_prev - m_new)
    p = jnp.exp(st - m_new)
    l_s[...] = alpha * l_s[...] + jnp.sum(p, axis=0, keepdims=True)
    acc_s[...] = alpha * acc_s[...] + jnp.dot(vt, p.astype(BF16), preferred_element_type=F32)
    m_s[...] = m_new


def _dot_nt(a, b):
    return lax.dot_general(a, b, (((1,), (1,)), ((), ())), preferred_element_type=F32)


MLA_TQ = 512
MLA_TK = 512


MLA_HPS = 4


def _mla_attn_kernel(qt_ref, kn_ref, kr_ref, vt_ref, o_ref, m_s, l_s, acc_s, *, tq, tk, hps):
    q0 = pl.program_id(2) * tq
    qts = [qt_ref[0, hh * 2 * LANES:(hh + 1) * 2 * LANES, :] for hh in range(hps)]
    _softmax_reset(m_s, l_s, acc_s)

    def tile(j, causal):
        k0 = pl.multiple_of(j * tk, tk)
        kr = kr_ref[0, pl.ds(k0, tk), :]
        for hh in range(hps):
            hs = slice(hh * LANES, (hh + 1) * LANES)
            ka = jnp.concatenate([kn_ref[0, pl.ds(k0, tk), hs], kr], axis=1)
            st = jnp.dot(ka, qts[hh], preferred_element_type=F32)
            if causal:
                kpos = k0 + lax.broadcasted_iota(jnp.int32, (tk, tq), 0)
                qpos = q0 + lax.broadcasted_iota(jnp.int32, (tk, tq), 1)
                st = jnp.where(kpos <= qpos, st, -MASK_BIG)
            _online_softmax_step_t(st, vt_ref[0, hs, pl.ds(k0, tk)], m_s.at[hh], l_s.at[hh], acc_s.at[hh])

    j_diag = q0 // tk

    def body(j, carry):
        tile(j, False)
        return carry

    lax.fori_loop(0, j_diag, body, 0)
    tile(j_diag, True)
    for hh in range(hps):
        o_ref[0, :, hh * LANES:(hh + 1) * LANES] = jnp.transpose(acc_s[hh] / l_s[hh]).astype(o_ref.dtype)


def _mla_attention(qt, kv, kr, vt):
    B, _, S = qt.shape
    H = MLA_HEADS
    hps = MLA_HPS
    tq = tk = min(MLA_TQ, S)
    return pl.pallas_call(
        functools.partial(_mla_attn_kernel, tq=tq, tk=tk, hps=hps),
        grid=(B, H // hps, S // tq),
        in_specs=[pl.BlockSpec((1, hps * 2 * LANES, tq), lambda b, h, i: (b, h, i)),
                  pl.BlockSpec((1, S, hps * LANES), lambda b, h, i: (b, 0, h)),
                  pl.BlockSpec((1, S, LANES), lambda b, h, i: (b, 0, 0)),
                  pl.BlockSpec((1, hps * LANES, S), lambda b, h, i: (b, h, 0))],
        out_specs=pl.BlockSpec((1, tq, hps * LANES), lambda b, h, i: (b, i, h)),
        out_shape=jax.ShapeDtypeStruct((B, S, H * MLA_DV), BF16),
        scratch_shapes=[pltpu.VMEM((hps, 1, tq), F32), pltpu.VMEM((hps, 1, tq), F32),
                        pltpu.VMEM((hps, MLA_DV, tq), F32)],
        compiler_params=_params(("parallel", "parallel", "arbitrary")),
        name="mla_attn",
    )(qt, kv, kr, vt)


def _gelu_tanh(x):
    return 0.5 * x * (1.0 + jnp.tanh(math.sqrt(2.0 / math.pi) * (x + 0.044715 * (x * x * x))))


def _compress_kernel(z_ref, pe_ref, w1_ref, w2_ref, cos_ref, sin_ref, o_ref, *, hid):
    which = pl.program_id(1)
    z = z_ref[0, 0, 0]
    n_chunks = z.shape[0]
    cd = z.shape[1]
    ab = jnp.dot(z, w1_ref[0], preferred_element_type=F32)
    a_part = ab[:, :hid]
    b_next = pltpu.roll(ab[:, hid:], n_chunks - 1, 0)
    pe_a = jnp.dot(pe_ref[0, :, :cd], w1_ref[0, :, :hid], preferred_element_type=F32)
    pe_b = jnp.dot(pe_ref[0, :, cd:], w1_ref[0, :, hid:], preferred_element_type=F32)
    h = _gelu_tanh(a_part + b_next + pe_a[0:1] + pe_b[0:1])
    out = jnp.dot(h.astype(BF16), w2_ref[0], preferred_element_type=F32)
    roped = _rope128(out, cos_ref[...], sin_ref[...])
    o_ref[0, 0, 0] = jnp.where(which == 0, roped, out).astype(o_ref.dtype)


def _nsa_compress(zc, pe, w1, w2, cos_c, sin_c):
    B, _, G, n_chunks, cd = zc.shape
    hid = w2.shape[1]
    d = w2.shape[2]
    return pl.pallas_call(
        functools.partial(_compress_kernel, hid=hid),
        grid=(B, 2, G),
        in_specs=[pl.BlockSpec((1, 1, 1, n_chunks, cd), lambda b, t, g: (b, t, g, 0, 0)),
                  pl.BlockSpec((1, 8, 2 * cd), lambda b, t, g: (t, 0, 0)),
                  pl.BlockSpec((1, cd, 2 * hid), lambda b, t, g: (t, 0, 0)),
                  pl.BlockSpec((1, hid, d), lambda b, t, g: (t, 0, 0)),
                  pl.BlockSpec((n_chunks, LANES), lambda b, t, g: (0, 0)),
                  pl.BlockSpec((n_chunks, LANES), lambda b, t, g: (0, 0))],
        out_specs=pl.BlockSpec((1, 1, 1, n_chunks, d), lambda b, t, g: (b, t, g, 0, 0)),
        out_shape=jax.ShapeDtypeStruct((B, 2, G, n_chunks, d), BF16),
        compiler_params=_params(("parallel", "parallel", "parallel")),
        name="nsa_compress",
    )(zc, pe, w1, w2, cos_c, sin_c)


NSA_TQ = 256
NSA_TK = 512


def _nsa_kernel(qt_ref, g_ref, kc_ref, vct_ref, ks_ref, vst_ref, kw_ref, vwt_ref, o_ref,
                m_s, l_s, acc_s, out_s, *, tq, tk, hpg, dk, n_cmp, n_slc):
    q0 = pl.program_id(2) * tq
    cols = hpg * tq
    i32 = jnp.int32
    qt = jnp.concatenate([qt_ref[0, h * dk:(h + 1) * dk, :] for h in range(hpg)], axis=1)
    gates = g_ref[0, 0]

    def gate_row(branch):
        return jnp.concatenate([gates[3 * h + branch:3 * h + branch + 1, :] for h in range(hpg)], axis=1)

    def qpos_of(n_rows):
        return q0 + lax.rem(lax.broadcasted_iota(i32, (n_rows, cols), 1), tq)

    n_pad = kc_ref.shape[3]
    sc = jnp.dot(kc_ref[0, 0, 0], qt, preferred_element_type=F32)
    n_io = lax.broadcasted_iota(i32, (n_pad, cols), 0)
    vis = (n_io * CMP_STRIDE + (CMP_BLK - 1) <= qpos_of(n_pad)) & (n_io < n_cmp)
    sc = jnp.where(vis, sc, NEG_INF)
    e = jnp.where(vis, jnp.exp(sc - jnp.max(sc, axis=0, keepdims=True)), 0.0)
    p = e / jnp.maximum(jnp.sum(e, axis=0, keepdims=True), 1e-30)
    o_cmp = jnp.dot(vct_ref[0, 0], p.astype(BF16), preferred_element_type=F32)
    out_s[...] = gate_row(0) * o_cmp

    imp = p[:, 0:tq]
    for h in range(1, hpg):
        imp = imp + p[:, h * tq:(h + 1) * tq]
    ratio = SLC_BLK // CMP_STRIDE
    span = CMP_BLK // CMP_STRIDE
    j_p = lax.broadcasted_iota(i32, (n_slc, n_pad), 0)
    c_p = lax.broadcasted_iota(i32, (n_slc, n_pad), 1)
    lo_c = ratio * j_p - (span - 1)
    pool = jnp.where((c_p >= lo_c) & (c_p < lo_c + ratio + span - 1), 1.0, 0.0).astype(BF16)
    hi = imp.astype(BF16)
    r1 = imp - hi.astype(F32)
    mid = r1.astype(BF16)
    lo = (r1 - mid.astype(F32)).astype(BF16)
    p_slc = (jnp.dot(pool, hi, preferred_element_type=F32) + jnp.dot(pool, mid, preferred_element_type=F32)
             + jnp.dot(pool, lo, preferred_element_type=F32))

    j_io = lax.broadcasted_iota(i32, (n_slc, tq), 0)
    qpos_l = q0 + lax.broadcasted_iota(i32, (n_slc, tq), 1)
    cur = qpos_l // SLC_BLK
    forced = (j_io == 0) | (j_io == cur) | (j_io == cur - 1)
    valid = j_io * SLC_BLK <= qpos_l
    score = jnp.where(forced, FORCE_SCORE, jnp.where(valid, p_slc, -FORCE_SCORE))
    bias = jnp.where(_rank_before(score) < N_SEL, 0.0, -MASK_BIG)
    if n_slc < LANES:
        bias = jnp.concatenate([bias, jnp.full((LANES - n_slc, tq), -MASK_BIG, F32)], axis=0)
    bias = bias.astype(BF16)
    qa = jnp.concatenate([qt, jnp.concatenate([bias] * hpg, axis=1)], axis=0)

    qpos_k = qpos_of(tk)
    k_io = lax.broadcasted_iota(i32, (tk, cols), 0)

    _softmax_reset(m_s, l_s, acc_s)

    def slc_tile(j, causal):
        k0 = pl.multiple_of(j * tk, tk)
        kblk = (k0 + lax.broadcasted_iota(i32, (tk, LANES), 0)) // SLC_BLK
        onehot = jnp.where(kblk == lax.broadcasted_iota(i32, (tk, LANES), 1), 1.0, 0.0).astype(BF16)
        ka = jnp.concatenate([ks_ref[0, pl.ds(k0, tk), :], onehot], axis=1)
        s = jnp.dot(ka, qa, preferred_element_type=F32)
        if causal:
            s = jnp.where(k0 + k_io <= qpos_k, s, -MASK_BIG)
        _online_softmax_step_t(s, vst_ref[0, :, pl.ds(k0, tk)], m_s, l_s, acc_s)

    j_diag = q0 // tk

    def body(j, carry):
        slc_tile(j, False)
        return carry

    lax.fori_loop(0, j_diag, body, 0)
    slc_tile(j_diag, True)
    out_s[...] += gate_row(1) * (acc_s[...] / l_s[...])

    _softmax_reset(m_s, l_s, acc_s)

    def win_tile(j):
        k0 = pl.multiple_of(j * tk, tk)
        s = jnp.dot(kw_ref[0, pl.ds(k0, tk), :], qt, preferred_element_type=F32)
        kpos = k0 + k_io
        s = jnp.where((kpos <= qpos_k) & (kpos > qpos_k - WIN), s, -MASK_BIG)
        _online_softmax_step_t(s, vwt_ref[0, :, pl.ds(k0, tk)], m_s, l_s, acc_s)

    @pl.when(j_diag >= 1)
    def _():
        win_tile(j_diag - 1)

    win_tile(j_diag)
    out = out_s[...] + gate_row(2) * (acc_s[...] / l_s[...])
    for h in range(hpg):
        o_ref[0, :, h * dk:(h + 1) * dk] = jnp.transpose(out[:, h * tq:(h + 1) * tq]).astype(o_ref.dtype)


def _nsa_attention(qkv, qt, gates_t, cmp_kv, vct, vt):
    B, S, _ = qkv.shape
    G, hpg, dk = NSA_GROUPS, NSA_HPG, NSA_DK
    n_pad = cmp_kv.shape[3]
    n_cmp = (S - CMP_BLK) // CMP_STRIDE + 1
    tq, tk = min(NSA_TQ, S), min(NSA_TK, S)
    assert tk == WIN and tq <= tk
    cols = hpg * tq
    kv0 = NSA_HEADS

    def k_spec(branch):
        base = kv0 + 2 * G * branch
        return pl.BlockSpec((1, S, dk), lambda b, g, i: (b, 0, base + g))

    def vt_spec(branch):
        return pl.BlockSpec((1, dk, S), lambda b, g, i: (b, NSA_HEADS + (branch - 1) * G + g, 0))

    return pl.pallas_call(
        functools.partial(_nsa_kernel, tq=tq, tk=tk, hpg=hpg, dk=dk, n_cmp=n_cmp, n_slc=S // SLC_BLK),
        grid=(B, G, S // tq),
        in_specs=[pl.BlockSpec((1, hpg * dk, tq), lambda b, g, i: (b, g, i)),
                  pl.BlockSpec((1, 1, LANES, tq), lambda b, g, i: (b, g, 0, i)),
                  pl.BlockSpec((1, 1, 1, n_pad, dk), lambda b, g, i: (b, 0, g, 0, 0)),
                  pl.BlockSpec((1, 1, dk, n_pad), lambda b, g, i: (b, g, 0, 0)),
                  k_spec(1), vt_spec(1), k_spec(2), vt_spec(2)],
        out_specs=pl.BlockSpec((1, tq, hpg * dk), lambda b, g, i: (b, i, g)),
        out_shape=jax.ShapeDtypeStruct((B, S, NSA_HEADS * NSA_DV), BF16),
        scratch_shapes=[pltpu.VMEM((1, cols), F32), pltpu.VMEM((1, cols), F32),
                        pltpu.VMEM((dk, cols), F32), pltpu.VMEM((dk, cols), F32)],
        compiler_params=_params(("parallel", "parallel", "arbitrary")),
        name="nsa_attn",
    )(qt, gates_t, cmp_kv, vct, qkv, vt, qkv, vt)


MOE_TM = 256
MOE_FFN_TM = 256


def _hidden_chunks(f):
    chunks, c0 = [], 0
    while c0 < f:
        cs = min(MXU_WIDTH_V7X, f - c0)
        chunks.append((c0, cs))
        c0 += cs
    return tuple(chunks)


def _swiglu(x, wgu_ref, wd_ref, h_s, chunks):
    for c0, cs in chunks:
        gu = jnp.dot(x, wgu_ref[0, :, 2 * c0:2 * c0 + 2 * cs], preferred_element_type=F32)
        g, u = gu[:, :cs], gu[:, cs:]
        h_s[:, c0:c0 + cs] = (g * jax.nn.sigmoid(g) * u).astype(BF16)
    return jnp.dot(h_s[...], wd_ref[0], preferred_element_type=F32)


W_CHUNK_ROWS = 128
W_STAGE_SLOTS = 4
MOE_FFN_VMEM_LIMIT_BYTES = 56 * 1024 * 1024


def _moe_ffn_kernel(blk_e_ref, n_used_ref, first_ref, nxt_ref, slot_ref, clo_ref, chi_ref,
                    x_ref, wg_hbm, wu_hbm, wd_hbm, *rest, chunks, has_prev):
    o_ref, wgu_buf, wd_buf, stg_gu, stg_d, sem_gu, sem_d, h_s = rest[1:] if has_prev else rest
    i = pl.program_id(0)
    rows = W_CHUNK_ROWS
    n_g = wg_hbm.shape[1] // rows
    n_gu = 2 * n_g
    n_d = wd_hbm.shape[1] // rows

    def gu_copy(src, e, r0, s):
        return pltpu.make_async_copy(src.at[e, pl.ds(r0, rows), :], stg_gu.at[s], sem_gu.at[s])

    def d_copy(e, r0, s):
        return pltpu.make_async_copy(wd_hbm.at[e, pl.ds(r0, rows), :], stg_d.at[s], sem_d.at[s])

    def gu_start(e, c):
        s = lax.rem(c, W_STAGE_SLOTS)

        @pl.when(c < n_g)
        def _():
            gu_copy(wg_hbm, e, pl.multiple_of(c * rows, rows), s).start()

        @pl.when(c >= n_g)
        def _():
            gu_copy(wu_hbm, e, pl.multiple_of((c - n_g) * rows, rows), s).start()

    def d_start(e, c):
        d_copy(e, pl.multiple_of(c * rows, rows), lax.rem(c, W_STAGE_SLOTS)).start()

    def start_prefetch(e):
        for c in range(W_STAGE_SLOTS):
            gu_start(e, jnp.int32(c))
            d_start(e, jnp.int32(c))

    def convert(e, dst, lo, hi):
        def gu_body(c, carry):
            s = lax.rem(c, W_STAGE_SLOTS)
            gu_copy(wg_hbm, 0, 0, s).wait()
            r0 = pl.multiple_of(lax.rem(c, n_g) * rows, rows)

            @pl.when(c < n_g)
            def _():
                for c0, cs in chunks:
                    wgu_buf[dst, pl.ds(r0, rows), 2 * c0:2 * c0 + cs] = stg_gu[s, :, c0:c0 + cs].astype(BF16)

            @pl.when(c >= n_g)
            def _():
                for c0, cs in chunks:
                    wgu_buf[dst, pl.ds(r0, rows), 2 * c0 + cs:2 * c0 + 2 * cs] = stg_gu[s, :, c0:c0 + cs].astype(BF16)

            @pl.when(c + W_STAGE_SLOTS < n_gu)
            def _():
                gu_start(e, c + W_STAGE_SLOTS)

            return carry

        lax.fori_loop(jnp.clip(lo, 0, n_gu), jnp.clip(hi, 0, n_gu), gu_body, 0)

        def d_body(c, carry):
            s = lax.rem(c, W_STAGE_SLOTS)
            d_copy(0, 0, s).wait()
            wd_buf[dst, pl.ds(pl.multiple_of(c * rows, rows), rows), :] = stg_d[s].astype(BF16)

            @pl.when(c + W_STAGE_SLOTS < n_d)
            def _():
                d_start(e, c + W_STAGE_SLOTS)

            return carry

        lax.fori_loop(jnp.clip(lo - n_gu, 0, n_d), jnp.clip(hi - n_gu, 0, n_d), d_body, 0)

    @pl.when(i == 0)
    def _():
        start_prefetch(blk_e_ref[0])
        convert(blk_e_ref[0], 0, 0, n_gu + n_d)

    used = i < n_used_ref[0]

    @pl.when(used)
    def _():
        nxt = nxt_ref[i]
        cur = slot_ref[i]

        @pl.when((first_ref[i] == 1) & (nxt >= 0))
        def _():
            start_prefetch(nxt)

        o_ref[...] = _swiglu(x_ref[...], wgu_buf.at[pl.ds(cur, 1)], wd_buf.at[pl.ds(cur, 1)], h_s,
                             chunks).astype(o_ref.dtype)

        @pl.when(nxt >= 0)
        def _():
            convert(nxt, 1 - cur, clo_ref[i], chi_ref[i])

    @pl.when(jnp.logical_not(used))
    def _():
        o_ref[...] = jnp.zeros(o_ref.shape, o_ref.dtype)


def _pack_gate_up_kernel(wg_ref, wu_ref, o_ref, *, chunks):
    for c0, cs in chunks:
        o_ref[0, :, 2 * c0:2 * c0 + cs] = wg_ref[0, :, c0:c0 + cs].astype(o_ref.dtype)
        o_ref[0, :, 2 * c0 + cs:2 * c0 + 2 * cs] = wu_ref[0, :, c0:c0 + cs].astype(o_ref.dtype)


PACK_ROWS = 512


def _pack_gate_up(w_gate, w_up):
    E, D, F = w_gate.shape
    rows = min(PACK_ROWS, D)
    spec = pl.BlockSpec((1, rows, F), lambda e, r: (e, r, 0))
    return pl.pallas_call(
        functools.partial(_pack_gate_up_kernel, chunks=_hidden_chunks(F)),
        grid=(E, D // rows),
        in_specs=[spec, spec],
        out_specs=pl.BlockSpec((1, rows, 2 * F), lambda e, r: (e, r, 0)),
        out_shape=jax.ShapeDtypeStruct((E, D, 2 * F), BF16),
        compiler_params=_params(("parallel", "parallel")),
        name="pack_gate_up",
    )(w_gate, w_up)


def _stream_schedule(blk_e, n_used, n_chunks):
    i32 = jnp.int32
    n = blk_e.shape[0]
    idx = jnp.arange(n, dtype=i32)
    used = idx < n_used
    prev_e = jnp.concatenate([jnp.full((1,), -1, i32), blk_e[:-1]])
    first_b = used & (blk_e != prev_e)
    run_start = lax.cummax(jnp.where(first_b, idx, 0), axis=0)
    nxt_incl = lax.cummin(jnp.where(first_b, idx, n), axis=0, reverse=True)
    nxt_start = jnp.concatenate([nxt_incl[1:], jnp.full((1,), n, i32)])
    has_next = used & (nxt_start < n_used)
    nxt_e = jnp.sum(jnp.where(idx[None, :] == nxt_start[:, None], blk_e[None, :], 0), axis=1)
    nxt_e = jnp.where(has_next, nxt_e, -1).astype(i32)
    n_run = jnp.maximum(jnp.minimum(nxt_start, n_used) - run_start, 1)
    step = idx - run_start
    c_lo = (n_chunks * step // n_run).astype(i32)
    c_hi = (n_chunks * (step + 1) // n_run).astype(i32)
    slot = lax.rem(jnp.cumsum(first_b.astype(i32)) - 1, 2).astype(i32)
    return first_b.astype(i32), nxt_e, slot, c_lo, c_hi


MOE_CALLS = 4


def _moe_routed(xb, eid_t, rank_t, counts, w_gate, w_up, w_down):
    T, D = xb.shape
    K = eid_t.shape[0]
    E, F, _ = w_down.shape
    tm = MOE_FFN_TM
    n = T * K
    n_blk = -(-n // tm) + E
    i32 = jnp.int32

    padded = (counts + tm - 1) // tm * tm
    pad_end = jnp.cumsum(padded)
    pad_start = pad_end - padded
    start_t = jnp.zeros_like(eid_t)
    for e in range(E):
        start_t = jnp.where(eid_t == e, pad_start[e], start_t)
    dest = (start_t + rank_t).reshape(n)
    tok = jnp.zeros((n_blk * tm,), i32).at[dest].set(jnp.tile(jnp.arange(T, dtype=i32), K),
                                                     unique_indices=True, mode="promise_in_bounds")
    n_used = (pad_end[-1] // tm).astype(i32).reshape(1)
    blk_start = jnp.arange(n_blk, dtype=i32) * tm
    blk_e = jnp.minimum(jnp.sum((pad_end[None, :] <= blk_start[:, None]).astype(i32), axis=1), E - 1)

    n_chunks = (2 * D + F) // W_CHUNK_ROWS
    n_call = n_blk // MOE_CALLS
    hbm = pl.BlockSpec(memory_space=pl.ANY)
    scratch = [pltpu.VMEM((2, D, 2 * F), BF16), pltpu.VMEM((2, F, D), BF16),
               pltpu.VMEM((W_STAGE_SLOTS, W_CHUNK_ROWS, F), F32), pltpu.VMEM((W_STAGE_SLOTS, W_CHUNK_ROWS, D), F32),
               pltpu.SemaphoreType.DMA((W_STAGE_SLOTS,)), pltpu.SemaphoreType.DMA((W_STAGE_SLOTS,)),
               pltpu.VMEM((tm, F), BF16)]
    y = None
    for c in range(MOE_CALLS):
        b0 = c * n_call
        blk_e_c = blk_e[b0:b0 + n_call]
        n_used_c = jnp.clip(n_used - b0, 0, n_call)
        sched = _stream_schedule(blk_e_c, n_used_c[0], n_chunks)
        xs = xb.at[tok[b0 * tm:(b0 + n_call) * tm]].get(mode="promise_in_bounds")
        prev = () if y is None else (y,)
        y = pl.pallas_call(
            functools.partial(_moe_ffn_kernel, chunks=_hidden_chunks(F), has_prev=y is not None),
            grid_spec=pltpu.PrefetchScalarGridSpec(
                num_scalar_prefetch=7,
                grid=(n_call,),
                in_specs=[pl.BlockSpec((tm, D), lambda i, *_: (i, 0)), hbm, hbm, hbm] + [hbm] * len(prev),
                out_specs=pl.BlockSpec((tm, D), lambda i, *_, b0=b0: (i + b0, 0)),
                scratch_shapes=scratch),
            out_shape=jax.ShapeDtypeStruct((n_blk * tm, D), BF16),
            input_output_aliases={11: 0} if prev else {},
            compiler_params=pltpu.CompilerParams(dimension_semantics=("arbitrary",),
                                                 vmem_limit_bytes=MOE_FFN_VMEM_LIMIT_BYTES),
            name="moe_ffn",
        )(blk_e_c, n_used_c, *sched, xs, w_gate, w_up, w_down, *prev)

    return y.at[dest].get(mode="promise_in_bounds").reshape(K, T, D)


def _layernorm_rows(z, g, b):
    mu = jnp.mean(z, -1, keepdims=True)
    zc = z - mu
    var = jnp.mean(jnp.square(zc), -1, keepdims=True)
    return zc * lax.rsqrt(var + LN_EPS) * g + b


def _merge_kernel(x_ref, yn_ref, ym_ref, wg_ref, wbn_ref, wbm_ref, o_ref):
    x, yn, ym = x_ref[...], yn_ref[...], ym_ref[...]
    d = o_ref.shape[1]
    for c0 in range(0, d, MXU_WIDTH_V7X):
        cs = slice(c0, c0 + MXU_WIDTH_V7X)
        gn = jax.nn.sigmoid(jnp.dot(x, wg_ref[:, cs], preferred_element_type=F32))
        gm = jax.nn.sigmoid(jnp.dot(x, wg_ref[:, d + c0:d + c0 + MXU_WIDTH_V7X], preferred_element_type=F32))
        bn = jnp.dot(yn, wbn_ref[:, cs], preferred_element_type=F32)
        bm = jnp.dot(ym, wbm_ref[:, cs], preferred_element_type=F32)
        o_ref[:, cs] = (gn * bn + gm * bm).astype(o_ref.dtype)


def _merge(xb, y_nsa, y_mla, w_g, wb_nsa, wb_mla):
    T, D = xb.shape
    tm = PROJ_TM
    row = lambda a: pl.BlockSpec((tm, a.shape[1]), lambda i: (i, 0))
    return pl.pallas_call(
        _merge_kernel,
        grid=(T // tm,),
        in_specs=[row(xb), row(y_nsa), row(y_mla), _resident(w_g.shape), _resident(wb_nsa.shape),
                  _resident(wb_mla.shape)],
        out_specs=pl.BlockSpec((tm, D), lambda i: (i, 0)),
        out_shape=jax.ShapeDtypeStruct((T, D), BF16),
        compiler_params=_params(("parallel",)),
        name="merge",
    )(xb, y_nsa, y_mla, w_g, wb_nsa, wb_mla)


def _rank_before(v):
    n = v.shape[0]
    j_io = lax.broadcasted_iota(jnp.int32, v.shape, 0)
    cnt = jnp.zeros(v.shape, jnp.int32)
    for i in range(n):
        row = v[i:i + 1, :]
        cnt = cnt + ((row > v) | ((row == v) & (j_io > i))).astype(jnp.int32)
    return cnt


def _route_tile(logits_t, rb, cnt_s):
    E, tm = logits_t.shape
    per = E // N_EXPERT_GROUPS
    scores = jax.nn.sigmoid(logits_t)
    biased = scores + rb
    grp = []
    for g in range(N_EXPERT_GROUPS):
        blk = biased[g * per:(g + 1) * per, :]
        m1 = jnp.max(blk, axis=0, keepdims=True)
        eq = blk == m1
        n_eq = jnp.sum(eq.astype(F32), axis=0, keepdims=True)
        m2 = jnp.max(jnp.where(eq, NEG_INF, blk), axis=0, keepdims=True)
        grp.append(m1 + jnp.where(n_eq >= 2.0, m1, m2))
    gsel = _rank_before(jnp.concatenate(grp, axis=0)) < TOPK_GROUPS
    masked = jnp.concatenate(
        [jnp.where(gsel[g:g + 1, :], biased[g * per:(g + 1) * per, :], NEG_INF) for g in range(N_EXPERT_GROUPS)],
        axis=0)
    sel = _rank_before(masked) < TOP_K
    w_sel = jnp.where(sel, scores, 0.0)
    w_sel = w_sel / jnp.sum(w_sel, axis=0, keepdims=True) * ROUTED_SCALE
    tri = jnp.where(lax.broadcasted_iota(jnp.int32, (E, E), 1) <= lax.broadcasted_iota(jnp.int32, (E, E), 0),
                    1.0, 0.0).astype(BF16)
    sel_b = jnp.where(sel, 1.0, 0.0).astype(BF16)
    slot = jnp.dot(tri, sel_b, preferred_element_type=F32)
    upper = jnp.where(lax.broadcasted_iota(jnp.int32, (tm, tm), 0) <= lax.broadcasted_iota(jnp.int32, (tm, tm), 1),
                      1.0, 0.0).astype(BF16)
    incl = jnp.dot(sel_b, upper, preferred_element_type=F32)
    pos = cnt_s[...] + incl - 1.0
    cnt_s[...] = cnt_s[...] + incl[:, tm - 1:tm]
    e_io = lax.broadcasted_iota(jnp.int32, (E, tm), 0)
    ids, wts, ranks = [], [], []
    for k in range(TOP_K):
        mk = sel & (slot == float(k + 1))
        ids.append(jnp.sum(jnp.where(mk, e_io, 0), axis=0, keepdims=True))
        wts.append(jnp.sum(jnp.where(mk, w_sel, 0.0), axis=0, keepdims=True))
        ranks.append(jnp.sum(jnp.where(mk, pos, 0.0), axis=0, keepdims=True).astype(jnp.int32))
    pad = 8 - TOP_K
    ids.append(jnp.zeros((pad, tm), jnp.int32))
    wts.append(jnp.zeros((pad, tm), F32))
    ranks.append(jnp.zeros((pad, tm), jnp.int32))
    return jnp.concatenate(ids, axis=0), jnp.concatenate(wts, axis=0), jnp.concatenate(ranks, axis=0)


def _out_ln1_kernel(m_ref, x_ref, wo_ref, g_ref, b_ref, wrt_ref, rb_ref,
                    x1_ref, x1b_ref, eid_ref, wts_ref, rank_ref, cnt_ref, cnt_s):
    @pl.when(pl.program_id(0) == 0)
    def _():
        cnt_s[...] = jnp.zeros(cnt_s.shape, F32)

    m = m_ref[...]
    d = x1_ref.shape[1]
    for c0 in range(0, d, MXU_WIDTH_V7X):
        cs = slice(c0, c0 + MXU_WIDTH_V7X)
        x1_ref[:, cs] = DN_ALPHA * x_ref[:, cs] + jnp.dot(m, wo_ref[:, cs], preferred_element_type=F32)
    x1 = _layernorm_rows(x1_ref[...], g_ref[...], b_ref[...])
    x1_ref[...] = x1
    x1b = x1.astype(BF16)
    x1b_ref[...] = x1b
    logits_t = _dot_nt(wrt_ref[...], x1b)
    eid_ref[...], wts_ref[...], rank_ref[...] = _route_tile(logits_t, rb_ref[...], cnt_s)
    cnt_ref[...] = cnt_s[...]


def _out_ln1(merged, x2, w_out, g, b, w_router_t, router_b):
    T, D = x2.shape
    E = w_router_t.shape[0]
    tm = MOE_TM
    row = lambda w: pl.BlockSpec((tm, w), lambda i: (i, 0))
    col = pl.BlockSpec((8, tm), lambda i: (0, i))
    return pl.pallas_call(
        _out_ln1_kernel,
        grid=(T // tm,),
        in_specs=[row(D), row(D), _resident(w_out.shape), _resident(g.shape), _resident(b.shape),
                  _resident(w_router_t.shape), _resident(router_b.shape)],
        out_specs=[row(D), row(D), col, col, col, pl.BlockSpec((E, 1), lambda i: (0, 0))],
        out_shape=[jax.ShapeDtypeStruct((T, D), F32), jax.ShapeDtypeStruct((T, D), BF16),
                   jax.ShapeDtypeStruct((8, T), jnp.int32), jax.ShapeDtypeStruct((8, T), F32),
                   jax.ShapeDtypeStruct((8, T), jnp.int32), jax.ShapeDtypeStruct((E, 1), F32)],
        scratch_shapes=[pltpu.VMEM((E, 1), F32)],
        compiler_params=_params(("arbitrary",)),
        name="out_ln1",
    )(merged, x2, w_out, g, b, w_router_t, router_b)


def _shared_ffn_kernel(x_ref, sgu_ref, sd_ref, o_ref, h_s, *, chunks):
    o_ref[...] = _swiglu(x_ref[...], sgu_ref, sd_ref, h_s, chunks)


def _shared_ffn(x1b, sh_gate_up, sh_down):
    T, D = x1b.shape
    tm = PROJ_TM
    F = sh_down.shape[1]
    return pl.pallas_call(
        functools.partial(_shared_ffn_kernel, chunks=_hidden_chunks(F)),
        grid=(T // tm,),
        in_specs=[pl.BlockSpec((tm, D), lambda i: (i, 0)), _resident(sh_gate_up.shape), _resident(sh_down.shape)],
        out_specs=pl.BlockSpec((tm, D), lambda i: (i, 0)),
        out_shape=jax.ShapeDtypeStruct((T, D), F32),
        scratch_shapes=[pltpu.VMEM((tm, F), BF16)],
        compiler_params=_params(("parallel",)),
        name="shared_ffn",
    )(x1b, sh_gate_up, sh_down)


def _ffn_ln2_kernel(x1_ref, sh_ref, yk_ref, w_ref, g_ref, b_ref, x2_ref, x2b_ref):
    acc = sh_ref[...]
    w = w_ref[...]
    for k in range(yk_ref.shape[0]):
        acc = acc + yk_ref[k].astype(F32) * w[:, k:k + 1]
    x2 = _layernorm_rows(DN_ALPHA * x1_ref[...] + acc, g_ref[...], b_ref[...])
    x2_ref[...] = x2
    x2b_ref[...] = x2.astype(BF16)


def _ffn_ln2(x1, shared, yk, wts, g, b):
    T, D = x1.shape
    K = yk.shape[0]
    tm = MOE_TM
    row = lambda w: pl.BlockSpec((tm, w), lambda i: (i, 0))
    return pl.pallas_call(
        _ffn_ln2_kernel,
        grid=(T // tm,),
        in_specs=[row(D), row(D), pl.BlockSpec((K, tm, D), lambda i: (0, i, 0)), row(wts.shape[1]),
                  _resident(g.shape), _resident(b.shape)],
        out_specs=[row(D), row(D)],
        out_shape=[jax.ShapeDtypeStruct((T, D), F32), jax.ShapeDtypeStruct((T, D), BF16)],
        compiler_params=_params(("parallel",)),
        name="ffn_ln2",
    )(x1, shared, yk, wts, g, b)


def _ple_kernel(x2_ref, x2b_ref, p_ref, wpg_ref, wpp_ref, o_ref):
    x2b = x2b_ref[...]
    pb = p_ref[...].astype(BF16)
    d = o_ref.shape[1]
    for c0 in range(0, d, MXU_WIDTH_V7X):
        cs = slice(c0, c0 + MXU_WIDTH_V7X)
        gate = jax.nn.sigmoid(jnp.dot(x2b, wpg_ref[:, cs], preferred_element_type=F32))
        pp = jnp.dot(pb, wpp_ref[:, cs], preferred_element_type=F32)
        o_ref[:, cs] = x2_ref[:, cs] + gate * pp


def _ple(x2, x2b, p2, w_pg, w_pp):
    T, D = x2.shape
    tm = PROJ_TM
    row = lambda w: pl.BlockSpec((tm, w), lambda i: (i, 0))
    return pl.pallas_call(
        _ple_kernel,
        grid=(T // tm,),
        in_specs=[row(D), row(D), row(p2.shape[1]), _resident(w_pg.shape), _resident(w_pp.shape)],
        out_specs=row(D),
        out_shape=jax.ShapeDtypeStruct((T, D), F32),
        compiler_params=_params(("parallel",)),
        name="ple",
    )(x2, x2b, p2, w_pg, w_pp)


def _attention_branches(xb, S, w_in, pe_k, pe_v, wk1, wk2, wv1, wv2, q_norm, w_uq, kv_norm, w_uk, w_uv):
    T, D = xb.shape
    B = T // S
    c0 = W_NSA_Q
    c1 = c0 + W_NSA_KV
    c2 = c1 + W_NSA_GATE
    c3 = c2 + W_MLA_CQ
    c4 = c3 + W_MLA_CKV
    c5 = c4 + W_MLA_KR
    G, dk = NSA_GROUPS, NSA_DK
    pos = jnp.arange(S)
    cos128, sin128 = _rope_tables_128(pos)
    cos64, slo64, shi64 = _rope_tables_64(pos)

    qkv, qv_t = _nsa_proj(xb, w_in[:, :c1].astype(BF16), cos128, sin128, S)

    pad_kr = jnp.zeros((D, LANES - W_MLA_KR), w_in.dtype)
    pad_g = jnp.zeros((D, LANES - W_NSA_GATE), w_in.dtype)
    w_m = jnp.concatenate([w_in[:, c2:c4], w_in[:, c4:c5], pad_kr, w_in[:, c1:c2], pad_g], axis=1).astype(BF16)
    H = MLA_HEADS
    wq = w_uq.reshape(MLA_Q_RANK, H, MLA_NOPE + MLA_ROPE)
    wq = jnp.concatenate([wq, jnp.zeros((MLA_Q_RANK, H, 2 * LANES - MLA_NOPE - MLA_ROPE), wq.dtype)], axis=-1)
    wq = wq.reshape(MLA_Q_RANK, H * 2 * LANES).astype(BF16)
    wukv = jnp.concatenate([w_uk, w_uv], axis=1).astype(BF16)
    q_m, kv_m, kr_m, gates = _mla_in(xb, w_m, q_norm.reshape(1, -1), kv_norm.reshape(1, -1), wq, wukv,
                                     cos64, slo64, shi64, S)

    n_chunks = S // CMP_STRIDE
    kvc = qkv[:, NSA_HEADS * dk:(NSA_HEADS + 2 * G) * dk]
    zc = kvc.reshape(B, n_chunks, CMP_STRIDE, 2, G, dk).transpose(0, 3, 4, 1, 2, 5)
    zc = zc.reshape(B, 2, G, n_chunks, CMP_STRIDE * dk)
    pe = jnp.stack([pe_k.reshape(-1), pe_v.reshape(-1)])
    pe = jnp.broadcast_to(pe[:, None, :], (2, 8, pe.shape[-1])).astype(BF16)
    half = CMP_STRIDE * dk
    w1 = jnp.stack([jnp.concatenate([wk1[:half], wk1[half:]], axis=1),
                    jnp.concatenate([wv1[:half], wv1[half:]], axis=1)]).astype(BF16)
    w2 = jnp.stack([wk2, wv2]).astype(BF16)
    pos_c = jnp.arange(n_chunks) * CMP_STRIDE + (CMP_BLK - 1)
    cos_c, sin_c = _rope_tables_128(pos_c)
    cmp_kv = _nsa_compress(zc, pe, w1, w2, cos_c, sin_c)

    g3 = gates[:, :W_NSA_GATE].reshape(B, S, G, NSA_HPG * 3).transpose(0, 2, 3, 1)
    g3 = jnp.pad(g3, ((0, 0), (0, 0), (0, LANES - NSA_HPG * 3), (0, 0)))

    qkv3 = qkv.reshape(B, S, -1)
    vct = jnp.swapaxes(cmp_kv[:, 1], 2, 3)
    y_nsa = _nsa_attention(qkv3, qv_t, g3, cmp_kv, vct, qv_t)
    kv_m = kv_m.reshape(B, S, -1)
    qt_m = jnp.swapaxes(q_m.reshape(B, S, -1), 1, 2)
    vt_m = jnp.swapaxes(kv_m[:, :, MLA_HEADS * MLA_NOPE:], 1, 2)
    y_mla = _mla_attention(qt_m, kv_m, kr_m.reshape(B, S, -1), vt_m)
    return y_nsa, y_mla


def kernel(x, p, w_in, nsa_pe_k, nsa_pe_v, nsa_cmp_k_w1, nsa_cmp_k_w2, nsa_cmp_v_w1, nsa_cmp_v_w2, mla_q_norm, mla_w_uq, mla_kv_norm, mla_w_uk, mla_w_uv, w_branch_nsa, w_branch_mla, w_out, ln1_g, ln1_b, router_w, router_b, exp_w_gate, exp_w_up, exp_w_down, sh_w_gate, sh_w_up, sh_w_down, ln2_g, ln2_b, ple_w_proj, ple_w_gate):
    B, S, D = x.shape
    c5 = D_IN - W_MERGE
    bs = B // BATCH_SPLITS
    T = bs * S
    for i in range(DEPTH):
        w_attn = (w_in[i], nsa_pe_k[i], nsa_pe_v[i], nsa_cmp_k_w1[i], nsa_cmp_k_w2[i], nsa_cmp_v_w1[i],
                  nsa_cmp_v_w2[i], mla_q_norm[i], mla_w_uq[i], mla_kv_norm[i], mla_w_uk[i], mla_w_uv[i])
        w_gm = w_in[i][:, c5:].astype(BF16)
        wb_nsa, wb_mla, wo = w_branch_nsa[i].astype(BF16), w_branch_mla[i].astype(BF16), w_out[i].astype(BF16)
        w_rt, b_rt = router_w[i].T.astype(BF16), router_b[i].reshape(N_EXPERTS, 1).astype(F32)
        sh_gu = _pack_gate_up(sh_w_gate[i][None], sh_w_up[i][None])
        sh_d = sh_w_down[i][None].astype(BF16)
        w_pg, w_pp = ple_w_gate[i].astype(BF16), ple_w_proj[i].astype(BF16)
        outs = []
        for h in range(BATCH_SPLITS):
            x2 = x[h * bs:(h + 1) * bs].reshape(T, D)
            p2 = p[i][h * bs:(h + 1) * bs].reshape(T, PLE_DIM)
            xb = x2.astype(BF16)
            y_nsa, y_mla = _attention_branches(xb, S, *w_attn)
            merged = _merge(xb, y_nsa.reshape(T, -1), y_mla.reshape(T, -1), w_gm, wb_nsa, wb_mla)
            x1, x1b, eid_t, wts_t, rank_t, counts = _out_ln1(
                merged, x2, wo, ln1_g[i].reshape(1, D), ln1_b[i].reshape(1, D), w_rt, b_rt)
            yk = _moe_routed(x1b, eid_t[:TOP_K], rank_t[:TOP_K], counts.reshape(N_EXPERTS).astype(jnp.int32),
                             exp_w_gate[i], exp_w_up[i], exp_w_down[i])
            x2n, x2b = _ffn_ln2(x1, _shared_ffn(x1b, sh_gu, sh_d), yk, wts_t.T, ln2_g[i].reshape(1, D),
                                ln2_b[i].reshape(1, D))
            outs.append(_ple(x2n, x2b, p2, w_pg, w_pp).reshape(bs, S, D))
        x = jnp.concatenate(outs, axis=0)
    return x
```
